```python
import math, functools
import jax, jax.numpy as jnp
from jax import lax
import numpy as np

D_MODEL = 1024
BATCH = 8
SEQ = 2048
DEPTH = 2
DEC_BATCH = 128
DEC_SEQ = 8
PAST_LEN = 2048
PAGE_SIZE = 128

D_CONV = D_MODEL // 2
CONV_WIDTH = 31
HEAD_DIM = 64
N_HEADS = (D_MODEL // 2) // HEAD_DIM
D_ATTN = N_HEADS * HEAD_DIM
KV_HEADS = 2
GROUP = N_HEADS // KV_HEADS
KV_DIM = 2 * KV_HEADS * HEAD_DIM
N_BRANCH = 3
IN_SIZES = (D_CONV, D_CONV, D_CONV, D_ATTN, KV_DIM, KV_DIM, KV_DIM, D_ATTN, N_BRANCH * N_HEADS)
D_IN = sum(IN_SIZES)
CMP_STRIDE = 16
CMP_LEN = 2 * CMP_STRIDE
CMP_HIDDEN = 128
SLC_BLOCK = 64
TOP_N = 8
WINDOW = 512
NUM_BUCKETS = 32
MAX_DISTANCE = 128
PLE_DIM = 256
Q_BLOCK = 128
EPS = 1e-6
NEG_INF = -1e30
FORCED_SCORE = 1e4
MASKED_SCORE = -1e4

kernel_name = 'hymba_conformer_nsa_decoder_step'


def rms_norm(x, g):
    xf = x.astype(jnp.float32)
    y = xf * lax.rsqrt(jnp.mean(xf * xf, axis=-1, keepdims=True) + EPS)
    return (y * g.astype(jnp.float32)).astype(x.dtype)


def layer_norm(x, g, b):
    xf = x.astype(jnp.float32)
    mu = jnp.mean(xf, axis=-1, keepdims=True)
    xc = xf - mu
    var = jnp.mean(xc * xc, axis=-1, keepdims=True)
    y = xc * lax.rsqrt(var + EPS) * g.astype(jnp.float32) + b.astype(jnp.float32)
    return y.astype(x.dtype)


def rel_bucket(dist):
    n = jnp.maximum(dist, 0)
    max_exact = NUM_BUCKETS // 2
    nf = jnp.maximum(n, 1).astype(jnp.float32)
    large = max_exact + (jnp.log(nf / max_exact) / math.log(MAX_DISTANCE / max_exact)
                         * (NUM_BUCKETS - max_exact)).astype(jnp.int32)
    large = jnp.minimum(large, NUM_BUCKETS - 1)
    return jnp.where(n < max_exact, n, large)


def masked_softmax(s, mask):
    p = jax.nn.softmax(jnp.where(mask, s, NEG_INF), axis=-1)
    return p * mask


def compress_blocks(x, pe, w1, w2):
    b, l = x.shape[:2]
    n_half = l // CMP_STRIDE
    halves = x[:, :n_half * CMP_STRIDE].reshape(b, n_half, CMP_STRIDE, KV_HEADS, HEAD_DIM)
    pe = pe.reshape(2, CMP_STRIDE, 1, HEAD_DIM)
    w1 = w1.reshape(2, CMP_STRIDE, HEAD_DIM, CMP_HIDDEN)
    hid = (jnp.einsum('bnsgd,sdh->bngh', halves[:, :-1] + pe[0], w1[0])
           + jnp.einsum('bnsgd,sdh->bngh', halves[:, 1:] + pe[1], w1[1]))
    return jnp.einsum('bngh,hd->bngd', jax.nn.silu(hid), w2)


def _chunks(a, n_chunks):
    b, q = a.shape[:2]
    return jnp.moveaxis(a.reshape(b, n_chunks, q // n_chunks, *a.shape[2:]), 1, 0)


def _unchunk(a):
    a = jnp.moveaxis(a, 0, 1)
    return a.reshape(a.shape[0], -1, *a.shape[3:])


def nsa_attention(q, gate_logits, kv_cmp, kv_slc, kv_win, win_start, cmp_pe, cmp_w1, cmp_w2, rel_bias):
    f32 = jnp.float32
    b, nq = q.shape[:2]
    l = kv_cmp.shape[1]
    scale = HEAD_DIM ** -0.5
    q_pos = (l - nq) + jnp.arange(nq, dtype=jnp.int32)
    qg = q.reshape(b, nq, KV_HEADS, GROUP, HEAD_DIM)
    tbl_g = rel_bias.astype(f32).reshape(NUM_BUCKETS, KV_HEADS, GROUP)
    tbl_t = jnp.transpose(tbl_g, (1, 0, 2))

    kc = compress_blocks(kv_cmp[:, :, 0], cmp_pe[0], cmp_w1[0], cmp_w2[0])
    vc = compress_blocks(kv_cmp[:, :, 1], cmp_pe[1], cmp_w1[1], cmp_w2[1])
    n_cmp = kc.shape[1]
    c_end = jnp.arange(n_cmp, dtype=jnp.int32) * CMP_STRIDE + (CMP_LEN - 1)
    c_dist = q_pos[:, None] - c_end[None, :]
    c_bias = jnp.transpose(tbl_g[rel_bucket(c_dist)], (0, 2, 3, 1))
    s_c = jnp.einsum('bqgrd,bcgd->bqgrc', qg, kc).astype(f32) * scale + c_bias
    p_cmp = masked_softmax(s_c, (c_dist >= 0)[:, None, None, :])
    o_cmp = jnp.einsum('bqgrc,bcgd->bqgrd', p_cmp.astype(vc.dtype), vc)

    n_slc = -(-l // SLC_BLOCK)
    top_n = min(TOP_N, n_slc)
    c_start = c_end - (CMP_LEN - 1)
    s_start = jnp.arange(n_slc, dtype=jnp.int32) * SLC_BLOCK
    cover = ((c_start[:, None] < s_start[None, :] + SLC_BLOCK)
             & (c_end[:, None] >= s_start[None, :])).astype(f32)
    imp = jnp.einsum('bqgrc,cj->bqgj', p_cmp, cover)
    cur = q_pos // SLC_BLOCK
    blk = jnp.arange(n_slc, dtype=jnp.int32)
    forced = (blk[None] == 0) | (blk[None] == cur[:, None]) | (blk[None] == cur[:, None] - 1)
    allowed = blk[None] <= cur[:, None]
    score = jnp.where(allowed[:, None], jnp.where(forced[:, None], FORCED_SCORE, imp), MASKED_SCORE)
    top_val, top_idx = lax.top_k(score, top_n)
    top_ok = top_val > 0.5 * MASKED_SCORE

    pad = n_slc * SLC_BLOCK - l
    kv_blk = jnp.pad(kv_slc, ((0, 0), (0, pad), (0, 0), (0, 0), (0, 0)))
    kv_blk = jnp.transpose(kv_blk.reshape(b, n_slc, SLC_BLOCK, 2, KV_HEADS, HEAD_DIM), (3, 0, 4, 1, 2, 5))
    k_blk, v_blk = kv_blk[0], kv_blk[1]
    kv_win_pad = jnp.pad(kv_win, ((0, 0), (WINDOW, 0), (0, 0), (0, 0), (0, 0)))
    b_ix = jnp.arange(b)[:, None, None, None]
    g_ix = jnp.arange(KV_HEADS)[None, None, :, None]

    def chunk_fn(args):
        qc, pc, ic, okc = args
        kg = k_blk[b_ix, g_ix, ic]
        vg = v_blk[b_ix, g_ix, ic]
        kpos = ic[..., None] * SLC_BLOCK + jnp.arange(SLC_BLOCK, dtype=jnp.int32)
        d = pc[None, :, None, None, None] - kpos
        bias = jnp.transpose(tbl_t[g_ix[..., None], rel_bucket(d)], (0, 1, 2, 5, 3, 4))
        s = jnp.einsum('bcgrd,bcgnkd->bcgrnk', qc, kg).astype(f32) * scale + bias
        m = ((d >= 0) & okc[..., None])[:, :, :, None]
        p = masked_softmax(s.reshape(*s.shape[:4], -1), m.reshape(*m.shape[:4], -1))
        o_s = jnp.einsum('bcgrnk,bcgnkd->bcgrd', p.reshape(s.shape).astype(vg.dtype), vg)
        c = pc.shape[0]
        kw = lax.dynamic_slice_in_dim(kv_win_pad, pc[0] - win_start, WINDOW + c, axis=1)
        wpos = pc[0] - WINDOW + jnp.arange(WINDOW + c, dtype=jnp.int32)
        wd = pc[:, None] - wpos[None, :]
        wm = (wd >= 0) & (wd < WINDOW) & (wpos[None, :] >= win_start)
        w_bias = jnp.transpose(tbl_g[rel_bucket(wd)], (0, 2, 3, 1))
        sw = jnp.einsum('bcgrd,bkgd->bcgrk', qc, kw[:, :, 0]).astype(f32) * scale + w_bias
        pw = masked_softmax(sw, wm[:, None, None, :])
        o_w = jnp.einsum('bcgrk,bkgd->bcgrd', pw.astype(kw.dtype), kw[:, :, 1])
        return o_s, o_w

    chunk = Q_BLOCK if nq % Q_BLOCK == 0 else nq
    n_chunks = nq // chunk
    o_slc, o_win = lax.map(chunk_fn, (_chunks(qg, n_chunks), q_pos.reshape(n_chunks, chunk),
                                      _chunks(top_idx, n_chunks), _chunks(top_ok, n_chunks)))
    o_slc, o_win = _unchunk(o_slc), _unchunk(o_win)

    g = jax.nn.sigmoid(gate_logits.astype(f32)).reshape(b, nq, N_BRANCH, KV_HEADS, GROUP, 1).astype(q.dtype)
    o = g[:, :, 0] * o_cmp + g[:, :, 1] * o_slc + g[:, :, 2] * o_win
    return o.reshape(b, nq, D_ATTN)


def mixer_layer(h, ple, conv_buf, past_cmp, past_slc, past_win, *, norm_g, w_in, conv_w, conv_b,
                conv_ln_g, conv_ln_b, cmp_pe, cmp_w1, cmp_w2, w_out, w_ple, w_ple_gate, rel_bias):
    b, nq, _ = h.shape
    u = rms_norm(h, norm_g)
    proj = jnp.einsum('bqd,de->bqe', u, w_in)
    splits = [int(s) for s in np.cumsum(IN_SIZES)[:-1]]
    glu_a, glu_b, z_conv, q, kv_c, kv_s, kv_w, z_attn, gate_logits = jnp.split(proj, splits, axis=-1)

    glu = glu_a * jax.nn.sigmoid(glu_b)
    xc = jnp.concatenate([conv_buf.astype(glu.dtype), glu], axis=1)
    dw = lax.conv_general_dilated(xc, conv_w[:, None, :].astype(xc.dtype), window_strides=(1,),
                                  padding='VALID', dimension_numbers=('NWC', 'WIO', 'NWC'),
                                  feature_group_count=D_CONV) + conv_b
    conv_out = jax.nn.silu(layer_norm(dw, conv_ln_g, conv_ln_b)) * jax.nn.silu(z_conv)

    def rows(a):
        return a.reshape(b, nq, 2, KV_HEADS, HEAD_DIM)
    kv_c, kv_s, kv_w = rows(kv_c), rows(kv_s), rows(kv_w)
    full_c = jnp.concatenate([past_cmp.astype(kv_c.dtype), kv_c], axis=1)
    full_s = jnp.concatenate([past_slc.astype(kv_s.dtype), kv_s], axis=1)
    win_all = jnp.concatenate([past_win.astype(kv_w.dtype), kv_w], axis=1)
    l = full_c.shape[1]
    win_start = l - win_all.shape[1]
    attn = nsa_attention(q.reshape(b, nq, N_HEADS, HEAD_DIM), gate_logits, full_c, full_s, win_all,
                         win_start, cmp_pe, cmp_w1, cmp_w2, rel_bias)

    mixed = jnp.concatenate([conv_out, attn * jax.nn.silu(z_attn)], axis=-1)
    h = h + jnp.einsum('bqe,ed->bqd', mixed, w_out)
    h = h + jax.nn.sigmoid(jnp.einsum('bqd,de->bqe', h, w_ple_gate)) * jnp.einsum('bqp,pd->bqd', ple, w_ple)
    new_win = win_all[:, -min(WINDOW, l):]
    new_conv = xc[:, -(CONV_WIDTH - 1):]
    return h, kv_c, kv_s, new_win, new_conv


def setup_inputs(seed: int = 0) -> dict:
    key = jax.random.key(seed)
    k = jax.random.split(key, 24)
    n_pages = PAST_LEN // PAGE_SIZE
    n_used = DEC_BATCH * n_pages
    n_pool = n_used + max(1, n_used // 4)
    w_buf = min(WINDOW, PAST_LEN)

    def nrm(kk, shape, scale=1.0):
        return jax.random.normal(kk, shape, jnp.float32) * scale

    page_table = jax.random.permutation(k[4], n_pool)[:n_used].reshape(DEC_BATCH, n_pages).astype(jnp.int32)
    return {
        'x_prompt': nrm(k[0], (BATCH, SEQ, D_MODEL)),
        'x_sample': nrm(k[1], (DEC_BATCH, DEC_SEQ, D_MODEL)),
        'cache_cmp_kv': nrm(k[2], (DEPTH, n_pool, PAGE_SIZE, 2, KV_HEADS, HEAD_DIM)),
        'cache_slc_kv': nrm(k[3], (DEPTH, n_pool, PAGE_SIZE, 2, KV_HEADS, HEAD_DIM)),
        'page_table': page_table,
        'state_win_kv': nrm(k[5], (DEPTH, DEC_BATCH, w_buf, 2, KV_HEADS, HEAD_DIM)),
        'state_conv': nrm(k[6], (DEPTH, DEC_BATCH, CONV_WIDTH - 1, D_CONV), 0.5),
        'p_prompt': nrm(k[7], (DEPTH, BATCH, SEQ, PLE_DIM)),
        'p_sample': nrm(k[8], (DEPTH, DEC_BATCH, DEC_SEQ, PLE_DIM)),
        'norm_g': 1.0 + nrm(k[9], (DEPTH, D_MODEL), 0.05),
        'w_in': nrm(k[10], (DEPTH, D_MODEL, D_IN), D_MODEL ** -0.5),
        'conv_w': nrm(k[11], (DEPTH, CONV_WIDTH, D_CONV), CONV_WIDTH ** -0.5),
        'conv_b': nrm(k[12], (DEPTH, D_CONV), 0.02),
        'conv_ln_g': 1.0 + nrm(k[13], (DEPTH, D_CONV), 0.05),
        'conv_ln_b': nrm(k[14], (DEPTH, D_CONV), 0.02),
        'cmp_pe': nrm(k[15], (DEPTH, 2, CMP_LEN, HEAD_DIM), 0.1),
        'cmp_w1': nrm(k[16], (DEPTH, 2, CMP_LEN, HEAD_DIM, CMP_HIDDEN), (CMP_LEN * HEAD_DIM) ** -0.5),
        'cmp_w2': nrm(k[17], (DEPTH, 2, CMP_HIDDEN, HEAD_DIM), CMP_HIDDEN ** -0.5),
        'w_out': nrm(k[18], (DEPTH, D_CONV + D_ATTN, D_MODEL), (D_CONV + D_ATTN) ** -0.5),
        'w_ple': nrm(k[19], (DEPTH, PLE_DIM, D_MODEL), PLE_DIM ** -0.5),
        'w_ple_gate': nrm(k[20], (DEPTH, D_MODEL, D_MODEL), D_MODEL ** -0.5),
        'rel_bias': nrm(k[21], (NUM_BUCKETS, N_HEADS), 0.5),
        'final_norm_g': 1.0 + nrm(k[22], (D_MODEL,), 0.05),
    }


def reference(x_prompt, x_sample, cache_cmp_kv, cache_slc_kv, page_table, state_win_kv, state_conv,
              p_prompt, p_sample, norm_g, w_in, conv_w, conv_b, conv_ln_g, conv_ln_b, cmp_pe, cmp_w1,
              cmp_w2, w_out, w_ple, w_ple_gate, rel_bias, final_norm_g):
    def paged_rows(pool):
        return pool[page_table].reshape(page_table.shape[0], -1, *pool.shape[2:])

    bp = x_prompt.shape[0]
    empty = jnp.zeros((bp, 0, 2, KV_HEADS, HEAD_DIM), x_prompt.dtype)
    conv_zero = jnp.zeros((bp, CONV_WIDTH - 1, D_CONV), x_prompt.dtype)
    hp, hs = x_prompt, x_sample
    cmp_p, cmp_s, slc_p, slc_s, win_p, win_s, conv_p, conv_s = [], [], [], [], [], [], [], []
    for i in range(DEPTH):
        layer = functools.partial(
            mixer_layer, norm_g=norm_g[i], w_in=w_in[i], conv_w=conv_w[i], conv_b=conv_b[i],
            conv_ln_g=conv_ln_g[i], conv_ln_b=conv_ln_b[i], cmp_pe=cmp_pe[i], cmp_w1=cmp_w1[i],
            cmp_w2=cmp_w2[i], w_out=w_out[i], w_ple=w_ple[i], w_ple_gate=w_ple_gate[i], rel_bias=rel_bias)
        hp, c_p, s_p, w_p, v_p = layer(hp, p_prompt[i], conv_zero, empty, empty, empty)
        hs, c_s, s_s, w_s, v_s = layer(hs, p_sample[i], state_conv[i], paged_rows(cache_cmp_kv[i]),
                                       paged_rows(cache_slc_kv[i]), state_win_kv[i])
        cmp_p.append(c_p); cmp_s.append(c_s); slc_p.append(s_p); slc_s.append(s_s)
        win_p.append(w_p); win_s.append(w_s); conv_p.append(v_p); conv_s.append(v_s)
    y_prompt = rms_norm(hp, final_norm_g)
    y_sample = rms_norm(hs, final_norm_g)
    return (y_prompt, y_sample, jnp.stack(cmp_p), jnp.stack(cmp_s), jnp.stack(slc_p), jnp.stack(slc_s),
            jnp.stack(win_p), jnp.stack(win_s), jnp.stack(conv_p), jnp.stack(conv_s))
```

```python
import functools
import math

import numpy as np
import jax
import jax.numpy as jnp
from jax import lax
from jax.experimental import pallas as pl
from jax.experimental.pallas import tpu as pltpu

F32 = jnp.float32
BF16 = jnp.bfloat16

D_MODEL = 1024
D_CONV = 512
CONV_WIDTH = 31
CONV_HIST = CONV_WIDTH - 1
HEAD_DIM = 64
N_HEADS = 8
KV_HEADS = 2
GROUP = N_HEADS // KV_HEADS
D_ATTN = N_HEADS * HEAD_DIM
KV_DIM = 2 * KV_HEADS * HEAD_DIM
N_BRANCH = 3
CMP_STRIDE = 16
CMP_LEN = 2 * CMP_STRIDE
CMP_HIDDEN = 128
SLC_BLOCK = 64
SLC_SHIFT = 6
TOP_N = 8
WINDOW = 512
NUM_BUCKETS = 32
MAX_DISTANCE = 128
PAGE_SIZE = 128
EPS = 1e-6
NEG_INF = -1e30
FORCED_SCORE = 1e4
MASKED_SCORE = -1e4
PAD_SCORE = -3e4
TINY = 1e-30
SCALE = HEAD_DIM ** -0.5

LANES = 128
HALF = LANES // 2
TQ = 256
TK = 256
TM = 256
VMEM_LIMIT = 56 * 1024 * 1024

SEC_CONV = (0, 3 * D_CONV)
SEC_Q = (SEC_CONV[1], SEC_CONV[1] + D_ATTN)
SEC_KVC = (SEC_Q[1], SEC_Q[1] + KV_DIM)
SEC_KVS = (SEC_KVC[1], SEC_KVC[1] + KV_DIM)
SEC_KVW = (SEC_KVS[1], SEC_KVS[1] + KV_DIM)
SEC_Z = (SEC_KVW[1], SEC_KVW[1] + D_ATTN)
SEC_GATE = (SEC_Z[1], SEC_Z[1] + LANES)
D_IN = SEC_Z[1] + N_BRANCH * N_HEADS
D_IN_PAD = SEC_GATE[1]
SECTIONS = (SEC_CONV, SEC_Q, SEC_KVC, SEC_KVS, SEC_KVW, SEC_Z, SEC_GATE)


def _bucket_lower_bounds():
    n = np.arange(0, 4 * MAX_DISTANCE, dtype=np.int64)
    max_exact = NUM_BUCKETS // 2
    nf = np.maximum(n, 1).astype(np.float32)
    large = max_exact + (np.log(nf / np.float32(max_exact)) / np.float32(math.log(MAX_DISTANCE / max_exact))
                         * np.float32(NUM_BUCKETS - max_exact)).astype(np.int32)
    large = np.minimum(large, NUM_BUCKETS - 1)
    bucket = np.where(n < max_exact, n, large)
    return [int(np.argmax(bucket >= b)) for b in range(NUM_BUCKETS)]


BUCKET_LOWER = _bucket_lower_bounds()


def _dot(a, b):
    return jnp.dot(a.astype(BF16), b.astype(BF16), preferred_element_type=F32)


def _dot_nt(a, b):
    return lax.dot_general(a.astype(BF16), b.astype(BF16), (((1,), (1,)), ((), ())),
                           preferred_element_type=F32)


def _dot_split(a, b):
    hi = a.astype(BF16)
    lo = (a - hi.astype(F32)).astype(BF16)
    return (jnp.dot(hi, b, preferred_element_type=F32) + jnp.dot(lo, b, preferred_element_type=F32))


def _sigmoid(x):
    return 1.0 / (1.0 + jnp.exp(-x))


def _silu(x):
    return x * _sigmoid(x)


def _masked_softmax(s, mask):
    s = jnp.where(mask, s, NEG_INF)
    m = jnp.max(s, axis=-1, keepdims=True)
    e = jnp.exp(s - m) * mask.astype(F32)
    l = jnp.sum(e, axis=-1, keepdims=True)
    return e / jnp.maximum(l, TINY)


def _bias_of(dist, rb_ref, h):
    out = jnp.full(dist.shape, rb_ref[0, h], F32)
    for b in range(1, NUM_BUCKETS):
        out = jnp.where(dist >= BUCKET_LOWER[b], rb_ref[b, h], out)
    return out


def _bias_kernel(rb_ref, nb_ref, cb_ref, fb_ref, sba_ref, swa_ref, sbb_ref, sc_ref, *, n_qt, nc_p, past, wbuf, nq_s):
    h = pl.program_id(0)
    qi = lax.broadcasted_iota(jnp.int32, (TQ, 2 * TK), 0)
    c = lax.broadcasted_iota(jnp.int32, (TQ, 2 * TK), 1)
    nb_ref[0] = _bias_of(qi + TK - c, rb_ref, h)
    qn = lax.broadcasted_iota(jnp.int32, (TQ, nc_p), 0)
    nn = lax.broadcasted_iota(jnp.int32, (TQ, nc_p), 1)
    for t in range(n_qt):
        cb_ref[t, 0] = _bias_of(t * TQ + qn - (nn * CMP_STRIDE + CMP_LEN - 1), rb_ref, h)
    fb_ref[0] = jnp.full(fb_ref.shape[1:], rb_ref[NUM_BUCKETS - 1, h], F32)
    qs = lax.broadcasted_iota(jnp.int32, (nq_s, past), 0)
    ks = lax.broadcasted_iota(jnp.int32, (nq_s, past), 1)
    sba_ref[0] = _bias_of(past + qs - ks, rb_ref, h)
    qs = lax.broadcasted_iota(jnp.int32, (nq_s, wbuf), 0)
    ks = lax.broadcasted_iota(jnp.int32, (nq_s, wbuf), 1)
    swa_ref[0] = _bias_of(wbuf + qs - ks, rb_ref, h)
    qs = lax.broadcasted_iota(jnp.int32, (nq_s, LANES), 0)
    ks = lax.broadcasted_iota(jnp.int32, (nq_s, LANES), 1)
    sbb_ref[0] = _bias_of(qs - ks, rb_ref, h)
    nc_s = sc_ref.shape[2]
    qs = lax.broadcasted_iota(jnp.int32, (nq_s, nc_s), 0)
    ks = lax.broadcasted_iota(jnp.int32, (nq_s, nc_s), 1)
    sc_ref[0] = _bias_of(past + qs - (ks * CMP_STRIDE + CMP_LEN - 1), rb_ref, h)


def _bias_tables(rel_bias, *, seq, past, wbuf, nq_s):
    n_qt = seq // TQ
    nc_p = seq // CMP_STRIDE
    nc_s = past // CMP_STRIDE
    kern = functools.partial(_bias_kernel, n_qt=n_qt, nc_p=nc_p, past=past, wbuf=wbuf, nq_s=nq_s)
    shapes = (
        jax.ShapeDtypeStruct((N_HEADS, TQ, 2 * TK), F32),
        jax.ShapeDtypeStruct((n_qt, N_HEADS, TQ, nc_p), F32),
        jax.ShapeDtypeStruct((N_HEADS, 8, TK), F32),
        jax.ShapeDtypeStruct((N_HEADS, nq_s, past), F32),
        jax.ShapeDtypeStruct((N_HEADS, nq_s, wbuf), F32),
        jax.ShapeDtypeStruct((N_HEADS, nq_s, LANES), F32),
        jax.ShapeDtypeStruct((N_HEADS, nq_s, nc_s), F32),
    )
    out_specs = (
        pl.BlockSpec((1, TQ, 2 * TK), lambda h: (h, 0, 0)),
        pl.BlockSpec((n_qt, 1, TQ, nc_p), lambda h: (0, h, 0, 0)),
        pl.BlockSpec((1, 8, TK), lambda h: (h, 0, 0)),
        pl.BlockSpec((1, nq_s, past), lambda h: (h, 0, 0)),
        pl.BlockSpec((1, nq_s, wbuf), lambda h: (h, 0, 0)),
        pl.BlockSpec((1, nq_s, LANES), lambda h: (h, 0, 0)),
        pl.BlockSpec((1, nq_s, nc_s), lambda h: (h, 0, 0)),
    )
    return pl.pallas_call(
        kern, grid=(N_HEADS,),
        in_specs=[pl.BlockSpec(memory_space=pltpu.SMEM)],
        out_specs=out_specs, out_shape=shapes, name="bias_tables",
        compiler_params=pltpu.CompilerParams(dimension_semantics=("arbitrary",)),
    )(rel_bias)


def _in_proj_kernel(x_ref, g_ref, w_ref, *out_refs):
    x = x_ref[...]
    u = x * lax.rsqrt(jnp.mean(x * x, axis=-1, keepdims=True) + EPS) * g_ref[...]
    ub = u.astype(BF16)
    for ref, (a, b) in zip(out_refs, SECTIONS):
        ref[...] = jnp.dot(ub, w_ref[:, a:b], preferred_element_type=F32)


def _in_proj(x2d, g, w_pad):
    n = x2d.shape[0]
    tm = min(TM, n)
    widths = [b - a for a, b in SECTIONS]
    return pl.pallas_call(
        _in_proj_kernel, grid=(n // tm,),
        in_specs=[pl.BlockSpec((tm, D_MODEL), lambda i: (i, 0)),
                  pl.BlockSpec((1, D_MODEL), lambda i: (0, 0)),
                  pl.BlockSpec((D_MODEL, D_IN_PAD), lambda i: (0, 0))],
        out_specs=[pl.BlockSpec((tm, w), lambda i: (i, 0)) for w in widths],
        out_shape=[jax.ShapeDtypeStruct((n, w), F32) for w in widths],
        name="in_proj",
        compiler_params=pltpu.CompilerParams(dimension_semantics=("arbitrary",), vmem_limit_bytes=VMEM_LIMIT),
    )(x2d, g, w_pad)


HIST_PAD = 32


def _conv_kernel(c3_ref, hist_ref, w_ref, b_ref, lg_ref, lb_ref, out_ref, new_ref, xbuf, *, tt, n_t):
    t = pl.program_id(1)
    off = HIST_PAD - CONV_HIST

    @pl.when(t == 0)
    def _():
        xbuf[0:HIST_PAD, :] = jnp.zeros((HIST_PAD, D_CONV), F32)
        xbuf[off:HIST_PAD, :] = hist_ref[0]

    glu = c3_ref[0, :, 0:D_CONV] * _sigmoid(c3_ref[0, :, D_CONV:2 * D_CONV])
    xbuf[HIST_PAD:HIST_PAD + tt, :] = glu
    acc = jnp.zeros((tt, D_CONV), F32) + b_ref[...]
    for k in range(CONV_WIDTH):
        acc = acc + xbuf[off + k:off + k + tt, :] * w_ref[k:k + 1, :]
    mu = jnp.mean(acc, axis=-1, keepdims=True)
    xc = acc - mu
    var = jnp.mean(xc * xc, axis=-1, keepdims=True)
    y = xc * lax.rsqrt(var + EPS) * lg_ref[...] + lb_ref[...]
    out_ref[0] = _silu(y) * _silu(c3_ref[0, :, 2 * D_CONV:3 * D_CONV])
    tail = xbuf[off + tt:HIST_PAD + tt, :]

    @pl.when(t == n_t - 1)
    def _():
        new_ref[0] = tail

    if n_t > 1:
        @pl.when(t < n_t - 1)
        def _():
            xbuf[off:HIST_PAD, :] = tail


def _conv_module(c3, hist, conv_w, conv_b, ln_g, ln_b):
    bsz, t_len, _ = c3.shape
    tt = min(256, t_len)
    n_t = t_len // tt
    kern = functools.partial(_conv_kernel, tt=tt, n_t=n_t)
    vec = pl.BlockSpec((1, D_CONV), lambda b, t: (0, 0))
    return pl.pallas_call(
        kern, grid=(bsz, n_t),
        in_specs=[pl.BlockSpec((1, tt, 3 * D_CONV), lambda b, t: (b, t, 0)),
                  pl.BlockSpec((1, CONV_HIST, D_CONV), lambda b, t: (b, 0, 0)),
                  pl.BlockSpec((CONV_WIDTH, D_CONV), lambda b, t: (0, 0)),
                  vec, vec, vec],
        out_specs=[pl.BlockSpec((1, tt, D_CONV), lambda b, t: (b, t, 0)),
                   pl.BlockSpec((1, CONV_HIST, D_CONV), lambda b, t: (b, 0, 0))],
        out_shape=[jax.ShapeDtypeStruct((bsz, t_len, D_CONV), F32),
                   jax.ShapeDtypeStruct((bsz, CONV_HIST, D_CONV), F32)],
        scratch_shapes=[pltpu.VMEM((HIST_PAD + tt, D_CONV), F32)],
        name="conv_module",
        compiler_params=pltpu.CompilerParams(dimension_semantics=("arbitrary", "arbitrary")),
    )(c3, hist, conv_w, conv_b, ln_g, ln_b)


HALVES_PER_PAGE = PAGE_SIZE // CMP_STRIDE
N_TG = 2 * KV_HEADS
N_LT = KV_DIM // LANES
CMP_FEAT = CMP_STRIDE * HEAD_DIM


def _compress_kernel(pt_ref, *refs, n_pages):
    del pt_ref
    pages = refs[:n_pages]
    pe_ref, w1_ref, w2_ref, out_ref, y_scr, h_scr = refs[n_pages:]
    n_half = n_pages * HALVES_PER_PAGE
    low = lax.broadcasted_iota(jnp.int32, (HALVES_PER_PAGE, LANES), 1) < HALF
    for p in range(n_pages):
        rows = slice(p * HALVES_PER_PAGE, (p + 1) * HALVES_PER_PAGE)
        for j in range(CMP_STRIDE // 2):
            for lt in range(N_LT):
                a = pages[p][0, pl.ds(N_LT * (2 * j) + lt, HALVES_PER_PAGE, stride=N_LT * CMP_STRIDE), :]
                b = pages[p][0, pl.ds(N_LT * (2 * j + 1) + lt, HALVES_PER_PAGE, stride=N_LT * CMP_STRIDE), :]
                y_scr[2 * lt, rows, j * LANES:(j + 1) * LANES] = jnp.where(low, a, pltpu.roll(b, HALF, 1))
                y_scr[2 * lt + 1, rows, j * LANES:(j + 1) * LANES] = jnp.where(low, pltpu.roll(a, HALF, 1), b)
    h_scr[n_half:n_half + 8, :] = jnp.zeros((8, 2 * CMP_HIDDEN), F32)
    acts = []
    for tg in range(N_TG):
        t = tg // KV_HEADS
        w1 = w1_ref[t]
        c = jnp.dot(pe_ref[t].astype(BF16), w1, preferred_element_type=F32)
        cvec = c[0:1, 0:CMP_HIDDEN] + c[1:2, CMP_HIDDEN:2 * CMP_HIDDEN]
        h_scr[0:n_half, :] = jnp.dot(y_scr[tg].astype(BF16), w1, preferred_element_type=F32)
        hid = h_scr[0:n_half, 0:CMP_HIDDEN] + h_scr[1:n_half + 1, CMP_HIDDEN:2 * CMP_HIDDEN] + cvec
        acts.append(_silu(hid))
    out = jnp.dot(jnp.concatenate(acts, axis=1).astype(BF16), w2_ref[...], preferred_element_type=F32)
    row = lax.broadcasted_iota(jnp.int32, out.shape, 0)
    out_ref[0] = jnp.where(row < n_half - 1, out, 0.0)


def _compress(pages3d, table, pe_pad, w1cat, w2big):
    bsz, n_pages = table.shape
    n_half = n_pages * HALVES_PER_PAGE
    kern = functools.partial(_compress_kernel, n_pages=n_pages)
    pages3d = pages3d.reshape(-1, N_LT * PAGE_SIZE, LANES)
    page_specs = [pl.BlockSpec((1, N_LT * PAGE_SIZE, LANES), lambda b, pt, p=p: (pt[b, p], 0, 0))
                  for p in range(n_pages)]
    grid_spec = pltpu.PrefetchScalarGridSpec(
        num_scalar_prefetch=1, grid=(bsz,),
        in_specs=page_specs + [
            pl.BlockSpec((2, 8, CMP_FEAT), lambda b, pt: (0, 0, 0)),
            pl.BlockSpec((2, CMP_FEAT, 2 * CMP_HIDDEN), lambda b, pt: (0, 0, 0)),
            pl.BlockSpec((N_TG * CMP_HIDDEN, KV_DIM), lambda b, pt: (0, 0))],
        out_specs=pl.BlockSpec((1, n_half, KV_DIM), lambda b, pt: (b, 0, 0)),
        scratch_shapes=[pltpu.VMEM((N_TG, n_half, CMP_FEAT), F32),
                        pltpu.VMEM((n_half + 8, 2 * CMP_HIDDEN), F32)])
    return pl.pallas_call(
        kern, grid_spec=grid_spec,
        out_shape=jax.ShapeDtypeStruct((bsz, n_half, KV_DIM), F32),
        name="compress",
        compiler_params=pltpu.CompilerParams(dimension_semantics=("arbitrary",), vmem_limit_bytes=VMEM_LIMIT),
    )(table, *([pages3d] * n_pages), pe_pad, w1cat, w2big)


def _head_rows(q, g, low):
    parts = []
    for r in range(GROUP):
        h = GROUP * g + r
        tile = q[:, LANES * (h // 2):LANES * (h // 2 + 1)]
        if (h % 2) != g:
            tile = pltpu.roll(tile, HALF, 1)
        parts.append(jnp.where(low, tile, 0.0) if g == 0 else jnp.where(low, 0.0, tile))
    return parts


def _assemble_heads(o_heads, low):
    tiles = []
    for j in range(N_HEADS // 2):
        a, b = o_heads[2 * j], o_heads[2 * j + 1]
        if (2 * j) // GROUP == 0:
            tiles.append(jnp.where(low, a, pltpu.roll(b, HALF, 1)))
        else:
            tiles.append(jnp.where(low, pltpu.roll(a, HALF, 1), b))
    return jnp.concatenate(tiles, axis=1)


def _select_blocks_t(score_t, allowed_t, n_blk, top_n):
    idx = lax.broadcasted_iota(jnp.int32, score_t.shape, 0)
    cnt = jnp.zeros(score_t.shape, jnp.int32)
    for i in range(n_blk):
        row = score_t[i:i + 1, :]
        ahead = (row > score_t) | ((row == score_t) & (idx > i))
        cnt = cnt + ahead.astype(jnp.int32)
    return (cnt < top_n) & allowed_t


def _flash_step(qg, k_tile, v_tile, bias, mask, state):
    m, l, acc = state
    s = _dot_nt(qg, k_tile).reshape(GROUP, TQ, TK) + bias
    s = jnp.where(mask[None], s, NEG_INF)
    m_new = jnp.maximum(m, jnp.max(s, axis=-1, keepdims=True))
    alpha = jnp.exp(m - m_new)
    p = jnp.exp(s - m_new) * mask[None].astype(F32)
    l = alpha * l + jnp.sum(p, axis=-1, keepdims=True)
    pv = _dot(p.reshape(GROUP * TQ, TK), v_tile).reshape(GROUP, TQ, LANES)
    return m_new, l, alpha * acc + pv


def _flash_init():
    return (jnp.full((GROUP, TQ, 1), NEG_INF, F32), jnp.zeros((GROUP, TQ, 1), F32),
            jnp.zeros((GROUP, TQ, LANES), F32))


def _flash_out(state):
    _, l, acc = state
    return acc / jnp.maximum(l, TINY)


def _attn_prompt_kernel(q_ref, gate_ref, z_ref, kcvc_ref, kvs_ref, kvw_ref, nb_ref, cb_ref, fb_ref, cover_ref,
                        out_ref, *, n_slc):
    qt = pl.program_id(1)
    nc = kcvc_ref.shape[1]
    low = lax.broadcasted_iota(jnp.int32, (TQ, LANES), 1) < HALF
    qi = lax.broadcasted_iota(jnp.int32, (TQ, TK), 0)
    ki = lax.broadcasted_iota(jnp.int32, (TQ, TK), 1)
    causal = qi >= ki
    q = q_ref[...]
    gsig = _sigmoid(gate_ref[...])
    kc = kcvc_ref[0, :, 0:LANES].astype(BF16)
    vc = kcvc_ref[0, :, LANES:2 * LANES].astype(BF16)
    qpos_c = qt * TQ + lax.broadcasted_iota(jnp.int32, (TQ, nc), 0)
    c_end = lax.broadcasted_iota(jnp.int32, (TQ, nc), 1) * CMP_STRIDE + (CMP_LEN - 1)
    cmask = qpos_c >= c_end
    blk_t = lax.broadcasted_iota(jnp.int32, (n_slc, TQ), 0)
    cur_t = jnp.right_shift(qt * TQ + lax.broadcasted_iota(jnp.int32, (n_slc, TQ), 1), SLC_SHIFT)
    allowed_t = blk_t <= cur_t
    forced_t = (blk_t == 0) | (blk_t == cur_t) | (blk_t == cur_t - 1)
    e_row = lax.broadcasted_iota(jnp.int32, (LANES, TK), 0)
    e_col = jnp.right_shift(lax.broadcasted_iota(jnp.int32, (LANES, TK), 1), SLC_SHIFT)
    kt_prev = jnp.maximum(qt - 1, 0)
    kt_far = jnp.maximum(qt - 2, 0)

    def kv_tile(ref, kt):
        start = pl.multiple_of(kt * TK, TK)
        return (ref[0, pl.ds(start, TK), 0:LANES].astype(BF16),
                ref[0, pl.ds(start, TK), LANES:2 * LANES].astype(BF16))

    o_heads = []
    for g in range(KV_HEADS):
        hs = slice(GROUP * g, GROUP * (g + 1))
        qg = (jnp.concatenate(_head_rows(q, g, low), axis=0) * SCALE).astype(BF16)
        s_c = _dot_nt(qg, kc).reshape(GROUP, TQ, nc) + cb_ref[0, hs]
        p_c = _masked_softmax(s_c, cmask[None])
        o_cmp = _dot(p_c.reshape(GROUP * TQ, nc), vc).reshape(GROUP, TQ, LANES)
        imp = _dot_split(jnp.sum(p_c, axis=0), cover_ref[...])
        imp_t = imp.T[0:n_slc, :]
        score_t = jnp.where(allowed_t, jnp.where(forced_t, FORCED_SCORE, imp_t), MASKED_SCORE)
        sel_t = _select_blocks_t(score_t, allowed_t, n_slc, min(TOP_N, n_slc)).astype(F32)
        sel = jnp.concatenate([sel_t, jnp.zeros((LANES - n_slc, TQ), F32)], axis=0).T.astype(BF16)

        def sel_mask(kt):
            expand = (e_row == kt * (TK // SLC_BLOCK) + e_col).astype(BF16)
            return jnp.dot(sel, expand, preferred_element_type=F32) > 0.5

        far_bias = fb_ref[hs, 0:1, :]

        def far_body(kt, state):
            k_t, v_t = kv_tile(kvs_ref, kt)
            return _flash_step(qg, k_t, v_t, far_bias, sel_mask(kt), state)

        state = lax.fori_loop(0, jnp.maximum(qt - 1, 0), far_body, _flash_init())
        k_t, v_t = kv_tile(kvs_ref, kt_prev)
        state = _flash_step(qg, k_t, v_t, nb_ref[hs, :, 0:TK], sel_mask(kt_prev) & (qt >= 1), state)
        k_t, v_t = kv_tile(kvs_ref, qt)
        state = _flash_step(qg, k_t, v_t, nb_ref[hs, :, TK:2 * TK], sel_mask(qt) & causal, state)
        o_slc = _flash_out(state)
        state = _flash_init()
        k_t, v_t = kv_tile(kvw_ref, kt_far)
        state = _flash_step(qg, k_t, v_t, far_bias, (ki > qi) & (qt >= 2), state)
        k_t, v_t = kv_tile(kvw_ref, kt_prev)
        state = _flash_step(qg, k_t, v_t, nb_ref[hs, :, 0:TK], (ki >= 0) & (qt >= 1), state)
        k_t, v_t = kv_tile(kvw_ref, qt)
        state = _flash_step(qg, k_t, v_t, nb_ref[hs, :, TK:2 * TK], causal, state)
        o_win = _flash_out(state)
        for r in range(GROUP):
            h = GROUP * g + r
            o_heads.append(gsig[:, h:h + 1] * o_cmp[r] + gsig[:, N_HEADS + h:N_HEADS + h + 1] * o_slc[r]
                           + gsig[:, 2 * N_HEADS + h:2 * N_HEADS + h + 1] * o_win[r])
    out_ref[...] = _assemble_heads(o_heads, low) * _silu(z_ref[...])


def _attn_prompt(q2d, gate2d, z2d, kcvc, kvs, kvw, nb, cb, fb, cover, *, bsz, seq):
    assert WINDOW == 2 * TK and seq % TQ == 0
    n_qt = seq // TQ
    nc = seq // CMP_STRIDE
    n_slc = seq // SLC_BLOCK
    kern = functools.partial(_attn_prompt_kernel, n_slc=n_slc)
    tok = lambda w: pl.BlockSpec((TQ, w), lambda b, t: (b * n_qt + t, 0))
    per_b = lambda r, w: pl.BlockSpec((1, r, w), lambda b, t: (b, 0, 0))
    return pl.pallas_call(
        kern, grid=(bsz, n_qt),
        in_specs=[tok(D_ATTN), tok(LANES), tok(D_ATTN),
                  per_b(nc, KV_DIM), per_b(seq, KV_DIM), per_b(seq, KV_DIM),
                  pl.BlockSpec((N_HEADS, TQ, 2 * TK), lambda b, t: (0, 0, 0)),
                  pl.BlockSpec((1, N_HEADS, TQ, nc), lambda b, t: (t, 0, 0, 0)),
                  pl.BlockSpec((N_HEADS, 8, TK), lambda b, t: (0, 0, 0)),
                  pl.BlockSpec((nc, LANES), lambda b, t: (0, 0))],
        out_specs=tok(D_ATTN),
        out_shape=jax.ShapeDtypeStruct((bsz * seq, D_ATTN), F32),
        name="attn_prompt",
        compiler_params=pltpu.CompilerParams(dimension_semantics=("arbitrary", "arbitrary"),
                                             vmem_limit_bytes=VMEM_LIMIT),
    )(q2d, gate2d, z2d, kcvc, kvs, kvw, nb, cb, fb, cover)


def _attn_sample_kernel(pt_ref, *refs, n_pages, nq):
    del pt_ref
    pages = refs[:n_pages]
    (win_ref, kvsn_ref, kvwn_ref, q_ref, gate_ref, z_ref, kcvc_ref, sba_ref, swa_ref, sbb_ref, sc_ref,
     cover_ref, expand_ref, out_ref, nwin_ref) = refs[n_pages:]
    past = n_pages * PAGE_SIZE
    wbuf = win_ref.shape[1]
    nc = kcvc_ref.shape[1]
    rows = KV_HEADS * GROUP * nq
    cur = past // SLC_BLOCK
    n_slc = cur + 1
    low = lax.broadcasted_iota(jnp.int32, (nq, LANES), 1) < HALF
    q = q_ref[0]
    q_left = jnp.concatenate(_head_rows(q, 0, low) + _head_rows(q, 1, low), axis=0) * SCALE
    q_blk = jnp.concatenate([q_left, jnp.zeros((rows, LANES), F32)], axis=1).astype(BF16)
    q_left = q_left.astype(BF16)
    qi = lax.broadcasted_iota(jnp.int32, (rows, LANES), 0) & (nq - 1)
    ki = lax.broadcasted_iota(jnp.int32, (rows, LANES), 1)
    new_mask = (ki <= qi) & (ki < nq)
    pad_rows = jnp.zeros((LANES - nq, KV_DIM), F32)
    kc = kcvc_ref[0, :, 0:LANES].astype(BF16)
    vc = kcvc_ref[0, :, LANES:2 * LANES].astype(BF16)
    n_idx = lax.broadcasted_iota(jnp.int32, (rows, nc), 1)
    p_c = _masked_softmax(_dot_nt(q_left, kc) + sc_ref[...], n_idx < nc - 1)
    o_cmp = _dot(p_c, vc)
    blk = lax.broadcasted_iota(jnp.int32, (nq, LANES), 1)
    is_blk = blk < n_slc
    forced = (blk == 0) | (blk == cur) | (blk == cur - 1)
    sel_rows = []
    for g in range(KV_HEADS):
        p_sum = p_c[g * GROUP * nq:g * GROUP * nq + nq]
        for r in range(1, GROUP):
            p_sum = p_sum + p_c[(g * GROUP + r) * nq:(g * GROUP + r + 1) * nq]
        imp = _dot_split(p_sum, cover_ref[...])
        score = jnp.where(is_blk, jnp.where(forced, FORCED_SCORE, imp), PAD_SCORE)
        cnt = jnp.zeros((nq, LANES), jnp.int32)
        for i in range(n_slc):
            col = score[:, i:i + 1]
            cnt = cnt + ((col > score) | ((col == score) & (blk > i))).astype(jnp.int32)
        sel_g = ((cnt < min(TOP_N, n_slc)) & is_blk).astype(F32)
        sel_rows += [sel_g] * GROUP
    sel = jnp.concatenate(sel_rows, axis=0)
    xs = [pages[p][0].astype(BF16) for p in range(n_pages)]
    x_new = jnp.concatenate([kvsn_ref[0], pad_rows], axis=0).astype(BF16)
    s_a = jnp.concatenate([_dot_nt(q_blk, x) for x in xs], axis=1) + sba_ref[...]
    mask_a = jnp.dot(sel.astype(BF16), expand_ref[...], preferred_element_type=F32) > 0.5
    s_b = _dot_nt(q_blk, x_new) + sbb_ref[...]
    mask_b = new_mask & (sel[:, cur:cur + 1] > 0.5)
    s_a = jnp.where(mask_a, s_a, NEG_INF)
    s_b = jnp.where(mask_b, s_b, NEG_INF)
    m = jnp.maximum(jnp.max(s_a, axis=-1, keepdims=True), jnp.max(s_b, axis=-1, keepdims=True))
    p_a = jnp.exp(s_a - m) * mask_a.astype(F32)
    p_b = jnp.exp(s_b - m) * mask_b.astype(F32)
    l = jnp.sum(p_a, axis=-1, keepdims=True) + jnp.sum(p_b, axis=-1, keepdims=True)
    acc = _dot(p_b, x_new)
    for p in range(n_pages):
        acc = acc + jnp.dot(p_a[:, p * PAGE_SIZE:(p + 1) * PAGE_SIZE].astype(BF16), xs[p],
                            preferred_element_type=F32)
    o_slc = acc[:, LANES:2 * LANES] / jnp.maximum(l, TINY)
    xw = win_ref[0]
    xw_new = kvwn_ref[0]
    nwin_ref[0, 0:wbuf - nq, :] = xw[nq:wbuf, :]
    nwin_ref[0, wbuf - nq:wbuf, :] = xw_new
    xwb = xw.astype(BF16)
    xwn = jnp.concatenate([xw_new, pad_rows], axis=0).astype(BF16)
    jw = lax.broadcasted_iota(jnp.int32, (rows, wbuf), 1)
    qw = lax.broadcasted_iota(jnp.int32, (rows, wbuf), 0) & (nq - 1)
    mask_wa = jw > qw
    s_wa = jnp.where(mask_wa, _dot_nt(q_blk, xwb) + swa_ref[...], NEG_INF)
    s_wb = jnp.where(new_mask, _dot_nt(q_blk, xwn) + sbb_ref[...], NEG_INF)
    m = jnp.maximum(jnp.max(s_wa, axis=-1, keepdims=True), jnp.max(s_wb, axis=-1, keepdims=True))
    p_wa = jnp.exp(s_wa - m) * mask_wa.astype(F32)
    p_wb = jnp.exp(s_wb - m) * new_mask.astype(F32)
    l = jnp.sum(p_wa, axis=-1, keepdims=True) + jnp.sum(p_wb, axis=-1, keepdims=True)
    acc = _dot(p_wa, xwb) + _dot(p_wb, xwn)
    o_win = acc[:, LANES:2 * LANES] / jnp.maximum(l, TINY)
    gsig = _sigmoid(gate_ref[0])
    o_heads = []
    for h in range(N_HEADS):
        rs = slice(h * nq, (h + 1) * nq)
        o_heads.append(gsig[:, h:h + 1] * o_cmp[rs] + gsig[:, N_HEADS + h:N_HEADS + h + 1] * o_slc[rs]
                       + gsig[:, 2 * N_HEADS + h:2 * N_HEADS + h + 1] * o_win[rs])
    out_ref[0] = _assemble_heads(o_heads, low) * _silu(z_ref[0])


def _attn_sample(slc_pages3d, table, win_state, kvs_new, kvw_new, q, gate, z, kcvc, sba, swa, sbb, sc,
                 cover, expand):
    bsz, n_pages = table.shape
    nq = q.shape[1]
    wbuf = win_state.shape[1]
    nc = kcvc.shape[1]
    past = n_pages * PAGE_SIZE
    rows = N_HEADS * nq
    assert nq <= SLC_BLOCK and nq & (nq - 1) == 0 and past % SLC_BLOCK == 0 and wbuf == WINDOW
    kern = functools.partial(_attn_sample_kernel, n_pages=n_pages, nq=nq)
    page_specs = [pl.BlockSpec((1, PAGE_SIZE, KV_DIM), lambda b, pt, p=p: (pt[b, p], 0, 0)) for p in range(n_pages)]
    per_b = lambda r, w: pl.BlockSpec((1, r, w), lambda b, pt: (b, 0, 0))
    const = lambda r, w: pl.BlockSpec((r, w), lambda b, pt: (0, 0))
    grid_spec = pltpu.PrefetchScalarGridSpec(
        num_scalar_prefetch=1, grid=(bsz,),
        in_specs=page_specs + [
            per_b(wbuf, KV_DIM), per_b(nq, KV_DIM), per_b(nq, KV_DIM), per_b(nq, D_ATTN), per_b(nq, LANES),
            per_b(nq, D_ATTN), per_b(nc, KV_DIM),
            const(rows, past), const(rows, wbuf), const(rows, LANES), const(rows, nc),
            const(nc, LANES), const(LANES, past)],
        out_specs=[per_b(nq, D_ATTN), per_b(wbuf, KV_DIM)])
    return pl.pallas_call(
        kern, grid_spec=grid_spec,
        out_shape=[jax.ShapeDtypeStruct((bsz, nq, D_ATTN), F32),
                   jax.ShapeDtypeStruct((bsz, wbuf, KV_DIM), F32)],
        name="attn_sample",
        compiler_params=pltpu.CompilerParams(dimension_semantics=("arbitrary",), vmem_limit_bytes=VMEM_LIMIT),
    )(table, *([slc_pages3d] * n_pages), win_state, kvs_new, kvw_new, q, gate, z, kcvc, sba, swa, sbb, sc,
      cover, expand)


def _out_proj_kernel(h_ref, conv_ref, attn_ref, ple_ref, wo_ref, wg_ref, wp_ref, fg_ref, out_ref, *, final):
    h = h_ref[...]
    h = h + jnp.dot(conv_ref[...].astype(BF16), wo_ref[0:D_CONV, :], preferred_element_type=F32)
    h = h + jnp.dot(attn_ref[...].astype(BF16), wo_ref[D_CONV:D_CONV + D_ATTN, :], preferred_element_type=F32)
    gate = _sigmoid(jnp.dot(h.astype(BF16), wg_ref[...], preferred_element_type=F32))
    h = h + gate * jnp.dot(ple_ref[...].astype(BF16), wp_ref[...], preferred_element_type=F32)
    if final:
        h = h * lax.rsqrt(jnp.mean(h * h, axis=-1, keepdims=True) + EPS) * fg_ref[...]
    out_ref[...] = h


def _out_proj(h2d, conv2d, attn2d, ple2d, wo, wg, wp, fg, *, final):
    n = h2d.shape[0]
    tm = min(TM, n)
    ple_dim = ple2d.shape[1]
    kern = functools.partial(_out_proj_kernel, final=final)
    tok = lambda w: pl.BlockSpec((tm, w), lambda i: (i, 0))
    const = lambda r, w: pl.BlockSpec((r, w), lambda i: (0, 0))
    return pl.pallas_call(
        kern, grid=(n // tm,),
        in_specs=[tok(D_MODEL), tok(D_CONV), tok(D_ATTN), tok(ple_dim),
                  const(D_CONV + D_ATTN, D_MODEL), const(D_MODEL, D_MODEL), const(ple_dim, D_MODEL),
                  const(1, D_MODEL)],
        out_specs=tok(D_MODEL),
        out_shape=jax.ShapeDtypeStruct((n, D_MODEL), F32),
        name="out_proj",
        compiler_params=pltpu.CompilerParams(dimension_semantics=("arbitrary",), vmem_limit_bytes=VMEM_LIMIT),
    )(h2d, conv2d, attn2d, ple2d, wo, wg, wp, fg)


def _cover_matrix(n_cmp_rows, n_cmp, n_slc):
    c_start = np.arange(n_cmp_rows) * CMP_STRIDE
    c_end = c_start + CMP_LEN - 1
    s_start = np.arange(LANES) * SLC_BLOCK
    cover = (c_start[:, None] < s_start[None, :] + SLC_BLOCK) & (c_end[:, None] >= s_start[None, :])
    cover &= (np.arange(n_cmp_rows)[:, None] < n_cmp) & (np.arange(LANES)[None, :] < n_slc)
    return jnp.asarray(cover, dtype=BF16)


def _expand_matrix(past):
    e = np.arange(LANES)[:, None] == (np.arange(past)[None, :] // SLC_BLOCK)
    return jnp.asarray(e, dtype=BF16)


def kernel(x_prompt, x_sample, cache_cmp_kv, cache_slc_kv, page_table, state_win_kv, state_conv, p_prompt, p_sample, norm_g, w_in, conv_w, conv_b, conv_ln_g, conv_ln_b, cmp_pe, cmp_w1, cmp_w2, w_out, w_ple, w_ple_gate, rel_bias, final_norm_g):
    bp, seq, _ = x_prompt.shape
    bs, nq, _ = x_sample.shape
    depth = w_in.shape[0]
    n_pages = page_table.shape[1]
    past = n_pages * PAGE_SIZE
    wbuf = state_win_kv.shape[2]
    n_pool = cache_cmp_kv.shape[1]
    win_p = min(WINDOW, seq)

    nb, cb, fb, sba, swa, sbb, sc = _bias_tables(rel_bias, seq=seq, past=past, wbuf=wbuf, nq_s=nq)
    rows = N_HEADS * nq
    sba, swa, sbb, sc = (a.reshape(rows, a.shape[-1]) for a in (sba, swa, sbb, sc))
    nc_p, nc_s = seq // CMP_STRIDE, past // CMP_STRIDE
    cover_p = _cover_matrix(nc_p, nc_p - 1, seq // SLC_BLOCK)
    cover_s = _cover_matrix(nc_s, nc_s - 1, past // SLC_BLOCK + 1)
    expand_s = _expand_matrix(past)
    table_p = jnp.arange(bp * (seq // PAGE_SIZE), dtype=jnp.int32).reshape(bp, seq // PAGE_SIZE)
    conv_zero = jnp.zeros((bp, CONV_HIST, D_CONV), F32)
    fg = final_norm_g.reshape(1, D_MODEL)

    hp = x_prompt.reshape(bp * seq, D_MODEL)
    hs = x_sample.reshape(bs * nq, D_MODEL)
    outs = [[] for _ in range(8)]
    for i in range(depth):
        w_pad = jnp.pad(w_in[i], ((0, 0), (0, D_IN_PAD - D_IN))).astype(BF16)
        g = norm_g[i].reshape(1, D_MODEL)
        w1 = cmp_w1[i].reshape(2, 2, CMP_FEAT, CMP_HIDDEN)
        w1cat = jnp.concatenate([w1[:, 0], w1[:, 1]], axis=-1).astype(BF16)
        pe_pad = jnp.pad(cmp_pe[i].reshape(2, 2, CMP_FEAT), ((0, 0), (0, 6), (0, 0)))
        w2big = jnp.zeros((N_TG * CMP_HIDDEN, KV_DIM), F32)
        for tg in range(N_TG):
            w2big = w2big.at[tg * CMP_HIDDEN:(tg + 1) * CMP_HIDDEN, tg * HEAD_DIM:(tg + 1) * HEAD_DIM].set(
                cmp_w2[i, tg // KV_HEADS])
        w2big = w2big.astype(BF16)
        wo, wg, wp = w_out[i].astype(BF16), w_ple_gate[i].astype(BF16), w_ple[i].astype(BF16)
        cw, cbias = conv_w[i], conv_b[i].reshape(1, D_CONV)
        lg, lb = conv_ln_g[i].reshape(1, D_CONV), conv_ln_b[i].reshape(1, D_CONV)
        final = i == depth - 1

        c3, q, kvc, kvs, kvw, z, gate = _in_proj(hp, g, w_pad)
        conv_out, new_conv = _conv_module(c3.reshape(bp, seq, 3 * D_CONV), conv_zero, cw, cbias, lg, lb)
        kcvc = _compress(kvc.reshape(bp * seq // PAGE_SIZE, PAGE_SIZE, KV_DIM), table_p, pe_pad, w1cat, w2big)
        attn = _attn_prompt(q, gate, z, kcvc, kvs.reshape(bp, seq, KV_DIM), kvw.reshape(bp, seq, KV_DIM),
                            nb, cb, fb, cover_p, bsz=bp, seq=seq)
        hp = _out_proj(hp, conv_out.reshape(bp * seq, D_CONV), attn, p_prompt[i].reshape(bp * seq, -1),
                       wo, wg, wp, fg, final=final)
        outs[0].append(kvc.reshape(bp, seq, 2, KV_HEADS, HEAD_DIM))
        outs[2].append(kvs.reshape(bp, seq, 2, KV_HEADS, HEAD_DIM))
        outs[4].append(kvw.reshape(bp, seq, 2, KV_HEADS, HEAD_DIM)[:, seq - win_p:])
        outs[6].append(new_conv)

        c3, q, kvc, kvs, kvw, z, gate = _in_proj(hs, g, w_pad)
        conv_out, new_conv = _conv_module(c3.reshape(bs, nq, 3 * D_CONV), state_conv[i], cw, cbias, lg, lb)
        kcvc = _compress(cache_cmp_kv[i].reshape(n_pool, PAGE_SIZE, KV_DIM), page_table, pe_pad, w1cat, w2big)
        attn, new_win = _attn_sample(
            cache_slc_kv[i].reshape(n_pool, PAGE_SIZE, KV_DIM), page_table,
            state_win_kv[i].reshape(bs, wbuf, KV_DIM), kvs.reshape(bs, nq, KV_DIM), kvw.reshape(bs, nq, KV_DIM),
            q.reshape(bs, nq, D_ATTN), gate.reshape(bs, nq, LANES), z.reshape(bs, nq, D_ATTN), kcvc,
            sba, swa, sbb, sc, cover_s, expand_s)
        hs = _out_proj(hs, conv_out.reshape(bs * nq, D_CONV), attn.reshape(bs * nq, D_ATTN),
                       p_sample[i].reshape(bs * nq, -1), wo, wg, wp, fg, final=final)
        outs[1].append(kvc.reshape(bs, nq, 2, KV_HEADS, HEAD_DIM))
        outs[3].append(kvs.reshape(bs, nq, 2, KV_HEADS, HEAD_DIM))
        outs[5].append(new_win.reshape(bs, wbuf, 2, KV_HEADS, HEAD_DIM))
        outs[7].append(new_conv)

    return (hp.reshape(bp, seq, D_MODEL), hs.reshape(bs, nq, D_MODEL)) + tuple(jnp.stack(o) for o in outs)
```

```python
import functools
import math

import numpy as np
import jax
import jax.numpy as jnp
from jax import lax
from jax.experimental import pallas as pl
from jax.experimental.pallas import tpu as pltpu

F32 = jnp.float32
BF16 = jnp.bfloat16

D_MODEL = 1024
D_CONV = 512
CONV_WIDTH = 31
CONV_HIST = CONV_WIDTH - 1
HEAD_DIM = 64
N_HEADS = 8
KV_HEADS = 2
GROUP = N_HEADS // KV_HEADS
D_ATTN = N_HEADS * HEAD_DIM
KV_DIM = 2 * KV_HEADS * HEAD_DIM
N_BRANCH = 3
CMP_STRIDE = 16
CMP_LEN = 2 * CMP_STRIDE
CMP_HIDDEN = 128
SLC_BLOCK = 64
SLC_SHIFT = 6
TOP_N = 8
WINDOW = 512
NUM_BUCKETS = 32
MAX_DISTANCE = 128
PAGE_SIZE = 128
EPS = 1e-6
NEG_INF = -1e30
FORCED_SCORE = 1e4
MASKED_SCORE = -1e4
PAD_SCORE = -3e4
TINY = 1e-30
SCALE = HEAD_DIM ** -0.5

LANES = 128
HALF = LANES // 2
TQ = 256
TK = 256
TM = 256
VMEM_LIMIT = 56 * 1024 * 1024

SEC_CONV = (0, 3 * D_CONV)
SEC_Q = (SEC_CONV[1], SEC_CONV[1] + D_ATTN)
SEC_KVC = (SEC_Q[1], SEC_Q[1] + KV_DIM)
SEC_KVS = (SEC_KVC[1], SEC_KVC[1] + KV_DIM)
SEC_KVW = (SEC_KVS[1], SEC_KVS[1] + KV_DIM)
SEC_Z = (SEC_KVW[1], SEC_KVW[1] + D_ATTN)
SEC_GATE = (SEC_Z[1], SEC_Z[1] + LANES)
D_IN = SEC_Z[1] + N_BRANCH * N_HEADS
D_IN_PAD = SEC_GATE[1]
SECTIONS = (SEC_CONV, SEC_Q, SEC_KVC, SEC_KVS, SEC_KVW, SEC_Z, SEC_GATE)


def _bucket_lower_bounds():
    n = np.arange(0, 4 * MAX_DISTANCE, dtype=np.int64)
    max_exact = NUM_BUCKETS // 2
    nf = np.maximum(n, 1).astype(np.float32)
    large = max_exact + (np.log(nf / np.float32(max_exact)) / np.float32(math.log(MAX_DISTANCE / max_exact))
                         * np.float32(NUM_BUCKETS - max_exact)).astype(np.int32)
    large = np.minimum(large, NUM_BUCKETS - 1)
    bucket = np.where(n < max_exact, n, large)
    return [int(np.argmax(bucket >= b)) for b in range(NUM_BUCKETS)]


BUCKET_LOWER = _bucket_lower_bounds()


def _dot(a, b):
    return jnp.dot(a.astype(BF16), b.astype(BF16), preferred_element_type=F32)


def _dot_nt(a, b):
    return lax.dot_general(a.astype(BF16), b.astype(BF16), (((1,), (1,)), ((), ())),
                           preferred_element_type=F32)


def _dot_split(a, b):
    hi = a.astype(BF16)
    lo = (a - hi.astype(F32)).astype(BF16)
    return (jnp.dot(hi, b, preferred_element_type=F32) + jnp.dot(lo, b, preferred_element_type=F32))


def _sigmoid(x):
    return 1.0 / (1.0 + jnp.exp(-x))


def _silu(x):
    return x * _sigmoid(x)


def _masked_softmax(s, mask):
    s = jnp.where(mask, s, NEG_INF)
    m = jnp.max(s, axis=-1, keepdims=True)
    e = jnp.exp(s - m) * mask.astype(F32)
    l = jnp.sum(e, axis=-1, keepdims=True)
    return e / jnp.maximum(l, TINY)


def _bias_of(dist, rb_ref, h):
    out = jnp.full(dist.shape, rb_ref[0, h], F32)
    for b in range(1, NUM_BUCKETS):
        out = jnp.where(dist >= BUCKET_LOWER[b], rb_ref[b, h], out)
    return out


def _bias_kernel(rb_ref, nb_ref, cb_ref, fb_ref, sba_ref, swa_ref, sbb_ref, sc_ref, *, n_qt, nc_p, past, wbuf, nq_s):
    h = pl.program_id(0)
    qi = lax.broadcasted_iota(jnp.int32, (TQ, 2 * TK), 0)
    c = lax.broadcasted_iota(jnp.int32, (TQ, 2 * TK), 1)
    nb_ref[0] = _bias_of(qi + TK - c, rb_ref, h)
    qn = lax.broadcasted_iota(jnp.int32, (TQ, nc_p), 0)
    nn = lax.broadcasted_iota(jnp.int32, (TQ, nc_p), 1)
    for t in range(n_qt):
        cb_ref[t, 0] = _bias_of(t * TQ + qn - (nn * CMP_STRIDE + CMP_LEN - 1), rb_ref, h)
    fb_ref[0] = jnp.full(fb_ref.shape[1:], rb_ref[NUM_BUCKETS - 1, h], F32)
    qs = lax.broadcasted_iota(jnp.int32, (nq_s, past), 0)
    ks = lax.broadcasted_iota(jnp.int32, (nq_s, past), 1)
    sba_ref[0] = _bias_of(past + qs - ks, rb_ref, h)
    qs = lax.broadcasted_iota(jnp.int32, (nq_s, wbuf), 0)
    ks = lax.broadcasted_iota(jnp.int32, (nq_s, wbuf), 1)
    swa_ref[0] = _bias_of(wbuf + qs - ks, rb_ref, h)
    qs = lax.broadcasted_iota(jnp.int32, (nq_s, LANES), 0)
    ks = lax.broadcasted_iota(jnp.int32, (nq_s, LANES), 1)
    sbb_ref[0] = _bias_of(qs - ks, rb_ref, h)
    nc_s = sc_ref.shape[2]
    qs = lax.broadcasted_iota(jnp.int32, (nq_s, nc_s), 0)
    ks = lax.broadcasted_iota(jnp.int32, (nq_s, nc_s), 1)
    sc_ref[0] = _bias_of(past + qs - (ks * CMP_STRIDE + CMP_LEN - 1), rb_ref, h)


def _bias_tables(rel_bias, *, seq, past, wbuf, nq_s):
    n_qt = seq // TQ
    nc_p = seq // CMP_STRIDE
    nc_s = past // CMP_STRIDE
    kern = functools.partial(_bias_kernel, n_qt=n_qt, nc_p=nc_p, past=past, wbuf=wbuf, nq_s=nq_s)
    shapes = (
        jax.ShapeDtypeStruct((N_HEADS, TQ, 2 * TK), F32),
        jax.ShapeDtypeStruct((n_qt, N_HEADS, TQ, nc_p), F32),
        jax.ShapeDtypeStruct((N_HEADS, 8, TK), F32),
        jax.ShapeDtypeStruct((N_HEADS, nq_s, past), F32),
        jax.ShapeDtypeStruct((N_HEADS, nq_s, wbuf), F32),
        jax.ShapeDtypeStruct((N_HEADS, nq_s, LANES), F32),
        jax.ShapeDtypeStruct((N_HEADS, nq_s, nc_s), F32),
    )
    out_specs = (
        pl.BlockSpec((1, TQ, 2 * TK), lambda h: (h, 0, 0)),
        pl.BlockSpec((n_qt, 1, TQ, nc_p), lambda h: (0, h, 0, 0)),
        pl.BlockSpec((1, 8, TK), lambda h: (h, 0, 0)),
        pl.BlockSpec((1, nq_s, past), lambda h: (h, 0, 0)),
        pl.BlockSpec((1, nq_s, wbuf), lambda h: (h, 0, 0)),
        pl.BlockSpec((1, nq_s, LANES), lambda h: (h, 0, 0)),
        pl.BlockSpec((1, nq_s, nc_s), lambda h: (h, 0, 0)),
    )
    return pl.pallas_call(
        kern, grid=(N_HEADS,),
        in_specs=[pl.BlockSpec(memory_space=pltpu.SMEM)],
        out_specs=out_specs, out_shape=shapes, name="bias_tables",
        compiler_params=pltpu.CompilerParams(dimension_semantics=("arbitrary",)),
    )(rel_bias)


KV_SECTIONS = (SEC_KVC, SEC_KVS, SEC_KVW)


def _in_proj_kernel(x_ref, g_ref, w_ref, *refs, kv_transposed):
    x = x_ref[...]
    u = x * lax.rsqrt(jnp.mean(x * x, axis=-1, keepdims=True) + EPS) * g_ref[...]
    ub = u.astype(BF16)
    if kv_transposed:
        wkv_t_ref, out_refs = refs[0], refs[1:]
    else:
        out_refs = refs
    for ref, sec in zip(out_refs, SECTIONS):
        if kv_transposed and sec in KV_SECTIONS:
            k = KV_SECTIONS.index(sec)
            ref[0] = lax.dot_general(wkv_t_ref[k * KV_DIM:(k + 1) * KV_DIM, :], ub, (((1,), (1,)), ((), ())),
                                     preferred_element_type=F32)
        else:
            ref[...] = jnp.dot(ub, w_ref[:, sec[0]:sec[1]], preferred_element_type=F32)


def _in_proj(x2d, g, w_pad, wkv_t=None, *, seq=None):
    n = x2d.shape[0]
    tm = min(TM, n)
    kv_transposed = wkv_t is not None
    in_specs = [pl.BlockSpec((tm, D_MODEL), lambda i: (i, 0)),
                pl.BlockSpec((1, D_MODEL), lambda i: (0, 0)),
                pl.BlockSpec((D_MODEL, D_IN_PAD), lambda i: (0, 0))]
    args = [x2d, g, w_pad]
    out_specs, out_shape = [], []
    if kv_transposed:
        in_specs.append(pl.BlockSpec((len(KV_SECTIONS) * KV_DIM, D_MODEL), lambda i: (0, 0)))
        args.append(wkv_t)
        tiles_per_seq = seq // tm
    for sec in SECTIONS:
        w = sec[1] - sec[0]
        if kv_transposed and sec in KV_SECTIONS:
            out_specs.append(pl.BlockSpec((1, w, tm), lambda i: (i // tiles_per_seq, 0, i % tiles_per_seq)))
            out_shape.append(jax.ShapeDtypeStruct((n // seq, w, seq), F32))
        else:
            out_specs.append(pl.BlockSpec((tm, w), lambda i: (i, 0)))
            out_shape.append(jax.ShapeDtypeStruct((n, w), F32))
    return pl.pallas_call(
        functools.partial(_in_proj_kernel, kv_transposed=kv_transposed), grid=(n // tm,),
        in_specs=in_specs, out_specs=out_specs, out_shape=out_shape, name="in_proj",
        compiler_params=pltpu.CompilerParams(dimension_semantics=("arbitrary",), vmem_limit_bytes=VMEM_LIMIT),
    )(*args)


HIST_PAD = 32


def _conv_kernel(c3_ref, hist_ref, w_ref, b_ref, lg_ref, lb_ref, out_ref, new_ref, xbuf, *, tt, n_t):
    t = pl.program_id(1)
    off = HIST_PAD - CONV_HIST

    @pl.when(t == 0)
    def _():
        xbuf[0:HIST_PAD, :] = jnp.zeros((HIST_PAD, D_CONV), F32)
        xbuf[off:HIST_PAD, :] = hist_ref[0]

    glu = c3_ref[0, :, 0:D_CONV] * _sigmoid(c3_ref[0, :, D_CONV:2 * D_CONV])
    xbuf[HIST_PAD:HIST_PAD + tt, :] = glu
    acc = jnp.zeros((tt, D_CONV), F32) + b_ref[...]
    for k in range(CONV_WIDTH):
        acc = acc + xbuf[off + k:off + k + tt, :] * w_ref[k:k + 1, :]
    mu = jnp.mean(acc, axis=-1, keepdims=True)
    xc = acc - mu
    var = jnp.mean(xc * xc, axis=-1, keepdims=True)
    y = xc * lax.rsqrt(var + EPS) * lg_ref[...] + lb_ref[...]
    out_ref[0] = _silu(y) * _silu(c3_ref[0, :, 2 * D_CONV:3 * D_CONV])
    tail = xbuf[off + tt:HIST_PAD + tt, :]

    @pl.when(t == n_t - 1)
    def _():
        new_ref[0] = tail

    if n_t > 1:
        @pl.when(t < n_t - 1)
        def _():
            xbuf[off:HIST_PAD, :] = tail


def _conv_module(c3, hist, conv_w, conv_b, ln_g, ln_b):
    bsz, t_len, _ = c3.shape
    tt = min(256, t_len)
    n_t = t_len // tt
    kern = functools.partial(_conv_kernel, tt=tt, n_t=n_t)
    vec = pl.BlockSpec((1, D_CONV), lambda b, t: (0, 0))
    return pl.pallas_call(
        kern, grid=(bsz, n_t),
        in_specs=[pl.BlockSpec((1, tt, 3 * D_CONV), lambda b, t: (b, t, 0)),
                  pl.BlockSpec((1, CONV_HIST, D_CONV), lambda b, t: (b, 0, 0)),
                  pl.BlockSpec((CONV_WIDTH, D_CONV), lambda b, t: (0, 0)),
                  vec, vec, vec],
        out_specs=[pl.BlockSpec((1, tt, D_CONV), lambda b, t: (b, t, 0)),
                   pl.BlockSpec((1, CONV_HIST, D_CONV), lambda b, t: (b, 0, 0))],
        out_shape=[jax.ShapeDtypeStruct((bsz, t_len, D_CONV), F32),
                   jax.ShapeDtypeStruct((bsz, CONV_HIST, D_CONV), F32)],
        scratch_shapes=[pltpu.VMEM((HIST_PAD + tt, D_CONV), F32)],
        name="conv_module",
        compiler_params=pltpu.CompilerParams(dimension_semantics=("arbitrary", "arbitrary")),
    )(c3, hist, conv_w, conv_b, ln_g, ln_b)


HALVES_PER_PAGE = PAGE_SIZE // CMP_STRIDE
N_TG = 2 * KV_HEADS
N_LT = KV_DIM // LANES
CMP_FEAT = CMP_STRIDE * HEAD_DIM


def _compress_body(page_tile, n_pages, pe_ref, w1_ref, w2_ref, out_ref, x_scr, y_scr, h_scr):
    n_half = n_pages * HALVES_PER_PAGE
    for p in range(n_pages):
        for t in range(N_LT):
            x_scr[t, p * PAGE_SIZE:(p + 1) * PAGE_SIZE, :] = page_tile(p, t).T
    low = lax.broadcasted_iota(jnp.int32, (n_half, LANES), 1) < HALF
    for j in range(CMP_STRIDE // 2):
        for t in range(N_LT):
            a = x_scr[t, pl.ds(2 * j, n_half, stride=CMP_STRIDE), :]
            b = x_scr[t, pl.ds(2 * j + 1, n_half, stride=CMP_STRIDE), :]
            y_scr[2 * t, :, j * LANES:(j + 1) * LANES] = jnp.where(low, a, pltpu.roll(b, HALF, 1))
            y_scr[2 * t + 1, :, j * LANES:(j + 1) * LANES] = jnp.where(low, pltpu.roll(a, HALF, 1), b)
    h_scr[n_half:n_half + 8, :] = jnp.zeros((8, 2 * CMP_HIDDEN), F32)
    acts = []
    for tg in range(N_TG):
        t = tg // KV_HEADS
        w1 = w1_ref[t]
        c = jnp.dot(pe_ref[t].astype(BF16), w1, preferred_element_type=F32)
        cvec = c[0:1, 0:CMP_HIDDEN] + c[1:2, CMP_HIDDEN:2 * CMP_HIDDEN]
        h_scr[0:n_half, :] = jnp.dot(y_scr[tg].astype(BF16), w1, preferred_element_type=F32)
        hid = h_scr[0:n_half, 0:CMP_HIDDEN] + h_scr[1:n_half + 1, CMP_HIDDEN:2 * CMP_HIDDEN] + cvec
        acts.append(_silu(hid))
    out = jnp.dot(jnp.concatenate(acts, axis=1).astype(BF16), w2_ref[...], preferred_element_type=F32)
    row = lax.broadcasted_iota(jnp.int32, out.shape, 0)
    out_ref[0] = jnp.where(row < n_half - 1, out, 0.0)


def _compress_paged_kernel(pt_ref, *refs, n_pages):
    del pt_ref
    pages = refs[:n_pages]

    def page_tile(p, t):
        return pages[p][0, 0, t].reshape(LANES, PAGE_SIZE)

    _compress_body(page_tile, n_pages, *refs[n_pages:])


def _compress_seq_kernel(kv_ref, *refs, n_pages):
    def page_tile(p, t):
        return kv_ref[0, t * LANES:(t + 1) * LANES, p * PAGE_SIZE:(p + 1) * PAGE_SIZE]

    _compress_body(page_tile, n_pages, *refs)


def _compress_specs(n_half):
    const = lambda shape: pl.BlockSpec(shape, lambda *a: (0,) * len(shape))
    weight_specs = [const((2, 8, CMP_FEAT)), const((2, CMP_FEAT, 2 * CMP_HIDDEN)), const((N_TG * CMP_HIDDEN, KV_DIM))]
    out_spec = pl.BlockSpec((1, n_half, KV_DIM), lambda b, *a: (b, 0, 0))
    scratch = [pltpu.VMEM((N_LT, n_half * CMP_STRIDE, LANES), F32),
               pltpu.VMEM((N_TG, n_half, CMP_FEAT), F32),
               pltpu.VMEM((n_half + 8, 2 * CMP_HIDDEN), F32)]
    return weight_specs, out_spec, scratch


def _compress_paged(cache_t, layer, table, pe_pad, w1cat, w2big):
    bsz, n_pages = table.shape
    n_half = n_pages * HALVES_PER_PAGE
    page_specs = [pl.BlockSpec((1, 1, 2, KV_HEADS, HEAD_DIM, PAGE_SIZE),
                               lambda b, pt, p=p: (layer, pt[b, p], 0, 0, 0, 0)) for p in range(n_pages)]
    weight_specs, out_spec, scratch = _compress_specs(n_half)
    grid_spec = pltpu.PrefetchScalarGridSpec(
        num_scalar_prefetch=1, grid=(bsz,), in_specs=page_specs + weight_specs, out_specs=out_spec,
        scratch_shapes=scratch)
    return pl.pallas_call(
        functools.partial(_compress_paged_kernel, n_pages=n_pages), grid_spec=grid_spec,
        out_shape=jax.ShapeDtypeStruct((bsz, n_half, KV_DIM), F32), name="compress_paged",
        compiler_params=pltpu.CompilerParams(dimension_semantics=("arbitrary",), vmem_limit_bytes=VMEM_LIMIT),
    )(table, *([cache_t] * n_pages), pe_pad, w1cat, w2big)


def _compress_seq(kv_t, pe_pad, w1cat, w2big):
    bsz, _, seq = kv_t.shape
    n_pages = seq // PAGE_SIZE
    n_half = n_pages * HALVES_PER_PAGE
    weight_specs, out_spec, scratch = _compress_specs(n_half)
    return pl.pallas_call(
        functools.partial(_compress_seq_kernel, n_pages=n_pages), grid=(bsz,),
        in_specs=[pl.BlockSpec((1, KV_DIM, seq), lambda b: (b, 0, 0))] + weight_specs, out_specs=out_spec,
        scratch_shapes=scratch,
        out_shape=jax.ShapeDtypeStruct((bsz, n_half, KV_DIM), F32), name="compress_seq",
        compiler_params=pltpu.CompilerParams(dimension_semantics=("arbitrary",), vmem_limit_bytes=VMEM_LIMIT),
    )(kv_t, pe_pad, w1cat, w2big)


def _head_rows(q, g, low):
    parts = []
    for r in range(GROUP):
        h = GROUP * g + r
        tile = q[:, LANES * (h // 2):LANES * (h // 2 + 1)]
        if (h % 2) != g:
            tile = pltpu.roll(tile, HALF, 1)
        parts.append(jnp.where(low, tile, 0.0) if g == 0 else jnp.where(low, 0.0, tile))
    return parts


def _assemble_heads(o_heads, low):
    tiles = []
    for j in range(N_HEADS // 2):
        a, b = o_heads[2 * j], o_heads[2 * j + 1]
        if (2 * j) // GROUP == 0:
            tiles.append(jnp.where(low, a, pltpu.roll(b, HALF, 1)))
        else:
            tiles.append(jnp.where(low, pltpu.roll(a, HALF, 1), b))
    return jnp.concatenate(tiles, axis=1)


def _select_blocks_t(score_t, allowed_t, n_blk, top_n):
    idx = lax.broadcasted_iota(jnp.int32, score_t.shape, 0)
    cnt = jnp.zeros(score_t.shape, jnp.int32)
    for i in range(n_blk):
        row = score_t[i:i + 1, :]
        ahead = (row > score_t) | ((row == score_t) & (idx > i))
        cnt = cnt + ahead.astype(jnp.int32)
    return (cnt < top_n) & allowed_t


def _flash_step(qg, k_tile, v_tile, bias, mask, state):
    m, l, acc = state
    s = _dot(qg, k_tile).reshape(GROUP, TQ, TK) + bias
    s = jnp.where(mask[None], s, NEG_INF)
    m_new = jnp.maximum(m, jnp.max(s, axis=-1, keepdims=True))
    alpha = jnp.exp(m - m_new)
    p = jnp.exp(s - m_new) * mask[None].astype(F32)
    l = alpha * l + jnp.sum(p, axis=-1, keepdims=True)
    pv = _dot_nt(p.reshape(GROUP * TQ, TK), v_tile).reshape(GROUP, TQ, LANES)
    return m_new, l, alpha * acc + pv


def _flash_init():
    return (jnp.full((GROUP, TQ, 1), NEG_INF, F32), jnp.zeros((GROUP, TQ, 1), F32),
            jnp.zeros((GROUP, TQ, LANES), F32))


def _flash_out(state):
    _, l, acc = state
    return acc / jnp.maximum(l, TINY)


def _attn_prompt_kernel(q_ref, gate_ref, z_ref, kcvc_ref, kvs_ref, kvw_ref, nb_ref, cb_ref, fb_ref, cover_ref,
                        out_ref, *, n_slc):
    qt = pl.program_id(1)
    nc = kcvc_ref.shape[1]
    low = lax.broadcasted_iota(jnp.int32, (TQ, LANES), 1) < HALF
    qi = lax.broadcasted_iota(jnp.int32, (TQ, TK), 0)
    ki = lax.broadcasted_iota(jnp.int32, (TQ, TK), 1)
    causal = qi >= ki
    q = q_ref[...]
    gsig = _sigmoid(gate_ref[...])
    kc = kcvc_ref[0, :, 0:LANES].astype(BF16)
    vc = kcvc_ref[0, :, LANES:2 * LANES].astype(BF16)
    qpos_c = qt * TQ + lax.broadcasted_iota(jnp.int32, (TQ, nc), 0)
    c_end = lax.broadcasted_iota(jnp.int32, (TQ, nc), 1) * CMP_STRIDE + (CMP_LEN - 1)
    cmask = qpos_c >= c_end
    blk_t = lax.broadcasted_iota(jnp.int32, (n_slc, TQ), 0)
    cur_t = jnp.right_shift(qt * TQ + lax.broadcasted_iota(jnp.int32, (n_slc, TQ), 1), SLC_SHIFT)
    allowed_t = blk_t <= cur_t
    forced_t = (blk_t == 0) | (blk_t == cur_t) | (blk_t == cur_t - 1)
    e_row = lax.broadcasted_iota(jnp.int32, (LANES, TK), 0)
    e_col = jnp.right_shift(lax.broadcasted_iota(jnp.int32, (LANES, TK), 1), SLC_SHIFT)
    kt_prev = jnp.maximum(qt - 1, 0)
    kt_far = jnp.maximum(qt - 2, 0)

    def kv_tile(ref, kt):
        start = pl.multiple_of(kt * TK, TK)
        return (ref[0, 0:LANES, pl.ds(start, TK)].astype(BF16),
                ref[0, LANES:2 * LANES, pl.ds(start, TK)].astype(BF16))

    o_heads = []
    for g in range(KV_HEADS):
        hs = slice(GROUP * g, GROUP * (g + 1))
        qg = (jnp.concatenate(_head_rows(q, g, low), axis=0) * SCALE).astype(BF16)
        s_c = _dot_nt(qg, kc).reshape(GROUP, TQ, nc) + cb_ref[0, hs]
        p_c = _masked_softmax(s_c, cmask[None])
        o_cmp = _dot(p_c.reshape(GROUP * TQ, nc), vc).reshape(GROUP, TQ, LANES)
        imp = _dot_split(jnp.sum(p_c, axis=0), cover_ref[...])
        imp_t = imp.T[0:n_slc, :]
        score_t = jnp.where(allowed_t, jnp.where(forced_t, FORCED_SCORE, imp_t), MASKED_SCORE)
        sel_t = _select_blocks_t(score_t, allowed_t, n_slc, min(TOP_N, n_slc)).astype(F32)
        sel = jnp.concatenate([sel_t, jnp.zeros((LANES - n_slc, TQ), F32)], axis=0).T.astype(BF16)

        def sel_mask(kt):
            expand = (e_row == kt * (TK // SLC_BLOCK) + e_col).astype(BF16)
            return jnp.dot(sel, expand, preferred_element_type=F32) > 0.5

        far_bias = fb_ref[hs, 0:1, :]

        def far_body(kt, state):
            k_t, v_t = kv_tile(kvs_ref, kt)
            return _flash_step(qg, k_t, v_t, far_bias, sel_mask(kt), state)

        state = lax.fori_loop(0, jnp.maximum(qt - 1, 0), far_body, _flash_init())
        k_t, v_t = kv_tile(kvs_ref, kt_prev)
        state = _flash_step(qg, k_t, v_t, nb_ref[hs, :, 0:TK], sel_mask(kt_prev) & (qt >= 1), state)
        k_t, v_t = kv_tile(kvs_ref, qt)
        state = _flash_step(qg, k_t, v_t, nb_ref[hs, :, TK:2 * TK], sel_mask(qt) & causal, state)
        o_slc = _flash_out(state)
        state = _flash_init()
        k_t, v_t = kv_tile(kvw_ref, kt_far)
        state = _flash_step(qg, k_t, v_t, far_bias, (ki > qi) & (qt >= 2), state)
        k_t, v_t = kv_tile(kvw_ref, kt_prev)
        state = _flash_step(qg, k_t, v_t, nb_ref[hs, :, 0:TK], (ki >= 0) & (qt >= 1), state)
        k_t, v_t = kv_tile(kvw_ref, qt)
        state = _flash_step(qg, k_t, v_t, nb_ref[hs, :, TK:2 * TK], causal, state)
        o_win = _flash_out(state)
        for r in range(GROUP):
            h = GROUP * g + r
            o_heads.append(gsig[:, h:h + 1] * o_cmp[r] + gsig[:, N_HEADS + h:N_HEADS + h + 1] * o_slc[r]
                           + gsig[:, 2 * N_HEADS + h:2 * N_HEADS + h + 1] * o_win[r])
    out_ref[...] = _assemble_heads(o_heads, low) * _silu(z_ref[...])


def _attn_prompt(q2d, gate2d, z2d, kcvc, kvs, kvw, nb, cb, fb, cover, *, bsz, seq):
    assert WINDOW == 2 * TK and seq % TQ == 0
    n_qt = seq // TQ
    nc = seq // CMP_STRIDE
    n_slc = seq // SLC_BLOCK
    kern = functools.partial(_attn_prompt_kernel, n_slc=n_slc)
    tok = lambda w: pl.BlockSpec((TQ, w), lambda b, t: (b * n_qt + t, 0))
    per_b = lambda r, w: pl.BlockSpec((1, r, w), lambda b, t: (b, 0, 0))
    return pl.pallas_call(
        kern, grid=(bsz, n_qt),
        in_specs=[tok(D_ATTN), tok(LANES), tok(D_ATTN),
                  per_b(nc, KV_DIM), per_b(KV_DIM, seq), per_b(KV_DIM, seq),
                  pl.BlockSpec((N_HEADS, TQ, 2 * TK), lambda b, t: (0, 0, 0)),
                  pl.BlockSpec((1, N_HEADS, TQ, nc), lambda b, t: (t, 0, 0, 0)),
                  pl.BlockSpec((N_HEADS, 8, TK), lambda b, t: (0, 0, 0)),
                  pl.BlockSpec((nc, LANES), lambda b, t: (0, 0))],
        out_specs=tok(D_ATTN),
        out_shape=jax.ShapeDtypeStruct((bsz * seq, D_ATTN), F32),
        name="attn_prompt",
        compiler_params=pltpu.CompilerParams(dimension_semantics=("arbitrary", "arbitrary"),
                                             vmem_limit_bytes=VMEM_LIMIT),
    )(q2d, gate2d, z2d, kcvc, kvs, kvw, nb, cb, fb, cover)


def _attn_sample_kernel(pt_ref, *refs, n_pages, nq):
    del pt_ref
    pages = refs[:n_pages]
    (win_ref, kvsn_ref, kvwn_ref, q_ref, gate_ref, z_ref, kcvc_ref, sba_ref, swa_ref, sbb_ref, sc_ref,
     cover_ref, expand_ref, out_ref, nwin_ref) = refs[n_pages:]
    past = n_pages * PAGE_SIZE
    wbuf = win_ref.shape[-1]
    nc = kcvc_ref.shape[1]
    rows = KV_HEADS * GROUP * nq
    cur = past // SLC_BLOCK
    n_slc = cur + 1
    low = lax.broadcasted_iota(jnp.int32, (nq, LANES), 1) < HALF
    q = q_ref[0]
    q_left = (jnp.concatenate(_head_rows(q, 0, low) + _head_rows(q, 1, low), axis=0) * SCALE).astype(BF16)
    qi = lax.broadcasted_iota(jnp.int32, (rows, LANES), 0) & (nq - 1)
    ki = lax.broadcasted_iota(jnp.int32, (rows, LANES), 1)
    new_mask = (ki <= qi) & (ki < nq)
    pad_rows = jnp.zeros((LANES - nq, LANES), F32)

    def new_tile(ref, t):
        return jnp.concatenate([ref[0, :, t * LANES:(t + 1) * LANES], pad_rows], axis=0)
    kc = kcvc_ref[0, :, 0:LANES].astype(BF16)
    vc = kcvc_ref[0, :, LANES:2 * LANES].astype(BF16)
    n_idx = lax.broadcasted_iota(jnp.int32, (rows, nc), 1)
    p_c = _masked_softmax(_dot_nt(q_left, kc) + sc_ref[...], n_idx < nc - 1)
    o_cmp = _dot(p_c, vc)
    blk = lax.broadcasted_iota(jnp.int32, (nq, LANES), 1)
    is_blk = blk < n_slc
    forced = (blk == 0) | (blk == cur) | (blk == cur - 1)
    sel_rows = []
    for g in range(KV_HEADS):
        p_sum = p_c[g * GROUP * nq:g * GROUP * nq + nq]
        for r in range(1, GROUP):
            p_sum = p_sum + p_c[(g * GROUP + r) * nq:(g * GROUP + r + 1) * nq]
        imp = _dot_split(p_sum, cover_ref[...])
        score = jnp.where(is_blk, jnp.where(forced, FORCED_SCORE, imp), PAD_SCORE)
        cnt = jnp.zeros((nq, LANES), jnp.int32)
        for i in range(n_slc):
            col = score[:, i:i + 1]
            cnt = cnt + ((col > score) | ((col == score) & (blk > i))).astype(jnp.int32)
        sel_g = ((cnt < min(TOP_N, n_slc)) & is_blk).astype(F32)
        sel_rows += [sel_g] * GROUP
    sel = jnp.concatenate(sel_rows, axis=0)
    k_pages = [pages[p][0, 0, 0].reshape(LANES, PAGE_SIZE).astype(BF16) for p in range(n_pages)]
    v_pages = [pages[p][0, 0, 1].reshape(LANES, PAGE_SIZE).astype(BF16) for p in range(n_pages)]
    k_new = new_tile(kvsn_ref, 0).astype(BF16)
    v_new = new_tile(kvsn_ref, 1).astype(BF16)
    s_a = jnp.concatenate([_dot(q_left, k) for k in k_pages], axis=1) + sba_ref[...]
    mask_a = jnp.dot(sel.astype(BF16), expand_ref[...], preferred_element_type=F32) > 0.5
    s_b = _dot_nt(q_left, k_new) + sbb_ref[...]
    mask_b = new_mask & (sel[:, cur:cur + 1] > 0.5)
    s_a = jnp.where(mask_a, s_a, NEG_INF)
    s_b = jnp.where(mask_b, s_b, NEG_INF)
    m = jnp.maximum(jnp.max(s_a, axis=-1, keepdims=True), jnp.max(s_b, axis=-1, keepdims=True))
    p_a = jnp.exp(s_a - m) * mask_a.astype(F32)
    p_b = jnp.exp(s_b - m) * mask_b.astype(F32)
    l = jnp.sum(p_a, axis=-1, keepdims=True) + jnp.sum(p_b, axis=-1, keepdims=True)
    acc = _dot(p_b, v_new)
    for p in range(n_pages):
        acc = acc + _dot_nt(p_a[:, p * PAGE_SIZE:(p + 1) * PAGE_SIZE], v_pages[p])
    o_slc = acc / jnp.maximum(l, TINY)
    win_t = [win_ref[0, 0, t].reshape(LANES, wbuf) for t in range(2)]
    new_w = [new_tile(kvwn_ref, t) for t in range(2)]
    lane_w = lax.broadcasted_iota(jnp.int32, (LANES, wbuf), 1)
    for t in range(2):
        placed = jnp.concatenate([jnp.zeros((LANES, wbuf - LANES), F32), pltpu.roll(new_w[t].T, LANES - nq, 1)],
                                 axis=1)
        shifted = pltpu.roll(win_t[t], wbuf - nq, 1)
        nwin_ref[0, t] = jnp.where(lane_w < wbuf - nq, shifted, placed).reshape(KV_HEADS, HEAD_DIM, wbuf)
    kw_t, vw_t = win_t[0].astype(BF16), win_t[1].astype(BF16)
    kw_new, vw_new = new_w[0].astype(BF16), new_w[1].astype(BF16)
    jw = lax.broadcasted_iota(jnp.int32, (rows, wbuf), 1)
    qw = lax.broadcasted_iota(jnp.int32, (rows, wbuf), 0) & (nq - 1)
    mask_wa = jw > qw
    s_wa = jnp.where(mask_wa, _dot(q_left, kw_t) + swa_ref[...], NEG_INF)
    s_wb = jnp.where(new_mask, _dot_nt(q_left, kw_new) + sbb_ref[...], NEG_INF)
    m = jnp.maximum(jnp.max(s_wa, axis=-1, keepdims=True), jnp.max(s_wb, axis=-1, keepdims=True))
    p_wa = jnp.exp(s_wa - m) * mask_wa.astype(F32)
    p_wb = jnp.exp(s_wb - m) * new_mask.astype(F32)
    l = jnp.sum(p_wa, axis=-1, keepdims=True) + jnp.sum(p_wb, axis=-1, keepdims=True)
    acc = _dot_nt(p_wa, vw_t) + _dot(p_wb, vw_new)
    o_win = acc / jnp.maximum(l, TINY)
    gsig = _sigmoid(gate_ref[0])
    o_heads = []
    for h in range(N_HEADS):
        rs = slice(h * nq, (h + 1) * nq)
        o_heads.append(gsig[:, h:h + 1] * o_cmp[rs] + gsig[:, N_HEADS + h:N_HEADS + h + 1] * o_slc[rs]
                       + gsig[:, 2 * N_HEADS + h:2 * N_HEADS + h + 1] * o_win[rs])
    out_ref[0] = _assemble_heads(o_heads, low) * _silu(z_ref[0])


def _attn_sample(cache_t, win_t, layer, table, kvs_new, kvw_new, q, gate, z, kcvc, sba, swa, sbb, sc, cover, expand):
    bsz, n_pages = table.shape
    nq = q.shape[1]
    wbuf = win_t.shape[-1]
    nc = kcvc.shape[1]
    past = n_pages * PAGE_SIZE
    rows = N_HEADS * nq
    assert nq <= SLC_BLOCK and nq & (nq - 1) == 0 and past % SLC_BLOCK == 0 and wbuf == WINDOW
    kern = functools.partial(_attn_sample_kernel, n_pages=n_pages, nq=nq)
    page_specs = [pl.BlockSpec((1, 1, 2, KV_HEADS, HEAD_DIM, PAGE_SIZE),
                               lambda b, pt, p=p: (layer, pt[b, p], 0, 0, 0, 0)) for p in range(n_pages)]
    per_b = lambda r, w: pl.BlockSpec((1, r, w), lambda b, pt: (b, 0, 0))
    const = lambda r, w: pl.BlockSpec((r, w), lambda b, pt: (0, 0))
    grid_spec = pltpu.PrefetchScalarGridSpec(
        num_scalar_prefetch=1, grid=(bsz,),
        in_specs=page_specs + [
            pl.BlockSpec((1, 1, 2, KV_HEADS, HEAD_DIM, wbuf), lambda b, pt: (layer, b, 0, 0, 0, 0)),
            per_b(nq, KV_DIM), per_b(nq, KV_DIM), per_b(nq, D_ATTN), per_b(nq, LANES),
            per_b(nq, D_ATTN), per_b(nc, KV_DIM),
            const(rows, past), const(rows, wbuf), const(rows, LANES), const(rows, nc),
            const(nc, LANES), const(LANES, past)],
        out_specs=[per_b(nq, D_ATTN),
                   pl.BlockSpec((1, 2, KV_HEADS, HEAD_DIM, wbuf), lambda b, pt: (b, 0, 0, 0, 0))])
    return pl.pallas_call(
        kern, grid_spec=grid_spec,
        out_shape=[jax.ShapeDtypeStruct((bsz, nq, D_ATTN), F32),
                   jax.ShapeDtypeStruct((bsz, 2, KV_HEADS, HEAD_DIM, wbuf), F32)],
        name="attn_sample",
        compiler_params=pltpu.CompilerParams(dimension_semantics=("arbitrary",), vmem_limit_bytes=VMEM_LIMIT),
    )(table, *([cache_t] * n_pages), win_t, kvs_new, kvw_new, q, gate, z, kcvc, sba, swa, sbb, sc, cover, expand)


def _out_proj_kernel(h_ref, conv_ref, attn_ref, ple_ref, wo_ref, wg_ref, wp_ref, fg_ref, out_ref, *, final):
    h = h_ref[...]
    h = h + jnp.dot(conv_ref[...].astype(BF16), wo_ref[0:D_CONV, :], preferred_element_type=F32)
    h = h + jnp.dot(attn_ref[...].astype(BF16), wo_ref[D_CONV:D_CONV + D_ATTN, :], preferred_element_type=F32)
    gate = _sigmoid(jnp.dot(h.astype(BF16), wg_ref[...], preferred_element_type=F32))
    h = h + gate * jnp.dot(ple_ref[0].astype(BF16), wp_ref[...], preferred_element_type=F32)
    if final:
        h = h * lax.rsqrt(jnp.mean(h * h, axis=-1, keepdims=True) + EPS) * fg_ref[...]
    out_ref[...] = h


def _out_proj(h2d, conv2d, attn2d, ple3d, layer, wo, wg, wp, fg, *, final):
    n = h2d.shape[0]
    tm = min(TM, n)
    ple_dim = ple3d.shape[-1]
    kern = functools.partial(_out_proj_kernel, final=final)
    tok = lambda w: pl.BlockSpec((tm, w), lambda i: (i, 0))
    const = lambda r, w: pl.BlockSpec((r, w), lambda i: (0, 0))
    return pl.pallas_call(
        kern, grid=(n // tm,),
        in_specs=[tok(D_MODEL), tok(D_CONV), tok(D_ATTN), pl.BlockSpec((1, tm, ple_dim), lambda i: (layer, i, 0)),
                  const(D_CONV + D_ATTN, D_MODEL), const(D_MODEL, D_MODEL), const(ple_dim, D_MODEL),
                  const(1, D_MODEL)],
        out_specs=tok(D_MODEL),
        out_shape=jax.ShapeDtypeStruct((n, D_MODEL), F32),
        name="out_proj",
        compiler_params=pltpu.CompilerParams(dimension_semantics=("arbitrary",), vmem_limit_bytes=VMEM_LIMIT),
    )(h2d, conv2d, attn2d, ple3d, wo, wg, wp, fg)


def _cover_matrix(n_cmp_rows, n_cmp, n_slc):
    c_start = np.arange(n_cmp_rows) * CMP_STRIDE
    c_end = c_start + CMP_LEN - 1
    s_start = np.arange(LANES) * SLC_BLOCK
    cover = (c_start[:, None] < s_start[None, :] + SLC_BLOCK) & (c_end[:, None] >= s_start[None, :])
    cover &= (np.arange(n_cmp_rows)[:, None] < n_cmp) & (np.arange(LANES)[None, :] < n_slc)
    return jnp.asarray(cover, dtype=BF16)


def _expand_matrix(past):
    e = np.arange(LANES)[:, None] == (np.arange(past)[None, :] // SLC_BLOCK)
    return jnp.asarray(e, dtype=BF16)


def kernel(x_prompt, x_sample, cache_cmp_kv, cache_slc_kv, page_table, state_win_kv, state_conv, p_prompt, p_sample, norm_g, w_in, conv_w, conv_b, conv_ln_g, conv_ln_b, cmp_pe, cmp_w1, cmp_w2, w_out, w_ple, w_ple_gate, rel_bias, final_norm_g):
    bp, seq, _ = x_prompt.shape
    bs, nq, _ = x_sample.shape
    depth = w_in.shape[0]
    n_pages = page_table.shape[1]
    past = n_pages * PAGE_SIZE
    wbuf = state_win_kv.shape[2]
    n_pool = cache_cmp_kv.shape[1]
    win_p = min(WINDOW, seq)

    nb, cb, fb, sba, swa, sbb, sc = _bias_tables(rel_bias, seq=seq, past=past, wbuf=wbuf, nq_s=nq)
    rows = N_HEADS * nq
    sba, swa, sbb, sc = (a.reshape(rows, a.shape[-1]) for a in (sba, swa, sbb, sc))
    nc_p, nc_s = seq // CMP_STRIDE, past // CMP_STRIDE
    cover_p = _cover_matrix(nc_p, nc_p - 1, seq // SLC_BLOCK)
    cover_s = _cover_matrix(nc_s, nc_s - 1, past // SLC_BLOCK + 1)
    expand_s = _expand_matrix(past)
    conv_zero = jnp.zeros((bp, CONV_HIST, D_CONV), F32)
    fg = final_norm_g.reshape(1, D_MODEL)
    to_t = lambda a: jnp.transpose(a, (0, 1, 3, 4, 5, 2))
    from_t = lambda a: jnp.transpose(a, (0, 1, 5, 2, 3, 4))
    cmp_t, slc_t, win_t = to_t(cache_cmp_kv), to_t(cache_slc_kv), to_t(state_win_kv)
    ple_p = p_prompt.reshape(depth, bp * seq, -1)
    ple_s = p_sample.reshape(depth, bs * nq, -1)

    hp = x_prompt.reshape(bp * seq, D_MODEL)
    hs = x_sample.reshape(bs * nq, D_MODEL)
    outs = [[] for _ in range(8)]
    for i in range(depth):
        w_pad = jnp.pad(w_in[i], ((0, 0), (0, D_IN_PAD - D_IN))).astype(BF16)
        wkv_t = w_in[i][:, SEC_KVC[0]:SEC_KVW[1]].T.astype(BF16)
        g = norm_g[i].reshape(1, D_MODEL)
        w1 = cmp_w1[i].reshape(2, 2, CMP_FEAT, CMP_HIDDEN)
        w1cat = jnp.concatenate([w1[:, 0], w1[:, 1]], axis=-1).astype(BF16)
        pe_pad = jnp.pad(cmp_pe[i].reshape(2, 2, CMP_FEAT), ((0, 0), (0, 6), (0, 0)))
        w2big = jnp.zeros((N_TG * CMP_HIDDEN, KV_DIM), F32)
        for tg in range(N_TG):
            w2big = w2big.at[tg * CMP_HIDDEN:(tg + 1) * CMP_HIDDEN, tg * HEAD_DIM:(tg + 1) * HEAD_DIM].set(
                cmp_w2[i, tg // KV_HEADS])
        w2big = w2big.astype(BF16)
        wo, wg, wp = w_out[i].astype(BF16), w_ple_gate[i].astype(BF16), w_ple[i].astype(BF16)
        cw, cbias = conv_w[i], conv_b[i].reshape(1, D_CONV)
        lg, lb = conv_ln_g[i].reshape(1, D_CONV), conv_ln_b[i].reshape(1, D_CONV)
        final = i == depth - 1

        c3, q, kvc_t, kvs_t, kvw_t, z, gate = _in_proj(hp, g, w_pad, wkv_t, seq=seq)
        conv_out, new_conv = _conv_module(c3.reshape(bp, seq, 3 * D_CONV), conv_zero, cw, cbias, lg, lb)
        kcvc = _compress_seq(kvc_t, pe_pad, w1cat, w2big)
        attn = _attn_prompt(q, gate, z, kcvc, kvs_t, kvw_t, nb, cb, fb, cover_p, bsz=bp, seq=seq)
        hp = _out_proj(hp, conv_out.reshape(bp * seq, D_CONV), attn, ple_p, i, wo, wg, wp, fg, final=final)
        six_d = lambda a: a.reshape(bp, 2, KV_HEADS, HEAD_DIM, a.shape[-1])
        outs[0].append(six_d(kvc_t))
        outs[2].append(six_d(kvs_t))
        outs[4].append(six_d(kvw_t[:, :, seq - win_p:]))
        outs[6].append(new_conv)

        c3, q, kvc, kvs, kvw, z, gate = _in_proj(hs, g, w_pad)
        conv_out, new_conv = _conv_module(c3.reshape(bs, nq, 3 * D_CONV), state_conv[i], cw, cbias, lg, lb)
        kcvc = _compress_paged(cmp_t, i, page_table, pe_pad, w1cat, w2big)
        attn, new_win = _attn_sample(
            slc_t, win_t, i, page_table, kvs.reshape(bs, nq, KV_DIM), kvw.reshape(bs, nq, KV_DIM),
            q.reshape(bs, nq, D_ATTN), gate.reshape(bs, nq, LANES), z.reshape(bs, nq, D_ATTN), kcvc,
            sba, swa, sbb, sc, cover_s, expand_s)
        hs = _out_proj(hs, conv_out.reshape(bs * nq, D_CONV), attn.reshape(bs * nq, D_ATTN), ple_s, i,
                       wo, wg, wp, fg, final=final)
        outs[1].append(kvc.reshape(bs, nq, 2, KV_HEADS, HEAD_DIM))
        outs[3].append(kvs.reshape(bs, nq, 2, KV_HEADS, HEAD_DIM))
        outs[5].append(new_win)
        outs[7].append(new_conv)

    stacked = [jnp.stack(o) for o in outs]
    for k in (0, 2, 4, 5):
        stacked[k] = from_t(stacked[k])
    return (hp.reshape(bp, seq, D_MODEL), hs.reshape(bs, nq, D_MODEL)) + tuple(stacked)
```

```python
import functools
import math

import numpy as np
import jax
import jax.numpy as jnp
from jax import lax
from jax.experimental import pallas as pl
from jax.experimental.pallas import tpu as pltpu

F32 = jnp.float32
BF16 = jnp.bfloat16

D_MODEL = 1024
D_CONV = 512
CONV_WIDTH = 31
CONV_HIST = CONV_WIDTH - 1
HEAD_DIM = 64
N_HEADS = 8
KV_HEADS = 2
GROUP = N_HEADS // KV_HEADS
D_ATTN = N_HEADS * HEAD_DIM
KV_DIM = 2 * KV_HEADS * HEAD_DIM
N_BRANCH = 3
CMP_STRIDE = 16
CMP_LEN = 2 * CMP_STRIDE
CMP_HIDDEN = 128
SLC_BLOCK = 64
SLC_SHIFT = 6
TOP_N = 8
WINDOW = 512
NUM_BUCKETS = 32
MAX_DISTANCE = 128
PAGE_SIZE = 128
EPS = 1e-6
NEG_INF = -1e30
FORCED_SCORE = 1e4
MASKED_SCORE = -1e4
PAD_SCORE = -3e4
TINY = 1e-30
SCALE = HEAD_DIM ** -0.5

LANES = 128
HALF = LANES // 2
TQ = 256
TK = 256
TM = 256
VMEM_LIMIT = 56 * 1024 * 1024

SEC_CONV = (0, 3 * D_CONV)
SEC_Q = (SEC_CONV[1], SEC_CONV[1] + D_ATTN)
SEC_KVC = (SEC_Q[1], SEC_Q[1] + KV_DIM)
SEC_KVS = (SEC_KVC[1], SEC_KVC[1] + KV_DIM)
SEC_KVW = (SEC_KVS[1], SEC_KVS[1] + KV_DIM)
SEC_Z = (SEC_KVW[1], SEC_KVW[1] + D_ATTN)
SEC_GATE = (SEC_Z[1], SEC_Z[1] + LANES)
D_IN = SEC_Z[1] + N_BRANCH * N_HEADS
D_IN_PAD = SEC_GATE[1]
SECTIONS = (SEC_CONV, SEC_Q, SEC_KVC, SEC_KVS, SEC_KVW, SEC_Z, SEC_GATE)


def _bucket_lower_bounds():
    n = np.arange(0, 4 * MAX_DISTANCE, dtype=np.int64)
    max_exact = NUM_BUCKETS // 2
    nf = np.maximum(n, 1).astype(np.float32)
    large = max_exact + (np.log(nf / np.float32(max_exact)) / np.float32(math.log(MAX_DISTANCE / max_exact))
                         * np.float32(NUM_BUCKETS - max_exact)).astype(np.int32)
    large = np.minimum(large, NUM_BUCKETS - 1)
    bucket = np.where(n < max_exact, n, large)
    return [int(np.argmax(bucket >= b)) for b in range(NUM_BUCKETS)]


BUCKET_LOWER = _bucket_lower_bounds()


def _dot(a, b):
    return jnp.dot(a.astype(BF16), b.astype(BF16), preferred_element_type=F32)


def _dot_nt(a, b):
    return lax.dot_general(a.astype(BF16), b.astype(BF16), (((1,), (1,)), ((), ())),
                           preferred_element_type=F32)


def _dot_split(a, b):
    hi = a.astype(BF16)
    lo = (a - hi.astype(F32)).astype(BF16)
    return (jnp.dot(hi, b, preferred_element_type=F32) + jnp.dot(lo, b, preferred_element_type=F32))


def _sigmoid(x):
    return 1.0 / (1.0 + jnp.exp(-x))


def _silu(x):
    return x * _sigmoid(x)


def _masked_softmax(s, mask):
    s = jnp.where(mask, s, NEG_INF)
    m = jnp.max(s, axis=-1, keepdims=True)
    e = jnp.exp(s - m) * mask.astype(F32)
    l = jnp.sum(e, axis=-1, keepdims=True)
    return e / jnp.maximum(l, TINY)


def _bias_of(dist, rb_ref, h):
    out = jnp.full(dist.shape, rb_ref[0, h], F32)
    for b in range(1, NUM_BUCKETS):
        out = jnp.where(dist >= BUCKET_LOWER[b], rb_ref[b, h], out)
    return out


def _bias_kernel(rb_ref, nb_ref, cb_ref, fb_ref, sba_ref, swa_ref, sbb_ref, sc_ref, *, n_qt, nc_p, past, wbuf, nq_s):
    h = pl.program_id(0)
    c = lax.broadcasted_iota(jnp.int32, (2 * TK, TQ), 0)
    qi = lax.broadcasted_iota(jnp.int32, (2 * TK, TQ), 1)
    nb_ref[0] = _bias_of(qi + TK - c, rb_ref, h)
    nn = lax.broadcasted_iota(jnp.int32, (nc_p, TQ), 0)
    qn = lax.broadcasted_iota(jnp.int32, (nc_p, TQ), 1)
    for t in range(n_qt):
        cb_ref[t, 0] = _bias_of(t * TQ + qn - (nn * CMP_STRIDE + CMP_LEN - 1), rb_ref, h)
    fb_ref[0] = jnp.full(fb_ref.shape[1:], rb_ref[NUM_BUCKETS - 1, h], F32)
    qs = lax.broadcasted_iota(jnp.int32, (nq_s, past), 0)
    ks = lax.broadcasted_iota(jnp.int32, (nq_s, past), 1)
    sba_ref[0] = _bias_of(past + qs - ks, rb_ref, h)
    qs = lax.broadcasted_iota(jnp.int32, (nq_s, wbuf), 0)
    ks = lax.broadcasted_iota(jnp.int32, (nq_s, wbuf), 1)
    swa_ref[0] = _bias_of(wbuf + qs - ks, rb_ref, h)
    qs = lax.broadcasted_iota(jnp.int32, (nq_s, LANES), 0)
    ks = lax.broadcasted_iota(jnp.int32, (nq_s, LANES), 1)
    sbb_ref[0] = _bias_of(qs - ks, rb_ref, h)
    nc_s = sc_ref.shape[2]
    qs = lax.broadcasted_iota(jnp.int32, (nq_s, nc_s), 0)
    ks = lax.broadcasted_iota(jnp.int32, (nq_s, nc_s), 1)
    sc_ref[0] = _bias_of(past + qs - (ks * CMP_STRIDE + CMP_LEN - 1), rb_ref, h)


def _bias_tables(rel_bias, *, seq, past, wbuf, nq_s):
    n_qt = seq // TQ
    nc_p = seq // CMP_STRIDE
    nc_s = past // CMP_STRIDE
    kern = functools.partial(_bias_kernel, n_qt=n_qt, nc_p=nc_p, past=past, wbuf=wbuf, nq_s=nq_s)
    shapes = (
        jax.ShapeDtypeStruct((N_HEADS, 2 * TK, TQ), F32),
        jax.ShapeDtypeStruct((n_qt, N_HEADS, nc_p, TQ), F32),
        jax.ShapeDtypeStruct((N_HEADS, 8, TQ), F32),
        jax.ShapeDtypeStruct((N_HEADS, nq_s, past), F32),
        jax.ShapeDtypeStruct((N_HEADS, nq_s, wbuf), F32),
        jax.ShapeDtypeStruct((N_HEADS, nq_s, LANES), F32),
        jax.ShapeDtypeStruct((N_HEADS, nq_s, nc_s), F32),
    )
    out_specs = (
        pl.BlockSpec((1, 2 * TK, TQ), lambda h: (h, 0, 0)),
        pl.BlockSpec((n_qt, 1, nc_p, TQ), lambda h: (0, h, 0, 0)),
        pl.BlockSpec((1, 8, TQ), lambda h: (h, 0, 0)),
        pl.BlockSpec((1, nq_s, past), lambda h: (h, 0, 0)),
        pl.BlockSpec((1, nq_s, wbuf), lambda h: (h, 0, 0)),
        pl.BlockSpec((1, nq_s, LANES), lambda h: (h, 0, 0)),
        pl.BlockSpec((1, nq_s, nc_s), lambda h: (h, 0, 0)),
    )
    return pl.pallas_call(
        kern, grid=(N_HEADS,),
        in_specs=[pl.BlockSpec(memory_space=pltpu.SMEM)],
        out_specs=out_specs, out_shape=shapes, name="bias_tables",
        compiler_params=pltpu.CompilerParams(dimension_semantics=("arbitrary",)),
    )(rel_bias)


KV_SECTIONS = (SEC_KVC, SEC_KVS, SEC_KVW)
ATTN_KV_SECTIONS = (SEC_KVS, SEC_KVW)


def _in_proj_kernel(x_ref, g_ref, w_ref, *refs, kv_transposed):
    x = x_ref[...]
    u = x * lax.rsqrt(jnp.mean(x * x, axis=-1, keepdims=True) + EPS) * g_ref[...]
    ub = u.astype(BF16)
    if kv_transposed:
        wkv_t_ref, out_refs, attn_refs = refs[0], refs[1:1 + len(SECTIONS)], refs[1 + len(SECTIONS):]
    else:
        out_refs = refs
    for ref, sec in zip(out_refs, SECTIONS):
        if kv_transposed and sec in KV_SECTIONS:
            k = KV_SECTIONS.index(sec)
            kv_t = lax.dot_general(wkv_t_ref[k * KV_DIM:(k + 1) * KV_DIM, :], ub, (((1,), (1,)), ((), ())),
                                   preferred_element_type=F32)
            ref[0] = kv_t
            if sec in ATTN_KV_SECTIONS:
                j = ATTN_KV_SECTIONS.index(sec)
                attn_refs[2 * j][...] = kv_t[0:LANES, :].T.astype(BF16)
                attn_refs[2 * j + 1][0] = kv_t[LANES:2 * LANES, :].astype(BF16)
        else:
            ref[...] = jnp.dot(ub, w_ref[:, sec[0]:sec[1]], preferred_element_type=F32)


def _in_proj(x2d, g, w_pad, wkv_t=None, *, seq=None):
    n = x2d.shape[0]
    tm = min(TM, n)
    kv_transposed = wkv_t is not None
    in_specs = [pl.BlockSpec((tm, D_MODEL), lambda i: (i, 0)),
                pl.BlockSpec((1, D_MODEL), lambda i: (0, 0)),
                pl.BlockSpec((D_MODEL, D_IN_PAD), lambda i: (0, 0))]
    args = [x2d, g, w_pad]
    out_specs, out_shape = [], []
    if kv_transposed:
        in_specs.append(pl.BlockSpec((len(KV_SECTIONS) * KV_DIM, D_MODEL), lambda i: (0, 0)))
        args.append(wkv_t)
        tiles_per_seq = seq // tm
    for sec in SECTIONS:
        w = sec[1] - sec[0]
        if kv_transposed and sec in KV_SECTIONS:
            out_specs.append(pl.BlockSpec((1, w, tm), lambda i: (i // tiles_per_seq, 0, i % tiles_per_seq)))
            out_shape.append(jax.ShapeDtypeStruct((n // seq, w, seq), F32))
        else:
            out_specs.append(pl.BlockSpec((tm, w), lambda i: (i, 0)))
            out_shape.append(jax.ShapeDtypeStruct((n, w), F32))
    if kv_transposed:
        for _ in ATTN_KV_SECTIONS:
            out_specs.append(pl.BlockSpec((tm, LANES), lambda i: (i, 0)))
            out_shape.append(jax.ShapeDtypeStruct((n, LANES), BF16))
            out_specs.append(pl.BlockSpec((1, LANES, tm), lambda i: (i // tiles_per_seq, 0, i % tiles_per_seq)))
            out_shape.append(jax.ShapeDtypeStruct((n // seq, LANES, seq), BF16))
    return pl.pallas_call(
        functools.partial(_in_proj_kernel, kv_transposed=kv_transposed), grid=(n // tm,),
        in_specs=in_specs, out_specs=out_specs, out_shape=out_shape, name="in_proj",
        compiler_params=pltpu.CompilerParams(dimension_semantics=("arbitrary",), vmem_limit_bytes=VMEM_LIMIT),
    )(*args)


HIST_PAD = 32


def _conv_kernel(c3_ref, hist_ref, w_ref, b_ref, lg_ref, lb_ref, out_ref, new_ref, xbuf, *, tt, n_t):
    t = pl.program_id(1)
    off = HIST_PAD - CONV_HIST

    @pl.when(t == 0)
    def _():
        xbuf[0:HIST_PAD, :] = jnp.zeros((HIST_PAD, D_CONV), F32)
        xbuf[off:HIST_PAD, :] = hist_ref[0]

    glu = c3_ref[0, :, 0:D_CONV] * _sigmoid(c3_ref[0, :, D_CONV:2 * D_CONV])
    xbuf[HIST_PAD:HIST_PAD + tt, :] = glu
    acc = jnp.zeros((tt, D_CONV), F32) + b_ref[...]
    for k in range(CONV_WIDTH):
        acc = acc + xbuf[off + k:off + k + tt, :] * w_ref[k:k + 1, :]
    mu = jnp.mean(acc, axis=-1, keepdims=True)
    xc = acc - mu
    var = jnp.mean(xc * xc, axis=-1, keepdims=True)
    y = xc * lax.rsqrt(var + EPS) * lg_ref[...] + lb_ref[...]
    out_ref[0] = _silu(y) * _silu(c3_ref[0, :, 2 * D_CONV:3 * D_CONV])
    tail = xbuf[off + tt:HIST_PAD + tt, :]

    @pl.when(t == n_t - 1)
    def _():
        new_ref[0] = tail

    if n_t > 1:
        @pl.when(t < n_t - 1)
        def _():
            xbuf[off:HIST_PAD, :] = tail


def _conv_module(c3, hist, conv_w, conv_b, ln_g, ln_b):
    bsz, t_len, _ = c3.shape
    tt = min(256, t_len)
    n_t = t_len // tt
    kern = functools.partial(_conv_kernel, tt=tt, n_t=n_t)
    vec = pl.BlockSpec((1, D_CONV), lambda b, t: (0, 0))
    return pl.pallas_call(
        kern, grid=(bsz, n_t),
        in_specs=[pl.BlockSpec((1, tt, 3 * D_CONV), lambda b, t: (b, t, 0)),
                  pl.BlockSpec((1, CONV_HIST, D_CONV), lambda b, t: (b, 0, 0)),
                  pl.BlockSpec((CONV_WIDTH, D_CONV), lambda b, t: (0, 0)),
                  vec, vec, vec],
        out_specs=[pl.BlockSpec((1, tt, D_CONV), lambda b, t: (b, t, 0)),
                   pl.BlockSpec((1, CONV_HIST, D_CONV), lambda b, t: (b, 0, 0))],
        out_shape=[jax.ShapeDtypeStruct((bsz, t_len, D_CONV), F32),
                   jax.ShapeDtypeStruct((bsz, CONV_HIST, D_CONV), F32)],
        scratch_shapes=[pltpu.VMEM((HIST_PAD + tt, D_CONV), F32)],
        name="conv_module",
        compiler_params=pltpu.CompilerParams(dimension_semantics=("arbitrary", "arbitrary")),
    )(c3, hist, conv_w, conv_b, ln_g, ln_b)


HALVES_PER_PAGE = PAGE_SIZE // CMP_STRIDE
N_TG = 2 * KV_HEADS
N_LT = KV_DIM // LANES
CMP_FEAT = CMP_STRIDE * HEAD_DIM


def _compress_body(page_tile, n_pages, pe_ref, w1_ref, w2_ref, out_ref, x_scr, y_scr, h_scr):
    n_half = n_pages * HALVES_PER_PAGE
    for p in range(n_pages):
        for t in range(N_LT):
            x_scr[t, p * PAGE_SIZE:(p + 1) * PAGE_SIZE, :] = page_tile(p, t).T
    low = lax.broadcasted_iota(jnp.int32, (n_half, LANES), 1) < HALF
    for j in range(CMP_STRIDE // 2):
        for t in range(N_LT):
            a = x_scr[t, pl.ds(2 * j, n_half, stride=CMP_STRIDE), :]
            b = x_scr[t, pl.ds(2 * j + 1, n_half, stride=CMP_STRIDE), :]
            y_scr[2 * t, :, j * LANES:(j + 1) * LANES] = jnp.where(low, a, pltpu.roll(b, HALF, 1))
            y_scr[2 * t + 1, :, j * LANES:(j + 1) * LANES] = jnp.where(low, pltpu.roll(a, HALF, 1), b)
    h_scr[n_half:n_half + 8, :] = jnp.zeros((8, 2 * CMP_HIDDEN), F32)
    acts = []
    for tg in range(N_TG):
        t = tg // KV_HEADS
        w1 = w1_ref[t]
        c = jnp.dot(pe_ref[t].astype(BF16), w1, preferred_element_type=F32)
        cvec = c[0:1, 0:CMP_HIDDEN] + c[1:2, CMP_HIDDEN:2 * CMP_HIDDEN]
        h_scr[0:n_half, :] = jnp.dot(y_scr[tg].astype(BF16), w1, preferred_element_type=F32)
        hid = h_scr[0:n_half, 0:CMP_HIDDEN] + h_scr[1:n_half + 1, CMP_HIDDEN:2 * CMP_HIDDEN] + cvec
        acts.append(_silu(hid))
    out = jnp.dot(jnp.concatenate(acts, axis=1).astype(BF16), w2_ref[...], preferred_element_type=F32)
    row = lax.broadcasted_iota(jnp.int32, out.shape, 0)
    out_ref[0] = jnp.where(row < n_half - 1, out, 0.0)


def _compress_paged_kernel(pt_ref, *refs, n_pages):
    del pt_ref
    pages = refs[:n_pages]

    def page_tile(p, t):
        return pages[p][0, 0, t].reshape(LANES, PAGE_SIZE)

    _compress_body(page_tile, n_pages, *refs[n_pages:])


def _compress_seq_kernel(kv_ref, *refs, n_pages):
    def page_tile(p, t):
        return kv_ref[0, t * LANES:(t + 1) * LANES, p * PAGE_SIZE:(p + 1) * PAGE_SIZE]

    _compress_body(page_tile, n_pages, *refs)


def _compress_specs(n_half):
    const = lambda shape: pl.BlockSpec(shape, lambda *a: (0,) * len(shape))
    weight_specs = [const((2, 8, CMP_FEAT)), const((2, CMP_FEAT, 2 * CMP_HIDDEN)), const((N_TG * CMP_HIDDEN, KV_DIM))]
    out_spec = pl.BlockSpec((1, n_half, KV_DIM), lambda b, *a: (b, 0, 0))
    scratch = [pltpu.VMEM((N_LT, n_half * CMP_STRIDE, LANES), F32),
               pltpu.VMEM((N_TG, n_half, CMP_FEAT), F32),
               pltpu.VMEM((n_half + 8, 2 * CMP_HIDDEN), F32)]
    return weight_specs, out_spec, scratch


def _compress_paged(cache_t, layer, table, pe_pad, w1cat, w2big):
    bsz, n_pages = table.shape
    n_half = n_pages * HALVES_PER_PAGE
    page_specs = [pl.BlockSpec((1, 1, 2, KV_HEADS, HEAD_DIM, PAGE_SIZE),
                               lambda b, pt, p=p: (layer, pt[b, p], 0, 0, 0, 0)) for p in range(n_pages)]
    weight_specs, out_spec, scratch = _compress_specs(n_half)
    grid_spec = pltpu.PrefetchScalarGridSpec(
        num_scalar_prefetch=1, grid=(bsz,), in_specs=page_specs + weight_specs, out_specs=out_spec,
        scratch_shapes=scratch)
    return pl.pallas_call(
        functools.partial(_compress_paged_kernel, n_pages=n_pages), grid_spec=grid_spec,
        out_shape=jax.ShapeDtypeStruct((bsz, n_half, KV_DIM), F32), name="compress_paged",
        compiler_params=pltpu.CompilerParams(dimension_semantics=("arbitrary",), vmem_limit_bytes=VMEM_LIMIT),
    )(table, *([cache_t] * n_pages), pe_pad, w1cat, w2big)


def _compress_seq(kv_t, pe_pad, w1cat, w2big):
    bsz, _, seq = kv_t.shape
    n_pages = seq // PAGE_SIZE
    n_half = n_pages * HALVES_PER_PAGE
    weight_specs, out_spec, scratch = _compress_specs(n_half)
    return pl.pallas_call(
        functools.partial(_compress_seq_kernel, n_pages=n_pages), grid=(bsz,),
        in_specs=[pl.BlockSpec((1, KV_DIM, seq), lambda b: (b, 0, 0))] + weight_specs, out_specs=out_spec,
        scratch_shapes=scratch,
        out_shape=jax.ShapeDtypeStruct((bsz, n_half, KV_DIM), F32), name="compress_seq",
        compiler_params=pltpu.CompilerParams(dimension_semantics=("arbitrary",), vmem_limit_bytes=VMEM_LIMIT),
    )(kv_t, pe_pad, w1cat, w2big)


def _head_rows(q, g, low):
    parts = []
    for r in range(GROUP):
        h = GROUP * g + r
        tile = q[:, LANES * (h // 2):LANES * (h // 2 + 1)]
        if (h % 2) != g:
            tile = pltpu.roll(tile, HALF, 1)
        parts.append(jnp.where(low, tile, 0.0) if g == 0 else jnp.where(low, 0.0, tile))
    return parts


def _assemble_heads(o_heads, low):
    tiles = []
    for j in range(N_HEADS // 2):
        a, b = o_heads[2 * j], o_heads[2 * j + 1]
        if (2 * j) // GROUP == 0:
            tiles.append(jnp.where(low, a, pltpu.roll(b, HALF, 1)))
        else:
            tiles.append(jnp.where(low, pltpu.roll(a, HALF, 1), b))
    return jnp.concatenate(tiles, axis=1)


def _select_blocks_t(score_t, allowed_t, n_blk, top_n):
    idx = lax.broadcasted_iota(jnp.int32, score_t.shape, 0)
    cnt = jnp.zeros(score_t.shape, jnp.int32)
    for i in range(n_blk):
        row = score_t[i:i + 1, :]
        ahead = (row > score_t) | ((row == score_t) & (idx > i))
        cnt = cnt + ahead.astype(jnp.int32)
    return (cnt < top_n) & allowed_t


def _flash_step(q_t, k_tile, v_tile, bias, mask, state):
    m, l, acc = state
    s = jnp.dot(k_tile, q_t, preferred_element_type=F32)
    ms, ls, ps, alphas = [], [], [], []
    for r in range(GROUP):
        cols = slice(r * TQ, (r + 1) * TQ)
        s_r = jnp.where(mask, s[:, cols] + bias(r), NEG_INF)
        m_r = jnp.maximum(m[:, cols], jnp.max(s_r, axis=0, keepdims=True))
        alpha = jnp.exp(m[:, cols] - m_r)
        p_r = jnp.where(mask, jnp.exp(s_r - m_r), 0.0)
        ls.append(alpha * l[:, cols] + jnp.sum(p_r, axis=0, keepdims=True))
        ms.append(m_r)
        alphas.append(alpha)
        ps.append(p_r.astype(BF16))
    pv = jnp.dot(v_tile, jnp.concatenate(ps, axis=1), preferred_element_type=F32)
    acc = jnp.concatenate(alphas, axis=1) * acc + pv
    return jnp.concatenate(ms, axis=1), jnp.concatenate(ls, axis=1), acc


def _flash_init():
    return (jnp.full((1, GROUP * TQ), NEG_INF, F32), jnp.zeros((1, GROUP * TQ), F32),
            jnp.zeros((LANES, GROUP * TQ), F32))


def _flash_out(state):
    _, l, acc = state
    return acc * (1.0 / jnp.maximum(l, TINY))


def _attn_prompt_kernel(q_ref, gate_ref, z_ref, kcvc_ref, ks_ref, vs_ref, kw_ref, vw_ref, nb_ref, cb_ref, fb_ref,
                        cover_ref, out_ref, *, n_slc):
    qt = pl.program_id(1)
    nc = kcvc_ref.shape[1]
    ki = lax.broadcasted_iota(jnp.int32, (TK, TQ), 0)
    qi = lax.broadcasted_iota(jnp.int32, (TK, TQ), 1)
    causal = qi >= ki
    q_t = q_ref[...].T
    g_t = _sigmoid(gate_ref[...]).T
    kc = kcvc_ref[0, :, 0:LANES].astype(BF16)
    vc_t = kcvc_ref[0, :, LANES:2 * LANES].T.astype(BF16)
    c_end = lax.broadcasted_iota(jnp.int32, (nc, TQ), 0) * CMP_STRIDE + (CMP_LEN - 1)
    cmask = qt * TQ + lax.broadcasted_iota(jnp.int32, (nc, TQ), 1) >= c_end
    blk_t = lax.broadcasted_iota(jnp.int32, (n_slc, TQ), 0)
    cur_t = jnp.right_shift(qt * TQ + lax.broadcasted_iota(jnp.int32, (n_slc, TQ), 1), SLC_SHIFT)
    allowed_t = blk_t <= cur_t
    forced_t = (blk_t == 0) | (blk_t == cur_t) | (blk_t == cur_t - 1)
    e_row = jnp.right_shift(lax.broadcasted_iota(jnp.int32, (TK, LANES), 0), SLC_SHIFT)
    e_col = lax.broadcasted_iota(jnp.int32, (TK, LANES), 1)
    kt_prev = jnp.maximum(qt - 1, 0)
    kt_far = jnp.maximum(qt - 2, 0)
    zero_half = jnp.zeros((HEAD_DIM, TQ), F32)

    def kv_tile(k_ref, v_ref, kt):
        start = pl.multiple_of(kt * TK, TK)
        return k_ref[pl.ds(start, TK), :], v_ref[0, :, pl.ds(start, TK)]

    pieces = []
    for g in range(KV_HEADS):
        heads = [GROUP * g + r for r in range(GROUP)]
        parts = []
        for h in heads:
            x = q_t[h * HEAD_DIM:(h + 1) * HEAD_DIM, :] * SCALE
            parts.append(jnp.concatenate([x, zero_half] if g == 0 else [zero_half, x], axis=0))
        qg = jnp.concatenate(parts, axis=1).astype(BF16)
        s_c = jnp.dot(kc, qg, preferred_element_type=F32)
        p_parts, p_sum = [], None
        for r, h in enumerate(heads):
            s_r = jnp.where(cmask, s_c[:, r * TQ:(r + 1) * TQ] + cb_ref[0, h], NEG_INF)
            e = jnp.where(cmask, jnp.exp(s_r - jnp.max(s_r, axis=0, keepdims=True)), 0.0)
            p_r = e * (1.0 / jnp.maximum(jnp.sum(e, axis=0, keepdims=True), TINY))
            p_sum = p_r if p_sum is None else p_sum + p_r
            p_parts.append(p_r.astype(BF16))
        o_cmp = jnp.dot(vc_t, jnp.concatenate(p_parts, axis=1), preferred_element_type=F32)
        hi = p_sum.astype(BF16)
        lo = (p_sum - hi.astype(F32)).astype(BF16)
        imp_t = (jnp.dot(cover_ref[...], hi, preferred_element_type=F32)
                 + jnp.dot(cover_ref[...], lo, preferred_element_type=F32))
        score_t = jnp.where(allowed_t, jnp.where(forced_t, FORCED_SCORE, imp_t[0:n_slc]), MASKED_SCORE)
        sel_t = _select_blocks_t(score_t, allowed_t, n_slc, min(TOP_N, n_slc)).astype(F32)
        sel_pad = jnp.concatenate([sel_t, jnp.zeros((LANES - n_slc, TQ), F32)], axis=0).astype(BF16)

        def sel_mask(kt):
            expand = (e_row + kt * (TK // SLC_BLOCK) == e_col).astype(BF16)
            return jnp.dot(expand, sel_pad, preferred_element_type=F32) > 0.5

        far_bias = lambda r: fb_ref[heads[r], 0:1, :]
        prev_bias = lambda r: nb_ref[heads[r], 0:TK, :]
        diag_bias = lambda r: nb_ref[heads[r], TK:2 * TK, :]

        def far_body(kt, state):
            k_t, v_t = kv_tile(ks_ref, vs_ref, kt)
            return _flash_step(qg, k_t, v_t, far_bias, sel_mask(kt), state)

        state = lax.fori_loop(0, jnp.maximum(qt - 1, 0), far_body, _flash_init())
        k_t, v_t = kv_tile(ks_ref, vs_ref, kt_prev)
        state = _flash_step(qg, k_t, v_t, prev_bias, sel_mask(kt_prev) & (qt >= 1), state)
        k_t, v_t = kv_tile(ks_ref, vs_ref, qt)
        state = _flash_step(qg, k_t, v_t, diag_bias, sel_mask(qt) & causal, state)
        o_slc = _flash_out(state)
        state = _flash_init()
        k_t, v_t = kv_tile(kw_ref, vw_ref, kt_far)
        state = _flash_step(qg, k_t, v_t, far_bias, (ki > qi) & (qt >= 2), state)
        k_t, v_t = kv_tile(kw_ref, vw_ref, kt_prev)
        state = _flash_step(qg, k_t, v_t, prev_bias, (ki >= 0) & (qt >= 1), state)
        k_t, v_t = kv_tile(kw_ref, vw_ref, qt)
        state = _flash_step(qg, k_t, v_t, diag_bias, causal, state)
        o_win = _flash_out(state)
        rows = slice(g * HEAD_DIM, (g + 1) * HEAD_DIM)
        for r, h in enumerate(heads):
            cols = slice(r * TQ, (r + 1) * TQ)
            pieces.append(g_t[h:h + 1, :] * o_cmp[rows, cols]
                          + g_t[N_HEADS + h:N_HEADS + h + 1, :] * o_slc[rows, cols]
                          + g_t[2 * N_HEADS + h:2 * N_HEADS + h + 1, :] * o_win[rows, cols])
    out_ref[...] = jnp.concatenate(pieces, axis=0).T * _silu(z_ref[...])


def _attn_prompt(q2d, gate2d, z2d, kcvc, ks, vs_t, kw, vw_t, nb, cb, fb, cover_t, *, bsz, seq):
    assert WINDOW == 2 * TK and seq % TQ == 0
    n_qt = seq // TQ
    nc = seq // CMP_STRIDE
    n_slc = seq // SLC_BLOCK
    kern = functools.partial(_attn_prompt_kernel, n_slc=n_slc)
    tok = lambda w: pl.BlockSpec((TQ, w), lambda b, t: (b * n_qt + t, 0))
    k_spec = pl.BlockSpec((seq, LANES), lambda b, t: (b, 0))
    v_spec = pl.BlockSpec((1, LANES, seq), lambda b, t: (b, 0, 0))
    return pl.pallas_call(
        kern, grid=(bsz, n_qt),
        in_specs=[tok(D_ATTN), tok(LANES), tok(D_ATTN),
                  pl.BlockSpec((1, nc, KV_DIM), lambda b, t: (b, 0, 0)), k_spec, v_spec, k_spec, v_spec,
                  pl.BlockSpec((N_HEADS, 2 * TK, TQ), lambda b, t: (0, 0, 0)),
                  pl.BlockSpec((1, N_HEADS, nc, TQ), lambda b, t: (t, 0, 0, 0)),
                  pl.BlockSpec((N_HEADS, 8, TQ), lambda b, t: (0, 0, 0)),
                  pl.BlockSpec((LANES, nc), lambda b, t: (0, 0))],
        out_specs=tok(D_ATTN),
        out_shape=jax.ShapeDtypeStruct((bsz * seq, D_ATTN), F32),
        name="attn_prompt",
        compiler_params=pltpu.CompilerParams(dimension_semantics=("arbitrary", "arbitrary"),
                                             vmem_limit_bytes=VMEM_LIMIT),
    )(q2d, gate2d, z2d, kcvc, ks, vs_t, kw, vw_t, nb, cb, fb, cover_t)


def _attn_sample_kernel(pt_ref, *refs, n_pages, nq):
    del pt_ref
    pages = refs[:n_pages]
    (win_ref, kvsn_ref, kvwn_ref, q_ref, gate_ref, z_ref, kcvc_ref, sba_ref, swa_ref, sbb_ref, sc_ref,
     cover_ref, expand_ref, out_ref, nwin_ref) = refs[n_pages:]
    past = n_pages * PAGE_SIZE
    wbuf = win_ref.shape[-1]
    nc = kcvc_ref.shape[1]
    rows = KV_HEADS * GROUP * nq
    cur = past // SLC_BLOCK
    n_slc = cur + 1
    low = lax.broadcasted_iota(jnp.int32, (nq, LANES), 1) < HALF
    q = q_ref[0]
    q_left = (jnp.concatenate(_head_rows(q, 0, low) + _head_rows(q, 1, low), axis=0) * SCALE).astype(BF16)
    qi = lax.broadcasted_iota(jnp.int32, (rows, LANES), 0) & (nq - 1)
    ki = lax.broadcasted_iota(jnp.int32, (rows, LANES), 1)
    new_mask = (ki <= qi) & (ki < nq)
    pad_rows = jnp.zeros((LANES - nq, LANES), F32)

    def new_tile(ref, t):
        return jnp.concatenate([ref[0, :, t * LANES:(t + 1) * LANES], pad_rows], axis=0)
    kc = kcvc_ref[0, :, 0:LANES].astype(BF16)
    vc = kcvc_ref[0, :, LANES:2 * LANES].astype(BF16)
    n_idx = lax.broadcasted_iota(jnp.int32, (rows, nc), 1)
    p_c = _masked_softmax(_dot_nt(q_left, kc) + sc_ref[...], n_idx < nc - 1)
    o_cmp = _dot(p_c, vc)
    blk = lax.broadcasted_iota(jnp.int32, (nq, LANES), 1)
    is_blk = blk < n_slc
    forced = (blk == 0) | (blk == cur) | (blk == cur - 1)
    sel_rows = []
    for g in range(KV_HEADS):
        p_sum = p_c[g * GROUP * nq:g * GROUP * nq + nq]
        for r in range(1, GROUP):
            p_sum = p_sum + p_c[(g * GROUP + r) * nq:(g * GROUP + r + 1) * nq]
        imp = _dot_split(p_sum, cover_ref[...])
        score = jnp.where(is_blk, jnp.where(forced, FORCED_SCORE, imp), PAD_SCORE)
        cnt = jnp.zeros((nq, LANES), jnp.int32)
        for i in range(n_slc):
            col = score[:, i:i + 1]
            cnt = cnt + ((col > score) | ((col == score) & (blk > i))).astype(jnp.int32)
        sel_g = ((cnt < min(TOP_N, n_slc)) & is_blk).astype(F32)
        sel_rows += [sel_g] * GROUP
    sel = jnp.concatenate(sel_rows, axis=0)
    k_pages = [pages[p][0, 0, 0].reshape(LANES, PAGE_SIZE).astype(BF16) for p in range(n_pages)]
    v_pages = [pages[p][0, 0, 1].reshape(LANES, PAGE_SIZE).astype(BF16) for p in range(n_pages)]
    k_new = new_tile(kvsn_ref, 0).astype(BF16)
    v_new = new_tile(kvsn_ref, 1).astype(BF16)
    s_a = jnp.concatenate([_dot(q_left, k) for k in k_pages], axis=1) + sba_ref[...]
    mask_a = jnp.dot(sel.astype(BF16), expand_ref[...], preferred_element_type=F32) > 0.5
    s_b = _dot_nt(q_left, k_new) + sbb_ref[...]
    mask_b = new_mask & (sel[:, cur:cur + 1] > 0.5)
    s_a = jnp.where(mask_a, s_a, NEG_INF)
    s_b = jnp.where(mask_b, s_b, NEG_INF)
    m = jnp.maximum(jnp.max(s_a, axis=-1, keepdims=True), jnp.max(s_b, axis=-1, keepdims=True))
    p_a = jnp.exp(s_a - m) * mask_a.astype(F32)
    p_b = jnp.exp(s_b - m) * mask_b.astype(F32)
    l = jnp.sum(p_a, axis=-1, keepdims=True) + jnp.sum(p_b, axis=-1, keepdims=True)
    acc = _dot(p_b, v_new)
    for p in range(n_pages):
        acc = acc + _dot_nt(p_a[:, p * PAGE_SIZE:(p + 1) * PAGE_SIZE], v_pages[p])
    o_slc = acc / jnp.maximum(l, TINY)
    win_t = [win_ref[0, 0, t].reshape(LANES, wbuf) for t in range(2)]
    new_w = [new_tile(kvwn_ref, t) for t in range(2)]
    lane_w = lax.broadcasted_iota(jnp.int32, (LANES, wbuf), 1)
    for t in range(2):
        placed = jnp.concatenate([jnp.zeros((LANES, wbuf - LANES), F32), pltpu.roll(new_w[t].T, LANES - nq, 1)],
                                 axis=1)
        shifted = pltpu.roll(win_t[t], wbuf - nq, 1)
        nwin_ref[0, t] = jnp.where(lane_w < wbuf - nq, shifted, placed).reshape(KV_HEADS, HEAD_DIM, wbuf)
    kw_t, vw_t = win_t[0].astype(BF16), win_t[1].astype(BF16)
    kw_new, vw_new = new_w[0].astype(BF16), new_w[1].astype(BF16)
    jw = lax.broadcasted_iota(jnp.int32, (rows, wbuf), 1)
    qw = lax.broadcasted_iota(jnp.int32, (rows, wbuf), 0) & (nq - 1)
    mask_wa = jw > qw
    s_wa = jnp.where(mask_wa, _dot(q_left, kw_t) + swa_ref[...], NEG_INF)
    s_wb = jnp.where(new_mask, _dot_nt(q_left, kw_new) + sbb_ref[...], NEG_INF)
    m = jnp.maximum(jnp.max(s_wa, axis=-1, keepdims=True), jnp.max(s_wb, axis=-1, keepdims=True))
    p_wa = jnp.exp(s_wa - m) * mask_wa.astype(F32)
    p_wb = jnp.exp(s_wb - m) * new_mask.astype(F32)
    l = jnp.sum(p_wa, axis=-1, keepdims=True) + jnp.sum(p_wb, axis=-1, keepdims=True)
    acc = _dot_nt(p_wa, vw_t) + _dot(p_wb, vw_new)
    o_win = acc / jnp.maximum(l, TINY)
    gsig = _sigmoid(gate_ref[0])
    o_heads = []
    for h in range(N_HEADS):
        rs = slice(h * nq, (h + 1) * nq)
        o_heads.append(gsig[:, h:h + 1] * o_cmp[rs] + gsig[:, N_HEADS + h:N_HEADS + h + 1] * o_slc[rs]
                       + gsig[:, 2 * N_HEADS + h:2 * N_HEADS + h + 1] * o_win[rs])
    out_ref[0] = _assemble_heads(o_heads, low) * _silu(z_ref[0])


def _attn_sample(cache_t, win_t, layer, table, kvs_new, kvw_new, q, gate, z, kcvc, sba, swa, sbb, sc, cover, expand):
    bsz, n_pages = table.shape
    nq = q.shape[1]
    wbuf = win_t.shape[-1]
    nc = kcvc.shape[1]
    past = n_pages * PAGE_SIZE
    rows = N_HEADS * nq
    assert nq <= SLC_BLOCK and nq & (nq - 1) == 0 and past % SLC_BLOCK == 0 and wbuf == WINDOW
    kern = functools.partial(_attn_sample_kernel, n_pages=n_pages, nq=nq)
    page_specs = [pl.BlockSpec((1, 1, 2, KV_HEADS, HEAD_DIM, PAGE_SIZE),
                               lambda b, pt, p=p: (layer, pt[b, p], 0, 0, 0, 0)) for p in range(n_pages)]
    per_b = lambda r, w: pl.BlockSpec((1, r, w), lambda b, pt: (b, 0, 0))
    const = lambda r, w: pl.BlockSpec((r, w), lambda b, pt: (0, 0))
    grid_spec = pltpu.PrefetchScalarGridSpec(
        num_scalar_prefetch=1, grid=(bsz,),
        in_specs=page_specs + [
            pl.BlockSpec((1, 1, 2, KV_HEADS, HEAD_DIM, wbuf), lambda b, pt: (layer, b, 0, 0, 0, 0)),
            per_b(nq, KV_DIM), per_b(nq, KV_DIM), per_b(nq, D_ATTN), per_b(nq, LANES),
            per_b(nq, D_ATTN), per_b(nc, KV_DIM),
            const(rows, past), const(rows, wbuf), const(rows, LANES), const(rows, nc),
            const(nc, LANES), const(LANES, past)],
        out_specs=[per_b(nq, D_ATTN),
                   pl.BlockSpec((1, 2, KV_HEADS, HEAD_DIM, wbuf), lambda b, pt: (b, 0, 0, 0, 0))])
    return pl.pallas_call(
        kern, grid_spec=grid_spec,
        out_shape=[jax.ShapeDtypeStruct((bsz, nq, D_ATTN), F32),
                   jax.ShapeDtypeStruct((bsz, 2, KV_HEADS, HEAD_DIM, wbuf), F32)],
        name="attn_sample",
        compiler_params=pltpu.CompilerParams(dimension_semantics=("arbitrary",), vmem_limit_bytes=VMEM_LIMIT),
    )(table, *([cache_t] * n_pages), win_t, kvs_new, kvw_new, q, gate, z, kcvc, sba, swa, sbb, sc, cover, expand)


def _out_proj_kernel(h_ref, conv_ref, attn_ref, ple_ref, wo_ref, wg_ref, wp_ref, fg_ref, out_ref, *, final):
    h = h_ref[...]
    h = h + jnp.dot(conv_ref[...].astype(BF16), wo_ref[0:D_CONV, :], preferred_element_type=F32)
    h = h + jnp.dot(attn_ref[...].astype(BF16), wo_ref[D_CONV:D_CONV + D_ATTN, :], preferred_element_type=F32)
    gate = _sigmoid(jnp.dot(h.astype(BF16), wg_ref[...], preferred_element_type=F32))
    h = h + gate * jnp.dot(ple_ref[0].astype(BF16), wp_ref[...], preferred_element_type=F32)
    if final:
        h = h * lax.rsqrt(jnp.mean(h * h, axis=-1, keepdims=True) + EPS) * fg_ref[...]
    out_ref[...] = h


def _out_proj(h2d, conv2d, attn2d, ple3d, layer, wo, wg, wp, fg, *, final):
    n = h2d.shape[0]
    tm = min(TM, n)
    ple_dim = ple3d.shape[-1]
    kern = functools.partial(_out_proj_kernel, final=final)
    tok = lambda w: pl.BlockSpec((tm, w), lambda i: (i, 0))
    const = lambda r, w: pl.BlockSpec((r, w), lambda i: (0, 0))
    return pl.pallas_call(
        kern, grid=(n // tm,),
        in_specs=[tok(D_MODEL), tok(D_CONV), tok(D_ATTN), pl.BlockSpec((1, tm, ple_dim), lambda i: (layer, i, 0)),
                  const(D_CONV + D_ATTN, D_MODEL), const(D_MODEL, D_MODEL), const(ple_dim, D_MODEL),
                  const(1, D_MODEL)],
        out_specs=tok(D_MODEL),
        out_shape=jax.ShapeDtypeStruct((n, D_MODEL), F32),
        name="out_proj",
        compiler_params=pltpu.CompilerParams(dimension_semantics=("arbitrary",), vmem_limit_bytes=VMEM_LIMIT),
    )(h2d, conv2d, attn2d, ple3d, wo, wg, wp, fg)


def _cover_matrix(n_cmp_rows, n_cmp, n_slc):
    c_start = np.arange(n_cmp_rows) * CMP_STRIDE
    c_end = c_start + CMP_LEN - 1
    s_start = np.arange(LANES) * SLC_BLOCK
    cover = (c_start[:, None] < s_start[None, :] + SLC_BLOCK) & (c_end[:, None] >= s_start[None, :])
    cover &= (np.arange(n_cmp_rows)[:, None] < n_cmp) & (np.arange(LANES)[None, :] < n_slc)
    return jnp.asarray(cover, dtype=BF16)


def _expand_matrix(past):
    e = np.arange(LANES)[:, None] == (np.arange(past)[None, :] // SLC_BLOCK)
    return jnp.asarray(e, dtype=BF16)


def kernel(x_prompt, x_sample, cache_cmp_kv, cache_slc_kv, page_table, state_win_kv, state_conv, p_prompt, p_sample, norm_g, w_in, conv_w, conv_b, conv_ln_g, conv_ln_b, cmp_pe, cmp_w1, cmp_w2, w_out, w_ple, w_ple_gate, rel_bias, final_norm_g):
    bp, seq, _ = x_prompt.shape
    bs, nq, _ = x_sample.shape
    depth = w_in.shape[0]
    n_pages = page_table.shape[1]
    past = n_pages * PAGE_SIZE
    wbuf = state_win_kv.shape[2]
    n_pool = cache_cmp_kv.shape[1]
    win_p = min(WINDOW, seq)

    nb, cb, fb, sba, swa, sbb, sc = _bias_tables(rel_bias, seq=seq, past=past, wbuf=wbuf, nq_s=nq)
    rows = N_HEADS * nq
    sba, swa, sbb, sc = (a.reshape(rows, a.shape[-1]) for a in (sba, swa, sbb, sc))
    nc_p, nc_s = seq // CMP_STRIDE, past // CMP_STRIDE
    cover_p = _cover_matrix(nc_p, nc_p - 1, seq // SLC_BLOCK)
    cover_s = _cover_matrix(nc_s, nc_s - 1, past // SLC_BLOCK + 1)
    expand_s = _expand_matrix(past)
    conv_zero = jnp.zeros((bp, CONV_HIST, D_CONV), F32)
    fg = final_norm_g.reshape(1, D_MODEL)
    to_t = lambda a: jnp.transpose(a, (0, 1, 3, 4, 5, 2))
    from_t = lambda a: jnp.transpose(a, (0, 1, 5, 2, 3, 4))
    cmp_t, slc_t, win_t = to_t(cache_cmp_kv), to_t(cache_slc_kv), to_t(state_win_kv)
    ple_p = p_prompt.reshape(depth, bp * seq, -1)
    ple_s = p_sample.reshape(depth, bs * nq, -1)

    hp = x_prompt.reshape(bp * seq, D_MODEL)
    hs = x_sample.reshape(bs * nq, D_MODEL)
    outs = [[] for _ in range(8)]
    for i in range(depth):
        w_pad = jnp.pad(w_in[i], ((0, 0), (0, D_IN_PAD - D_IN))).astype(BF16)
        wkv_t = w_in[i][:, SEC_KVC[0]:SEC_KVW[1]].T.astype(BF16)
        g = norm_g[i].reshape(1, D_MODEL)
        w1 = cmp_w1[i].reshape(2, 2, CMP_FEAT, CMP_HIDDEN)
        w1cat = jnp.concatenate([w1[:, 0], w1[:, 1]], axis=-1).astype(BF16)
        pe_pad = jnp.pad(cmp_pe[i].reshape(2, 2, CMP_FEAT), ((0, 0), (0, 6), (0, 0)))
        w2big = jnp.zeros((N_TG * CMP_HIDDEN, KV_DIM), F32)
        for tg in range(N_TG):
            w2big = w2big.at[tg * CMP_HIDDEN:(tg + 1) * CMP_HIDDEN, tg * HEAD_DIM:(tg + 1) * HEAD_DIM].set(
                cmp_w2[i, tg // KV_HEADS])
        w2big = w2big.astype(BF16)
        wo, wg, wp = w_out[i].astype(BF16), w_ple_gate[i].astype(BF16), w_ple[i].astype(BF16)
        cw, cbias = conv_w[i], conv_b[i].reshape(1, D_CONV)
        lg, lb = conv_ln_g[i].reshape(1, D_CONV), conv_ln_b[i].reshape(1, D_CONV)
        final = i == depth - 1

        c3, q, kvc_t, kvs_t, kvw_t, z, gate, ks, vs_t, kw, vw_t = _in_proj(hp, g, w_pad, wkv_t, seq=seq)
        conv_out, new_conv = _conv_module(c3.reshape(bp, seq, 3 * D_CONV), conv_zero, cw, cbias, lg, lb)
        kcvc = _compress_seq(kvc_t, pe_pad, w1cat, w2big)
        attn = _attn_prompt(q, gate, z, kcvc, ks, vs_t, kw, vw_t, nb, cb, fb, cover_p.T, bsz=bp, seq=seq)
        hp = _out_proj(hp, conv_out.reshape(bp * seq, D_CONV), attn, ple_p, i, wo, wg, wp, fg, final=final)
        six_d = lambda a: a.reshape(bp, 2, KV_HEADS, HEAD_DIM, a.shape[-1])
        outs[0].append(six_d(kvc_t))
        outs[2].append(six_d(kvs_t))
        outs[4].append(six_d(kvw_t[:, :, seq - win_p:]))
        outs[6].append(new_conv)

        c3, q, kvc, kvs, kvw, z, gate = _in_proj(hs, g, w_pad)
        conv_out, new_conv = _conv_module(c3.reshape(bs, nq, 3 * D_CONV), state_conv[i], cw, cbias, lg, lb)
        kcvc = _compress_paged(cmp_t, i, page_table, pe_pad, w1cat, w2big)
        attn, new_win = _attn_sample(
            slc_t, win_t, i, page_table, kvs.reshape(bs, nq, KV_DIM), kvw.reshape(bs, nq, KV_DIM),
            q.reshape(bs, nq, D_ATTN), gate.reshape(bs, nq, LANES), z.reshape(bs, nq, D_ATTN), kcvc,
            sba, swa, sbb, sc, cover_s, expand_s)
        hs = _out_proj(hs, conv_out.reshape(bs * nq, D_CONV), attn.reshape(bs * nq, D_ATTN), ple_s, i,
                       wo, wg, wp, fg, final=final)
        outs[1].append(kvc.reshape(bs, nq, 2, KV_HEADS, HEAD_DIM))
        outs[3].append(kvs.reshape(bs, nq, 2, KV_HEADS, HEAD_DIM))
        outs[5].append(new_win)
        outs[7].append(new_conv)

    stacked = [jnp.stack(o) for o in outs]
    for k in (0, 2, 4, 5):
        stacked[k] = from_t(stacked[k])
    return (hp.reshape(bp, seq, D_MODEL), hs.reshape(bs, nq, D_MODEL)) + tuple(stacked)
```

```python
import functools
import math

import numpy as np
import jax
import jax.numpy as jnp
from jax import lax
from jax.experimental import pallas as pl
from jax.experimental.pallas import tpu as pltpu

F32 = jnp.float32
BF16 = jnp.bfloat16

D_MODEL = 1024
D_CONV = 512
CONV_WIDTH = 31
CONV_HIST = CONV_WIDTH - 1
HEAD_DIM = 64
N_HEADS = 8
KV_HEADS = 2
GROUP = N_HEADS // KV_HEADS
D_ATTN = N_HEADS * HEAD_DIM
KV_DIM = 2 * KV_HEADS * HEAD_DIM
N_BRANCH = 3
CMP_STRIDE = 16
CMP_LEN = 2 * CMP_STRIDE
CMP_HIDDEN = 128
SLC_BLOCK = 64
SLC_SHIFT = 6
TOP_N = 8
WINDOW = 512
NUM_BUCKETS = 32
MAX_DISTANCE = 128
PAGE_SIZE = 128
EPS = 1e-6
NEG_INF = -1e30
FORCED_SCORE = 1e4
MASKED_SCORE = -1e4
PAD_SCORE = -3e4
TINY = 1e-30
SCALE = HEAD_DIM ** -0.5

LANES = 128
HALF = LANES // 2
TQ = 256
TK = 256
TM = 256
VMEM_LIMIT = 56 * 1024 * 1024

SEC_CONV = (0, 3 * D_CONV)
SEC_Q = (SEC_CONV[1], SEC_CONV[1] + D_ATTN)
SEC_KVC = (SEC_Q[1], SEC_Q[1] + KV_DIM)
SEC_KVS = (SEC_KVC[1], SEC_KVC[1] + KV_DIM)
SEC_KVW = (SEC_KVS[1], SEC_KVS[1] + KV_DIM)
SEC_Z = (SEC_KVW[1], SEC_KVW[1] + D_ATTN)
SEC_GATE = (SEC_Z[1], SEC_Z[1] + LANES)
D_IN = SEC_Z[1] + N_BRANCH * N_HEADS
D_IN_PAD = SEC_GATE[1]
SECTIONS = (SEC_CONV, SEC_Q, SEC_KVC, SEC_KVS, SEC_KVW, SEC_Z, SEC_GATE)


def _bucket_lower_bounds():
    n = np.arange(0, 4 * MAX_DISTANCE, dtype=np.int64)
    max_exact = NUM_BUCKETS // 2
    nf = np.maximum(n, 1).astype(np.float32)
    large = max_exact + (np.log(nf / np.float32(max_exact)) / np.float32(math.log(MAX_DISTANCE / max_exact))
                         * np.float32(NUM_BUCKETS - max_exact)).astype(np.int32)
    large = np.minimum(large, NUM_BUCKETS - 1)
    bucket = np.where(n < max_exact, n, large)
    return [int(np.argmax(bucket >= b)) for b in range(NUM_BUCKETS)]


BUCKET_LOWER = _bucket_lower_bounds()


def _dot(a, b):
    return jnp.dot(a.astype(BF16), b.astype(BF16), preferred_element_type=F32)


def _dot_nt(a, b):
    return lax.dot_general(a.astype(BF16), b.astype(BF16), (((1,), (1,)), ((), ())),
                           preferred_element_type=F32)


def _dot_split(a, b):
    hi = a.astype(BF16)
    lo = (a - hi.astype(F32)).astype(BF16)
    return (jnp.dot(hi, b, preferred_element_type=F32) + jnp.dot(lo, b, preferred_element_type=F32))


def _sigmoid(x):
    return 1.0 / (1.0 + jnp.exp(-x))


def _silu(x):
    return x * _sigmoid(x)


def _masked_softmax(s, mask):
    s = jnp.where(mask, s, NEG_INF)
    m = jnp.max(s, axis=-1, keepdims=True)
    e = jnp.exp(s - m) * mask.astype(F32)
    l = jnp.sum(e, axis=-1, keepdims=True)
    return e / jnp.maximum(l, TINY)


def _bias_of(dist, rb_ref, h):
    out = jnp.full(dist.shape, rb_ref[0, h], F32)
    for b in range(1, NUM_BUCKETS):
        out = jnp.where(dist >= BUCKET_LOWER[b], rb_ref[b, h], out)
    return out


def _bias_kernel(rb_ref, nb_ref, cb_ref, fb_ref, sba_ref, swa_ref, sbb_ref, sc_ref, *, n_qt, nc_p, past, wbuf, nq_s):
    h = pl.program_id(0)
    c = lax.broadcasted_iota(jnp.int32, (2 * TK, TQ), 0)
    qi = lax.broadcasted_iota(jnp.int32, (2 * TK, TQ), 1)
    nb_ref[0] = _bias_of(qi + TK - c, rb_ref, h)
    nn = lax.broadcasted_iota(jnp.int32, (nc_p, TQ), 0)
    qn = lax.broadcasted_iota(jnp.int32, (nc_p, TQ), 1)
    for t in range(n_qt):
        cb_ref[t, 0] = _bias_of(t * TQ + qn - (nn * CMP_STRIDE + CMP_LEN - 1), rb_ref, h)
    fb_ref[0] = jnp.full(fb_ref.shape[1:], rb_ref[NUM_BUCKETS - 1, h], F32)
    qs = lax.broadcasted_iota(jnp.int32, (nq_s, past), 0)
    ks = lax.broadcasted_iota(jnp.int32, (nq_s, past), 1)
    sba_ref[0] = _bias_of(past + qs - ks, rb_ref, h)
    qs = lax.broadcasted_iota(jnp.int32, (nq_s, wbuf), 0)
    ks = lax.broadcasted_iota(jnp.int32, (nq_s, wbuf), 1)
    swa_ref[0] = _bias_of(wbuf + qs - ks, rb_ref, h)
    qs = lax.broadcasted_iota(jnp.int32, (nq_s, LANES), 0)
    ks = lax.broadcasted_iota(jnp.int32, (nq_s, LANES), 1)
    sbb_ref[0] = _bias_of(qs - ks, rb_ref, h)
    nc_s = sc_ref.shape[2]
    qs = lax.broadcasted_iota(jnp.int32, (nq_s, nc_s), 0)
    ks = lax.broadcasted_iota(jnp.int32, (nq_s, nc_s), 1)
    sc_ref[0] = _bias_of(past + qs - (ks * CMP_STRIDE + CMP_LEN - 1), rb_ref, h)


def _bias_tables(rel_bias, *, seq, past, wbuf, nq_s):
    n_qt = seq // TQ
    nc_p = seq // CMP_STRIDE
    nc_s = past // CMP_STRIDE
    kern = functools.partial(_bias_kernel, n_qt=n_qt, nc_p=nc_p, past=past, wbuf=wbuf, nq_s=nq_s)
    shapes = (
        jax.ShapeDtypeStruct((N_HEADS, 2 * TK, TQ), F32),
        jax.ShapeDtypeStruct((n_qt, N_HEADS, nc_p, TQ), F32),
        jax.ShapeDtypeStruct((N_HEADS, 8, TQ), F32),
        jax.ShapeDtypeStruct((N_HEADS, nq_s, past), F32),
        jax.ShapeDtypeStruct((N_HEADS, nq_s, wbuf), F32),
        jax.ShapeDtypeStruct((N_HEADS, nq_s, LANES), F32),
        jax.ShapeDtypeStruct((N_HEADS, nq_s, nc_s), F32),
    )
    out_specs = (
        pl.BlockSpec((1, 2 * TK, TQ), lambda h: (h, 0, 0)),
        pl.BlockSpec((n_qt, 1, nc_p, TQ), lambda h: (0, h, 0, 0)),
        pl.BlockSpec((1, 8, TQ), lambda h: (h, 0, 0)),
        pl.BlockSpec((1, nq_s, past), lambda h: (h, 0, 0)),
        pl.BlockSpec((1, nq_s, wbuf), lambda h: (h, 0, 0)),
        pl.BlockSpec((1, nq_s, LANES), lambda h: (h, 0, 0)),
        pl.BlockSpec((1, nq_s, nc_s), lambda h: (h, 0, 0)),
    )
    return pl.pallas_call(
        kern, grid=(N_HEADS,),
        in_specs=[pl.BlockSpec(memory_space=pltpu.SMEM)],
        out_specs=out_specs, out_shape=shapes, name="bias_tables",
        compiler_params=pltpu.CompilerParams(dimension_semantics=("arbitrary",)),
    )(rel_bias)


KV_SECTIONS = (SEC_KVC, SEC_KVS, SEC_KVW)
ATTN_KV_SECTIONS = (SEC_KVS, SEC_KVW)


def _in_proj_kernel(x_ref, g_ref, w_ref, *refs, kv_transposed):
    x = x_ref[...]
    u = x * lax.rsqrt(jnp.mean(x * x, axis=-1, keepdims=True) + EPS) * g_ref[...]
    ub = u.astype(BF16)
    if kv_transposed:
        wkv_t_ref, out_refs, attn_refs = refs[0], refs[1:1 + len(SECTIONS)], refs[1 + len(SECTIONS):]
    else:
        out_refs = refs
    for ref, sec in zip(out_refs, SECTIONS):
        if kv_transposed and sec in KV_SECTIONS:
            k = KV_SECTIONS.index(sec)
            kv_t = lax.dot_general(wkv_t_ref[k * KV_DIM:(k + 1) * KV_DIM, :], ub, (((1,), (1,)), ((), ())),
                                   preferred_element_type=F32)
            ref[0] = kv_t
            if sec in ATTN_KV_SECTIONS:
                j = ATTN_KV_SECTIONS.index(sec)
                attn_refs[2 * j][...] = kv_t[0:LANES, :].T.astype(BF16)
                attn_refs[2 * j + 1][0] = kv_t[LANES:2 * LANES, :].astype(BF16)
        else:
            ref[...] = jnp.dot(ub, w_ref[:, sec[0]:sec[1]], preferred_element_type=F32)


def _in_proj(x2d, g, w_pad, wkv_t=None, *, seq=None):
    n = x2d.shape[0]
    tm = min(TM, n)
    kv_transposed = wkv_t is not None
    in_specs = [pl.BlockSpec((tm, D_MODEL), lambda i: (i, 0)),
                pl.BlockSpec((1, D_MODEL), lambda i: (0, 0)),
                pl.BlockSpec((D_MODEL, D_IN_PAD), lambda i: (0, 0))]
    args = [x2d, g, w_pad]
    out_specs, out_shape = [], []
    if kv_transposed:
        in_specs.append(pl.BlockSpec((len(KV_SECTIONS) * KV_DIM, D_MODEL), lambda i: (0, 0)))
        args.append(wkv_t)
        tiles_per_seq = seq // tm
    for sec in SECTIONS:
        w = sec[1] - sec[0]
        if kv_transposed and sec in KV_SECTIONS:
            out_specs.append(pl.BlockSpec((1, w, tm), lambda i: (i // tiles_per_seq, 0, i % tiles_per_seq)))
            out_shape.append(jax.ShapeDtypeStruct((n // seq, w, seq), F32))
        else:
            out_specs.append(pl.BlockSpec((tm, w), lambda i: (i, 0)))
            out_shape.append(jax.ShapeDtypeStruct((n, w), F32))
    if kv_transposed:
        for _ in ATTN_KV_SECTIONS:
            out_specs.append(pl.BlockSpec((tm, LANES), lambda i: (i, 0)))
            out_shape.append(jax.ShapeDtypeStruct((n, LANES), BF16))
            out_specs.append(pl.BlockSpec((1, LANES, tm), lambda i: (i // tiles_per_seq, 0, i % tiles_per_seq)))
            out_shape.append(jax.ShapeDtypeStruct((n // seq, LANES, seq), BF16))
    return pl.pallas_call(
        functools.partial(_in_proj_kernel, kv_transposed=kv_transposed), grid=(n // tm,),
        in_specs=in_specs, out_specs=out_specs, out_shape=out_shape, name="in_proj",
        compiler_params=pltpu.CompilerParams(dimension_semantics=("arbitrary",), vmem_limit_bytes=VMEM_LIMIT),
    )(*args)


HIST_PAD = 32


def _conv_kernel(c3_ref, hist_ref, w_ref, b_ref, lg_ref, lb_ref, out_ref, new_ref, xbuf, sbuf, *, tt, n_t, n_seq):
    t = pl.program_id(1)
    off = HIST_PAD - CONV_HIST

    def load_history(i):
        xbuf[0:HIST_PAD, :] = jnp.zeros((HIST_PAD, D_CONV), F32)
        xbuf[off:HIST_PAD, :] = hist_ref[i]

    def one_sequence(i):
        if n_t == 1:
            load_history(i)
        else:
            pl.when(t == 0)(lambda: load_history(i))
        glu = c3_ref[i, :, 0:D_CONV] * _sigmoid(c3_ref[i, :, D_CONV:2 * D_CONV])
        xbuf[HIST_PAD:HIST_PAD + tt, :] = glu
        acc = jnp.zeros((tt, D_CONV), F32) + b_ref[...]
        for b in range(8):
            a_max = (CONV_WIDTH - 1 - b) // 8
            sbuf[0:tt + 8 * a_max, :] = xbuf[off + b:off + b + tt + 8 * a_max, :]
            for a in range(a_max + 1):
                k = 8 * a + b
                acc = acc + sbuf[8 * a:8 * a + tt, :] * w_ref[k:k + 1, :]
        mu = jnp.mean(acc, axis=-1, keepdims=True)
        xc = acc - mu
        var = jnp.mean(xc * xc, axis=-1, keepdims=True)
        y = xc * lax.rsqrt(var + EPS) * lg_ref[...] + lb_ref[...]
        out_ref[i] = _silu(y) * _silu(c3_ref[i, :, 2 * D_CONV:3 * D_CONV])
        tail = xbuf[off + tt:HIST_PAD + tt, :]
        if n_t == 1:
            new_ref[i] = tail
        else:
            @pl.when(t == n_t - 1)
            def _():
                new_ref[i] = tail

            @pl.when(t < n_t - 1)
            def _():
                xbuf[off:HIST_PAD, :] = tail

    if n_seq == 1:
        one_sequence(0)
    else:
        def body(i, carry):
            one_sequence(i)
            return carry

        lax.fori_loop(0, n_seq, body, 0)


def _conv_module(c3, hist, conv_w, conv_b, ln_g, ln_b):
    bsz, t_len, _ = c3.shape
    tt = min(256, t_len)
    n_t = t_len // tt
    n_seq = 1 if n_t > 1 else math.gcd(bsz, 16)
    kern = functools.partial(_conv_kernel, tt=tt, n_t=n_t, n_seq=n_seq)
    vec = pl.BlockSpec((1, D_CONV), lambda b, t: (0, 0))
    return pl.pallas_call(
        kern, grid=(bsz // n_seq, n_t),
        in_specs=[pl.BlockSpec((n_seq, tt, 3 * D_CONV), lambda b, t: (b, t, 0)),
                  pl.BlockSpec((n_seq, CONV_HIST, D_CONV), lambda b, t: (b, 0, 0)),
                  pl.BlockSpec((CONV_WIDTH, D_CONV), lambda b, t: (0, 0)),
                  vec, vec, vec],
        out_specs=[pl.BlockSpec((n_seq, tt, D_CONV), lambda b, t: (b, t, 0)),
                   pl.BlockSpec((n_seq, CONV_HIST, D_CONV), lambda b, t: (b, 0, 0))],
        out_shape=[jax.ShapeDtypeStruct((bsz, t_len, D_CONV), F32),
                   jax.ShapeDtypeStruct((bsz, CONV_HIST, D_CONV), F32)],
        scratch_shapes=[pltpu.VMEM((HIST_PAD + tt, D_CONV), F32), pltpu.VMEM((HIST_PAD + tt, D_CONV), F32)],
        name="conv_module",
        compiler_params=pltpu.CompilerParams(dimension_semantics=("arbitrary", "arbitrary")),
    )(c3, hist, conv_w, conv_b, ln_g, ln_b)


HALVES_PER_PAGE = PAGE_SIZE // CMP_STRIDE
N_TG = 2 * KV_HEADS
N_LT = KV_DIM // LANES
CMP_FEAT = CMP_STRIDE * HEAD_DIM
X_PITCH = 24
SEQS_PER_STEP = 4


def _compress_body(page_tile, n_pages, pe_ref, w1_ref, w2_ref, out_ref, x_scr, y_scr, h_scr):
    n_half = n_pages * HALVES_PER_PAGE
    n_seq = out_ref.shape[0]
    low = lax.broadcasted_iota(jnp.int32, (n_half, LANES), 1) < HALF
    for i in range(n_seq):
        for p in range(n_pages):
            for t in range(N_LT):
                tile = page_tile(i, p, t).T
                for n in range(HALVES_PER_PAGE):
                    dst = (p * HALVES_PER_PAGE + n) * X_PITCH
                    x_scr[t, dst:dst + CMP_STRIDE, :] = tile[n * CMP_STRIDE:(n + 1) * CMP_STRIDE, :]
        rows = slice(i * n_half, (i + 1) * n_half)
        for j in range(CMP_STRIDE // 2):
            for t in range(N_LT):
                a = x_scr[t, pl.ds(2 * j, n_half, stride=X_PITCH), :]
                b = x_scr[t, pl.ds(2 * j + 1, n_half, stride=X_PITCH), :]
                y_scr[2 * t, rows, j * LANES:(j + 1) * LANES] = jnp.where(low, a, pltpu.roll(b, HALF, 1))
                y_scr[2 * t + 1, rows, j * LANES:(j + 1) * LANES] = jnp.where(low, pltpu.roll(a, HALF, 1), b)
    n_rows = n_seq * n_half
    h_scr[n_rows:n_rows + 8, :] = jnp.zeros((8, 2 * CMP_HIDDEN), F32)
    acts = []
    for tg in range(N_TG):
        t = tg // KV_HEADS
        w1 = w1_ref[t]
        c = jnp.dot(pe_ref[t].astype(BF16), w1, preferred_element_type=F32)
        cvec = c[0:1, 0:CMP_HIDDEN] + c[1:2, CMP_HIDDEN:2 * CMP_HIDDEN]
        h_scr[0:n_rows, :] = jnp.dot(y_scr[tg].astype(BF16), w1, preferred_element_type=F32)
        hid = h_scr[0:n_rows, 0:CMP_HIDDEN] + h_scr[1:n_rows + 1, CMP_HIDDEN:2 * CMP_HIDDEN] + cvec
        acts.append(_silu(hid))
    out = jnp.dot(jnp.concatenate(acts, axis=1).astype(BF16), w2_ref[...], preferred_element_type=F32)
    row = lax.broadcasted_iota(jnp.int32, out.shape, 0) & (n_half - 1)
    out_ref[...] = jnp.where(row < n_half - 1, out, 0.0).reshape(n_seq, n_half, KV_DIM)


def _compress_paged_kernel(pt_ref, *refs, n_pages, n_seq):
    del pt_ref
    pages = refs[:n_seq * n_pages]

    def page_tile(i, p, t):
        return pages[i * n_pages + p][0, 0, t].reshape(LANES, PAGE_SIZE)

    _compress_body(page_tile, n_pages, *refs[n_seq * n_pages:])


def _compress_seq_kernel(kv_ref, *refs, n_pages):
    def page_tile(i, p, t):
        return kv_ref[i, t * LANES:(t + 1) * LANES, p * PAGE_SIZE:(p + 1) * PAGE_SIZE]

    _compress_body(page_tile, n_pages, *refs)


def _compress_specs(n_half, n_seq):
    assert n_half & (n_half - 1) == 0
    const = lambda shape: pl.BlockSpec(shape, lambda *a: (0,) * len(shape))
    weight_specs = [const((2, 8, CMP_FEAT)), const((2, CMP_FEAT, 2 * CMP_HIDDEN)), const((N_TG * CMP_HIDDEN, KV_DIM))]
    out_spec = pl.BlockSpec((n_seq, n_half, KV_DIM), lambda b, *a: (b, 0, 0))
    scratch = [pltpu.VMEM((N_LT, n_half * X_PITCH, LANES), F32),
               pltpu.VMEM((N_TG, n_seq * n_half, CMP_FEAT), F32),
               pltpu.VMEM((n_seq * n_half + 8, 2 * CMP_HIDDEN), F32)]
    return weight_specs, out_spec, scratch


def _compress_paged(cache_t, layer, table, pe_pad, w1cat, w2big):
    bsz, n_pages = table.shape
    n_half = n_pages * HALVES_PER_PAGE
    n_seq = math.gcd(bsz, SEQS_PER_STEP)
    page_specs = [pl.BlockSpec((1, 1, 2, KV_HEADS, HEAD_DIM, PAGE_SIZE),
                               lambda b, pt, i=i, p=p: (layer, pt[b * n_seq + i, p], 0, 0, 0, 0))
                  for i in range(n_seq) for p in range(n_pages)]
    weight_specs, out_spec, scratch = _compress_specs(n_half, n_seq)
    grid_spec = pltpu.PrefetchScalarGridSpec(
        num_scalar_prefetch=1, grid=(bsz // n_seq,), in_specs=page_specs + weight_specs, out_specs=out_spec,
        scratch_shapes=scratch)
    return pl.pallas_call(
        functools.partial(_compress_paged_kernel, n_pages=n_pages, n_seq=n_seq), grid_spec=grid_spec,
        out_shape=jax.ShapeDtypeStruct((bsz, n_half, KV_DIM), F32), name="compress_paged",
        compiler_params=pltpu.CompilerParams(dimension_semantics=("arbitrary",), vmem_limit_bytes=VMEM_LIMIT),
    )(table, *([cache_t] * (n_seq * n_pages)), pe_pad, w1cat, w2big)


def _compress_seq(kv_t, pe_pad, w1cat, w2big):
    bsz, _, seq = kv_t.shape
    n_pages = seq // PAGE_SIZE
    n_half = n_pages * HALVES_PER_PAGE
    weight_specs, out_spec, scratch = _compress_specs(n_half, 1)
    return pl.pallas_call(
        functools.partial(_compress_seq_kernel, n_pages=n_pages), grid=(bsz,),
        in_specs=[pl.BlockSpec((1, KV_DIM, seq), lambda b: (b, 0, 0))] + weight_specs, out_specs=out_spec,
        scratch_shapes=scratch,
        out_shape=jax.ShapeDtypeStruct((bsz, n_half, KV_DIM), F32), name="compress_seq",
        compiler_params=pltpu.CompilerParams(dimension_semantics=("arbitrary",), vmem_limit_bytes=VMEM_LIMIT),
    )(kv_t, pe_pad, w1cat, w2big)


def _head_rows(q, g, low):
    parts = []
    for r in range(GROUP):
        h = GROUP * g + r
        tile = q[:, LANES * (h // 2):LANES * (h // 2 + 1)]
        if (h % 2) != g:
            tile = pltpu.roll(tile, HALF, 1)
        parts.append(jnp.where(low, tile, 0.0) if g == 0 else jnp.where(low, 0.0, tile))
    return parts


def _assemble_heads(o_heads, low):
    tiles = []
    for j in range(N_HEADS // 2):
        a, b = o_heads[2 * j], o_heads[2 * j + 1]
        if (2 * j) // GROUP == 0:
            tiles.append(jnp.where(low, a, pltpu.roll(b, HALF, 1)))
        else:
            tiles.append(jnp.where(low, pltpu.roll(a, HALF, 1), b))
    return jnp.concatenate(tiles, axis=1)


def _select_blocks_t(score_t, allowed_t, n_blk, top_n):
    idx = lax.broadcasted_iota(jnp.int32, score_t.shape, 0)
    cnt = jnp.zeros(score_t.shape, jnp.int32)
    for i in range(n_blk):
        row = score_t[i:i + 1, :]
        ahead = (row > score_t) | ((row == score_t) & (idx > i))
        cnt = cnt + ahead.astype(jnp.int32)
    return (cnt < top_n) & allowed_t


def _flash_step(q_t, k_tile, v_tile, bias, mask, state):
    m, l, acc = state
    s = jnp.dot(k_tile, q_t, preferred_element_type=F32)
    ms, ls, ps, alphas = [], [], [], []
    for r in range(GROUP):
        cols = slice(r * TQ, (r + 1) * TQ)
        s_r = jnp.where(mask, s[:, cols] + bias(r), NEG_INF)
        m_r = jnp.maximum(m[:, cols], jnp.max(s_r, axis=0, keepdims=True))
        alpha = jnp.exp(m[:, cols] - m_r)
        p_r = jnp.where(mask, jnp.exp(s_r - m_r), 0.0)
        ls.append(alpha * l[:, cols] + jnp.sum(p_r, axis=0, keepdims=True))
        ms.append(m_r)
        alphas.append(alpha)
        ps.append(p_r.astype(BF16))
    pv = jnp.dot(v_tile, jnp.concatenate(ps, axis=1), preferred_element_type=F32)
    acc = jnp.concatenate(alphas, axis=1) * acc + pv
    return jnp.concatenate(ms, axis=1), jnp.concatenate(ls, axis=1), acc


def _flash_init():
    return (jnp.full((1, GROUP * TQ), NEG_INF, F32), jnp.zeros((1, GROUP * TQ), F32),
            jnp.zeros((LANES, GROUP * TQ), F32))


def _flash_out(state):
    _, l, acc = state
    return acc * (1.0 / jnp.maximum(l, TINY))


def _attn_prompt_kernel(q_ref, gate_ref, z_ref, kcvc_ref, ks_ref, vs_ref, kw_ref, vw_ref, nb_ref, cb_ref, fb_ref,
                        cover_ref, out_ref, *, n_slc):
    qt = pl.program_id(1)
    nc = kcvc_ref.shape[1]
    ki = lax.broadcasted_iota(jnp.int32, (TK, TQ), 0)
    qi = lax.broadcasted_iota(jnp.int32, (TK, TQ), 1)
    causal = qi >= ki
    q_t = q_ref[...].T
    g_t = _sigmoid(gate_ref[...]).T
    kc = kcvc_ref[0, :, 0:LANES].astype(BF16)
    vc_t = kcvc_ref[0, :, LANES:2 * LANES].T.astype(BF16)
    c_end = lax.broadcasted_iota(jnp.int32, (nc, TQ), 0) * CMP_STRIDE + (CMP_LEN - 1)
    cmask = qt * TQ + lax.broadcasted_iota(jnp.int32, (nc, TQ), 1) >= c_end
    blk_t = lax.broadcasted_iota(jnp.int32, (n_slc, TQ), 0)
    cur_t = jnp.right_shift(qt * TQ + lax.broadcasted_iota(jnp.int32, (n_slc, TQ), 1), SLC_SHIFT)
    allowed_t = blk_t <= cur_t
    forced_t = (blk_t == 0) | (blk_t == cur_t) | (blk_t == cur_t - 1)
    e_row = jnp.right_shift(lax.broadcasted_iota(jnp.int32, (TK, LANES), 0), SLC_SHIFT)
    e_col = lax.broadcasted_iota(jnp.int32, (TK, LANES), 1)
    kt_prev = jnp.maximum(qt - 1, 0)
    kt_far = jnp.maximum(qt - 2, 0)
    zero_half = jnp.zeros((HEAD_DIM, TQ), F32)

    def kv_tile(k_ref, v_ref, kt):
        start = pl.multiple_of(kt * TK, TK)
        return k_ref[pl.ds(start, TK), :], v_ref[0, :, pl.ds(start, TK)]

    pieces = []
    for g in range(KV_HEADS):
        heads = [GROUP * g + r for r in range(GROUP)]
        parts = []
        for h in heads:
            x = q_t[h * HEAD_DIM:(h + 1) * HEAD_DIM, :] * SCALE
            parts.append(jnp.concatenate([x, zero_half] if g == 0 else [zero_half, x], axis=0))
        qg = jnp.concatenate(parts, axis=1).astype(BF16)
        s_c = jnp.dot(kc, qg, preferred_element_type=F32)
        p_parts, p_sum = [], None
        for r, h in enumerate(heads):
            s_r = jnp.where(cmask, s_c[:, r * TQ:(r + 1) * TQ] + cb_ref[0, h], NEG_INF)
            e = jnp.where(cmask, jnp.exp(s_r - jnp.max(s_r, axis=0, keepdims=True)), 0.0)
            p_r = e * (1.0 / jnp.maximum(jnp.sum(e, axis=0, keepdims=True), TINY))
            p_sum = p_r if p_sum is None else p_sum + p_r
            p_parts.append(p_r.astype(BF16))
        o_cmp = jnp.dot(vc_t, jnp.concatenate(p_parts, axis=1), preferred_element_type=F32)
        hi = p_sum.astype(BF16)
        lo = (p_sum - hi.astype(F32)).astype(BF16)
        imp_t = (jnp.dot(cover_ref[...], hi, preferred_element_type=F32)
                 + jnp.dot(cover_ref[...], lo, preferred_element_type=F32))
        score_t = jnp.where(allowed_t, jnp.where(forced_t, FORCED_SCORE, imp_t[0:n_slc]), MASKED_SCORE)
        sel_t = _select_blocks_t(score_t, allowed_t, n_slc, min(TOP_N, n_slc)).astype(F32)
        sel_pad = jnp.concatenate([sel_t, jnp.zeros((LANES - n_slc, TQ), F32)], axis=0).astype(BF16)

        def sel_mask(kt):
            expand = (e_row + kt * (TK // SLC_BLOCK) == e_col).astype(BF16)
            return jnp.dot(expand, sel_pad, preferred_element_type=F32) > 0.5

        far_bias = lambda r: fb_ref[heads[r], 0:1, :]
        prev_bias = lambda r: nb_ref[heads[r], 0:TK, :]
        diag_bias = lambda r: nb_ref[heads[r], TK:2 * TK, :]

        def far_body(kt, state):
            k_t, v_t = kv_tile(ks_ref, vs_ref, kt)
            return _flash_step(qg, k_t, v_t, far_bias, sel_mask(kt), state)

        state = lax.fori_loop(0, jnp.maximum(qt - 1, 0), far_body, _flash_init())
        k_t, v_t = kv_tile(ks_ref, vs_ref, kt_prev)
        state = _flash_step(qg, k_t, v_t, prev_bias, sel_mask(kt_prev) & (qt >= 1), state)
        k_t, v_t = kv_tile(ks_ref, vs_ref, qt)
        state = _flash_step(qg, k_t, v_t, diag_bias, sel_mask(qt) & causal, state)
        o_slc = _flash_out(state)
        state = _flash_init()
        k_t, v_t = kv_tile(kw_ref, vw_ref, kt_far)
        state = _flash_step(qg, k_t, v_t, far_bias, (ki > qi) & (qt >= 2), state)
        k_t, v_t = kv_tile(kw_ref, vw_ref, kt_prev)
        state = _flash_step(qg, k_t, v_t, prev_bias, (ki >= 0) & (qt >= 1), state)
        k_t, v_t = kv_tile(kw_ref, vw_ref, qt)
        state = _flash_step(qg, k_t, v_t, diag_bias, causal, state)
        o_win = _flash_out(state)
        rows = slice(g * HEAD_DIM, (g + 1) * HEAD_DIM)
        for r, h in enumerate(heads):
            cols = slice(r * TQ, (r + 1) * TQ)
            pieces.append(g_t[h:h + 1, :] * o_cmp[rows, cols]
                          + g_t[N_HEADS + h:N_HEADS + h + 1, :] * o_slc[rows, cols]
                          + g_t[2 * N_HEADS + h:2 * N_HEADS + h + 1, :] * o_win[rows, cols])
    out_ref[...] = jnp.concatenate(pieces, axis=0).T * _silu(z_ref[...])


def _attn_prompt(q2d, gate2d, z2d, kcvc, ks, vs_t, kw, vw_t, nb, cb, fb, cover_t, *, bsz, seq):
    assert WINDOW == 2 * TK and seq % TQ == 0
    n_qt = seq // TQ
    nc = seq // CMP_STRIDE
    n_slc = seq // SLC_BLOCK
    kern = functools.partial(_attn_prompt_kernel, n_slc=n_slc)
    tok = lambda w: pl.BlockSpec((TQ, w), lambda b, t: (b * n_qt + t, 0))
    k_spec = pl.BlockSpec((seq, LANES), lambda b, t: (b, 0))
    v_spec = pl.BlockSpec((1, LANES, seq), lambda b, t: (b, 0, 0))
    return pl.pallas_call(
        kern, grid=(bsz, n_qt),
        in_specs=[tok(D_ATTN), tok(LANES), tok(D_ATTN),
                  pl.BlockSpec((1, nc, KV_DIM), lambda b, t: (b, 0, 0)), k_spec, v_spec, k_spec, v_spec,
                  pl.BlockSpec((N_HEADS, 2 * TK, TQ), lambda b, t: (0, 0, 0)),
                  pl.BlockSpec((1, N_HEADS, nc, TQ), lambda b, t: (t, 0, 0, 0)),
                  pl.BlockSpec((N_HEADS, 8, TQ), lambda b, t: (0, 0, 0)),
                  pl.BlockSpec((LANES, nc), lambda b, t: (0, 0))],
        out_specs=tok(D_ATTN),
        out_shape=jax.ShapeDtypeStruct((bsz * seq, D_ATTN), F32),
        name="attn_prompt",
        compiler_params=pltpu.CompilerParams(dimension_semantics=("arbitrary", "arbitrary"),
                                             vmem_limit_bytes=VMEM_LIMIT),
    )(q2d, gate2d, z2d, kcvc, ks, vs_t, kw, vw_t, nb, cb, fb, cover_t)


def _attn_sample_kernel(pt_ref, *refs, n_pages, nq, n_seq, aliased_state):
    del pt_ref
    shared = list(refs[n_seq * n_pages:])
    if aliased_state:
        del shared[-3]
    for i in range(n_seq):
        _attn_sample_one(i, refs[i * n_pages:(i + 1) * n_pages], *shared, n_pages=n_pages, nq=nq)


def _attn_sample_one(i, pages, win_ref, kvsn_ref, kvwn_ref, q_ref, gate_ref, z_ref, kcvc_ref, sba_ref, swa_ref,
                     sbb_ref, sc_ref, cover_ref, expand_ref, out_ref, nwin_ref, *, n_pages, nq):
    past = n_pages * PAGE_SIZE
    wbuf = win_ref.shape[-1]
    nc = kcvc_ref.shape[1]
    rows = KV_HEADS * GROUP * nq
    cur = past // SLC_BLOCK
    n_slc = cur + 1
    low = lax.broadcasted_iota(jnp.int32, (nq, LANES), 1) < HALF
    q = q_ref[i]
    q_left = (jnp.concatenate(_head_rows(q, 0, low) + _head_rows(q, 1, low), axis=0) * SCALE).astype(BF16)
    qi = lax.broadcasted_iota(jnp.int32, (rows, LANES), 0) & (nq - 1)
    ki = lax.broadcasted_iota(jnp.int32, (rows, LANES), 1)
    new_mask = (ki <= qi) & (ki < nq)
    pad_rows = jnp.zeros((LANES - nq, LANES), F32)

    def new_tile(ref, t):
        return jnp.concatenate([ref[i, :, t * LANES:(t + 1) * LANES], pad_rows], axis=0)

    kc = kcvc_ref[i, :, 0:LANES].astype(BF16)
    vc = kcvc_ref[i, :, LANES:2 * LANES].astype(BF16)
    n_idx = lax.broadcasted_iota(jnp.int32, (rows, nc), 1)
    p_c = _masked_softmax(_dot_nt(q_left, kc) + sc_ref[...], n_idx < nc - 1)
    o_cmp = _dot(p_c, vc)
    blk = lax.broadcasted_iota(jnp.int32, (nq, LANES), 1)
    is_blk = blk < n_slc
    forced = (blk == 0) | (blk == cur) | (blk == cur - 1)
    sel_rows = []
    for g in range(KV_HEADS):
        p_sum = p_c[g * GROUP * nq:g * GROUP * nq + nq]
        for r in range(1, GROUP):
            p_sum = p_sum + p_c[(g * GROUP + r) * nq:(g * GROUP + r + 1) * nq]
        imp = _dot_split(p_sum, cover_ref[...])
        score = jnp.where(is_blk, jnp.where(forced, FORCED_SCORE, imp), PAD_SCORE)
        cnt = jnp.zeros((nq, LANES), jnp.int32)
        for j in range(n_slc):
            col = score[:, j:j + 1]
            cnt = cnt + ((col > score) | ((col == score) & (blk > j))).astype(jnp.int32)
        sel_g = ((cnt < min(TOP_N, n_slc)) & is_blk).astype(F32)
        sel_rows += [sel_g] * GROUP
    sel = jnp.concatenate(sel_rows, axis=0)
    k_pages = [pages[p][0, 0, 0].reshape(LANES, PAGE_SIZE).astype(BF16) for p in range(n_pages)]
    v_pages = [pages[p][0, 0, 1].reshape(LANES, PAGE_SIZE).astype(BF16) for p in range(n_pages)]
    k_new = new_tile(kvsn_ref, 0).astype(BF16)
    v_new = new_tile(kvsn_ref, 1).astype(BF16)
    s_a = jnp.concatenate([_dot(q_left, k) for k in k_pages], axis=1) + sba_ref[...]
    mask_a = jnp.dot(sel.astype(BF16), expand_ref[...], preferred_element_type=F32) > 0.5
    s_b = _dot_nt(q_left, k_new) + sbb_ref[...]
    mask_b = new_mask & (sel[:, cur:cur + 1] > 0.5)
    s_a = jnp.where(mask_a, s_a, NEG_INF)
    s_b = jnp.where(mask_b, s_b, NEG_INF)
    m = jnp.maximum(jnp.max(s_a, axis=-1, keepdims=True), jnp.max(s_b, axis=-1, keepdims=True))
    p_a = jnp.exp(s_a - m) * mask_a.astype(F32)
    p_b = jnp.exp(s_b - m) * mask_b.astype(F32)
    l = jnp.sum(p_a, axis=-1, keepdims=True) + jnp.sum(p_b, axis=-1, keepdims=True)
    acc = _dot(p_b, v_new)
    for p in range(n_pages):
        acc = acc + _dot_nt(p_a[:, p * PAGE_SIZE:(p + 1) * PAGE_SIZE], v_pages[p])
    o_slc = acc / jnp.maximum(l, TINY)
    win_t = [win_ref[0, i, t].reshape(LANES, wbuf) for t in range(2)]
    new_w = [new_tile(kvwn_ref, t) for t in range(2)]
    lane_w = lax.broadcasted_iota(jnp.int32, (LANES, wbuf), 1)
    for t in range(2):
        placed = jnp.concatenate([jnp.zeros((LANES, wbuf - LANES), F32), pltpu.roll(new_w[t].T, LANES - nq, 1)],
                                 axis=1)
        shifted = pltpu.roll(win_t[t], wbuf - nq, 1)
        nwin_ref[0, i, t] = jnp.where(lane_w < wbuf - nq, shifted, placed).reshape(KV_HEADS, HEAD_DIM, wbuf)
    kw_t, vw_t = win_t[0].astype(BF16), win_t[1].astype(BF16)
    kw_new, vw_new = new_w[0].astype(BF16), new_w[1].astype(BF16)
    jw = lax.broadcasted_iota(jnp.int32, (rows, wbuf), 1)
    qw = lax.broadcasted_iota(jnp.int32, (rows, wbuf), 0) & (nq - 1)
    mask_wa = jw > qw
    s_wa = jnp.where(mask_wa, _dot(q_left, kw_t) + swa_ref[...], NEG_INF)
    s_wb = jnp.where(new_mask, _dot_nt(q_left, kw_new) + sbb_ref[...], NEG_INF)
    m = jnp.maximum(jnp.max(s_wa, axis=-1, keepdims=True), jnp.max(s_wb, axis=-1, keepdims=True))
    p_wa = jnp.exp(s_wa - m) * mask_wa.astype(F32)
    p_wb = jnp.exp(s_wb - m) * new_mask.astype(F32)
    l = jnp.sum(p_wa, axis=-1, keepdims=True) + jnp.sum(p_wb, axis=-1, keepdims=True)
    acc = _dot_nt(p_wa, vw_t) + _dot(p_wb, vw_new)
    o_win = acc / jnp.maximum(l, TINY)
    gsig = _sigmoid(gate_ref[i])
    o_heads = []
    for h in range(N_HEADS):
        rs = slice(h * nq, (h + 1) * nq)
        o_heads.append(gsig[:, h:h + 1] * o_cmp[rs] + gsig[:, N_HEADS + h:N_HEADS + h + 1] * o_slc[rs]
                       + gsig[:, 2 * N_HEADS + h:2 * N_HEADS + h + 1] * o_win[rs])
    out_ref[i] = _assemble_heads(o_heads, low) * _silu(z_ref[i])


def _attn_sample(cache_t, win_t, layer, table, kvs_new, kvw_new, q, gate, z, kcvc, sba, swa, sbb, sc, cover, expand,
                 new_state=None):
    bsz, n_pages = table.shape
    nq = q.shape[1]
    wbuf = win_t.shape[-1]
    nc = kcvc.shape[1]
    past = n_pages * PAGE_SIZE
    rows = N_HEADS * nq
    assert nq <= SLC_BLOCK and nq & (nq - 1) == 0 and past % SLC_BLOCK == 0 and wbuf == WINDOW
    n_seq = math.gcd(bsz, SEQS_PER_STEP)
    aliased_state = new_state is not None
    kern = functools.partial(_attn_sample_kernel, n_pages=n_pages, nq=nq, n_seq=n_seq, aliased_state=aliased_state)
    page_specs = [pl.BlockSpec((1, 1, 2, KV_HEADS, HEAD_DIM, PAGE_SIZE),
                               lambda b, pt, i=i, p=p: (layer, pt[b * n_seq + i, p], 0, 0, 0, 0))
                  for i in range(n_seq) for p in range(n_pages)]
    per_b = lambda r, w: pl.BlockSpec((n_seq, r, w), lambda b, pt: (b, 0, 0))
    const = lambda r, w: pl.BlockSpec((r, w), lambda b, pt: (0, 0))
    state_spec = pl.BlockSpec((1, n_seq, 2, KV_HEADS, HEAD_DIM, wbuf), lambda b, pt: (layer, b, 0, 0, 0, 0))
    in_specs = page_specs + [
        state_spec, per_b(nq, KV_DIM), per_b(nq, KV_DIM), per_b(nq, D_ATTN), per_b(nq, LANES),
        per_b(nq, D_ATTN), per_b(nc, KV_DIM),
        const(rows, past), const(rows, wbuf), const(rows, LANES), const(rows, nc),
        const(nc, LANES), const(LANES, past)]
    args = [table, *([cache_t] * (n_seq * n_pages)), win_t, kvs_new, kvw_new, q, gate, z, kcvc, sba, swa, sbb, sc,
            cover, expand]
    aliases = {}
    if aliased_state:
        in_specs.append(pl.BlockSpec(memory_space=pl.ANY))
        aliases = {len(args): 1}
        args.append(new_state)
    grid_spec = pltpu.PrefetchScalarGridSpec(
        num_scalar_prefetch=1, grid=(bsz // n_seq,), in_specs=in_specs,
        out_specs=[per_b(nq, D_ATTN), state_spec])
    return pl.pallas_call(
        kern, grid_spec=grid_spec,
        out_shape=[jax.ShapeDtypeStruct((bsz, nq, D_ATTN), F32),
                   jax.ShapeDtypeStruct(win_t.shape, F32)],
        input_output_aliases=aliases, name="attn_sample",
        compiler_params=pltpu.CompilerParams(dimension_semantics=("arbitrary",), vmem_limit_bytes=VMEM_LIMIT),
    )(*args)


def _out_proj_kernel(h_ref, conv_ref, attn_ref, ple_ref, wo_ref, wg_ref, wp_ref, fg_ref, out_ref, *, final):
    h = h_ref[...]
    h = h + jnp.dot(conv_ref[...].astype(BF16), wo_ref[0:D_CONV, :], preferred_element_type=F32)
    h = h + jnp.dot(attn_ref[...].astype(BF16), wo_ref[D_CONV:D_CONV + D_ATTN, :], preferred_element_type=F32)
    gate = _sigmoid(jnp.dot(h.astype(BF16), wg_ref[...], preferred_element_type=F32))
    h = h + gate * jnp.dot(ple_ref[0].astype(BF16), wp_ref[...], preferred_element_type=F32)
    if final:
        h = h * lax.rsqrt(jnp.mean(h * h, axis=-1, keepdims=True) + EPS) * fg_ref[...]
    out_ref[...] = h


def _out_proj(h2d, conv2d, attn2d, ple3d, layer, wo, wg, wp, fg, *, final):
    n = h2d.shape[0]
    tm = min(TM, n)
    ple_dim = ple3d.shape[-1]
    kern = functools.partial(_out_proj_kernel, final=final)
    tok = lambda w: pl.BlockSpec((tm, w), lambda i: (i, 0))
    const = lambda r, w: pl.BlockSpec((r, w), lambda i: (0, 0))
    return pl.pallas_call(
        kern, grid=(n // tm,),
        in_specs=[tok(D_MODEL), tok(D_CONV), tok(D_ATTN), pl.BlockSpec((1, tm, ple_dim), lambda i: (layer, i, 0)),
                  const(D_CONV + D_ATTN, D_MODEL), const(D_MODEL, D_MODEL), const(ple_dim, D_MODEL),
                  const(1, D_MODEL)],
        out_specs=tok(D_MODEL),
        out_shape=jax.ShapeDtypeStruct((n, D_MODEL), F32),
        name="out_proj",
        compiler_params=pltpu.CompilerParams(dimension_semantics=("arbitrary",), vmem_limit_bytes=VMEM_LIMIT),
    )(h2d, conv2d, attn2d, ple3d, wo, wg, wp, fg)


def _cover_matrix(n_cmp_rows, n_cmp, n_slc):
    c_start = np.arange(n_cmp_rows) * CMP_STRIDE
    c_end = c_start + CMP_LEN - 1
    s_start = np.arange(LANES) * SLC_BLOCK
    cover = (c_start[:, None] < s_start[None, :] + SLC_BLOCK) & (c_end[:, None] >= s_start[None, :])
    cover &= (np.arange(n_cmp_rows)[:, None] < n_cmp) & (np.arange(LANES)[None, :] < n_slc)
    return jnp.asarray(cover, dtype=BF16)


def _expand_matrix(past):
    e = np.arange(LANES)[:, None] == (np.arange(past)[None, :] // SLC_BLOCK)
    return jnp.asarray(e, dtype=BF16)


def kernel(x_prompt, x_sample, cache_cmp_kv, cache_slc_kv, page_table, state_win_kv, state_conv, p_prompt, p_sample, norm_g, w_in, conv_w, conv_b, conv_ln_g, conv_ln_b, cmp_pe, cmp_w1, cmp_w2, w_out, w_ple, w_ple_gate, rel_bias, final_norm_g):
    bp, seq, _ = x_prompt.shape
    bs, nq, _ = x_sample.shape
    depth = w_in.shape[0]
    n_pages = page_table.shape[1]
    past = n_pages * PAGE_SIZE
    wbuf = state_win_kv.shape[2]
    n_pool = cache_cmp_kv.shape[1]
    win_p = min(WINDOW, seq)

    nb, cb, fb, sba, swa, sbb, sc = _bias_tables(rel_bias, seq=seq, past=past, wbuf=wbuf, nq_s=nq)
    rows = N_HEADS * nq
    sba, swa, sbb, sc = (a.reshape(rows, a.shape[-1]) for a in (sba, swa, sbb, sc))
    nc_p, nc_s = seq // CMP_STRIDE, past // CMP_STRIDE
    cover_p = _cover_matrix(nc_p, nc_p - 1, seq // SLC_BLOCK)
    cover_s = _cover_matrix(nc_s, nc_s - 1, past // SLC_BLOCK + 1)
    expand_s = _expand_matrix(past)
    conv_zero = jnp.zeros((bp, CONV_HIST, D_CONV), F32)
    fg = final_norm_g.reshape(1, D_MODEL)
    to_t = lambda a: jnp.transpose(a, (0, 1, 3, 4, 5, 2))
    from_t = lambda a: jnp.transpose(a, (0, 1, 5, 2, 3, 4))
    cmp_t, slc_t, win_t = to_t(cache_cmp_kv), to_t(cache_slc_kv), to_t(state_win_kv)
    ple_p = p_prompt.reshape(depth, bp * seq, -1)
    ple_s = p_sample.reshape(depth, bs * nq, -1)

    hp = x_prompt.reshape(bp * seq, D_MODEL)
    hs = x_sample.reshape(bs * nq, D_MODEL)
    outs = [[] for _ in range(8)]
    new_win = None
    for i in range(depth):
        w_pad = jnp.pad(w_in[i], ((0, 0), (0, D_IN_PAD - D_IN))).astype(BF16)
        wkv_t = w_in[i][:, SEC_KVC[0]:SEC_KVW[1]].T.astype(BF16)
        g = norm_g[i].reshape(1, D_MODEL)
        w1 = cmp_w1[i].reshape(2, 2, CMP_FEAT, CMP_HIDDEN)
        w1cat = jnp.concatenate([w1[:, 0], w1[:, 1]], axis=-1).astype(BF16)
        pe_pad = jnp.pad(cmp_pe[i].reshape(2, 2, CMP_FEAT), ((0, 0), (0, 6), (0, 0)))
        w2big = jnp.zeros((N_TG * CMP_HIDDEN, KV_DIM), F32)
        for tg in range(N_TG):
            w2big = w2big.at[tg * CMP_HIDDEN:(tg + 1) * CMP_HIDDEN, tg * HEAD_DIM:(tg + 1) * HEAD_DIM].set(
                cmp_w2[i, tg // KV_HEADS])
        w2big = w2big.astype(BF16)
        wo, wg, wp = w_out[i].astype(BF16), w_ple_gate[i].astype(BF16), w_ple[i].astype(BF16)
        cw, cbias = conv_w[i], conv_b[i].reshape(1, D_CONV)
        lg, lb = conv_ln_g[i].reshape(1, D_CONV), conv_ln_b[i].reshape(1, D_CONV)
        final = i == depth - 1

        c3, q, kvc_t, kvs_t, kvw_t, z, gate, ks, vs_t, kw, vw_t = _in_proj(hp, g, w_pad, wkv_t, seq=seq)
        conv_out, new_conv = _conv_module(c3.reshape(bp, seq, 3 * D_CONV), conv_zero, cw, cbias, lg, lb)
        kcvc = _compress_seq(kvc_t, pe_pad, w1cat, w2big)
        attn = _attn_prompt(q, gate, z, kcvc, ks, vs_t, kw, vw_t, nb, cb, fb, cover_p.T, bsz=bp, seq=seq)
        hp = _out_proj(hp, conv_out.reshape(bp * seq, D_CONV), attn, ple_p, i, wo, wg, wp, fg, final=final)
        six_d = lambda a: a.reshape(bp, 2, KV_HEADS, HEAD_DIM, a.shape[-1])
        outs[0].append(six_d(kvc_t))
        outs[2].append(six_d(kvs_t))
        outs[4].append(six_d(kvw_t[:, :, seq - win_p:]))
        outs[6].append(new_conv)

        c3, q, kvc, kvs, kvw, z, gate = _in_proj(hs, g, w_pad)
        conv_out, new_conv = _conv_module(c3.reshape(bs, nq, 3 * D_CONV), state_conv[i], cw, cbias, lg, lb)
        kcvc = _compress_paged(cmp_t, i, page_table, pe_pad, w1cat, w2big)
        attn, new_win = _attn_sample(
            slc_t, win_t, i, page_table, kvs.reshape(bs, nq, KV_DIM), kvw.reshape(bs, nq, KV_DIM),
            q.reshape(bs, nq, D_ATTN), gate.reshape(bs, nq, LANES), z.reshape(bs, nq, D_ATTN), kcvc,
            sba, swa, sbb, sc, cover_s, expand_s, new_state=new_win)
        hs = _out_proj(hs, conv_out.reshape(bs * nq, D_CONV), attn.reshape(bs * nq, D_ATTN), ple_s, i,
                       wo, wg, wp, fg, final=final)
        outs[1].append(kvc.reshape(bs, nq, 2, KV_HEADS, HEAD_DIM))
        outs[3].append(kvs.reshape(bs, nq, 2, KV_HEADS, HEAD_DIM))
        outs[7].append(new_conv)

    outs[5] = None
    stacked = [new_win if o is None else jnp.stack(o) for o in outs]
    for k in (0, 2, 4, 5):
        stacked[k] = from_t(stacked[k])
    return (hp.reshape(bp, seq, D_MODEL), hs.reshape(bs, nq, D_MODEL)) + tuple(stacked)
```

```python
import functools
import math

import numpy as np
import jax
import jax.numpy as jnp
from jax import lax
from jax.experimental import pallas as pl
from jax.experimental.pallas import tpu as pltpu

F32 = jnp.float32
BF16 = jnp.bfloat16

D_MODEL = 1024
D_CONV = 512
CONV_WIDTH = 31
CONV_HIST = CONV_WIDTH - 1
HEAD_DIM = 64
N_HEADS = 8
KV_HEADS = 2
GROUP = N_HEADS // KV_HEADS
D_ATTN = N_HEADS * HEAD_DIM
KV_DIM = 2 * KV_HEADS * HEAD_DIM
N_BRANCH = 3
CMP_STRIDE = 16
CMP_LEN = 2 * CMP_STRIDE
CMP_HIDDEN = 128
SLC_BLOCK = 64
SLC_SHIFT = 6
TOP_N = 8
WINDOW = 512
NUM_BUCKETS = 32
MAX_DISTANCE = 128
PAGE_SIZE = 128
EPS = 1e-6
NEG_INF = -1e30
FORCED_SCORE = 1e4
MASKED_SCORE = -1e4
PAD_SCORE = -3e4
TINY = 1e-30
SCALE = HEAD_DIM ** -0.5

LANES = 128
HALF = LANES // 2
TQ = 256
TK = 256
TM = 512
VMEM_LIMIT = 56 * 1024 * 1024

SEC_CONV = (0, 3 * D_CONV)
SEC_Q = (SEC_CONV[1], SEC_CONV[1] + D_ATTN)
SEC_KVC = (SEC_Q[1], SEC_Q[1] + KV_DIM)
SEC_KVS = (SEC_KVC[1], SEC_KVC[1] + KV_DIM)
SEC_KVW = (SEC_KVS[1], SEC_KVS[1] + KV_DIM)
SEC_Z = (SEC_KVW[1], SEC_KVW[1] + D_ATTN)
SEC_GATE = (SEC_Z[1], SEC_Z[1] + LANES)
D_IN = SEC_Z[1] + N_BRANCH * N_HEADS
D_IN_PAD = SEC_GATE[1]
SECTIONS = (SEC_CONV, SEC_Q, SEC_KVC, SEC_KVS, SEC_KVW, SEC_Z, SEC_GATE)


def _bucket_lower_bounds():
    n = np.arange(0, 4 * MAX_DISTANCE, dtype=np.int64)
    max_exact = NUM_BUCKETS // 2
    nf = np.maximum(n, 1).astype(np.float32)
    large = max_exact + (np.log(nf / np.float32(max_exact)) / np.float32(math.log(MAX_DISTANCE / max_exact))
                         * np.float32(NUM_BUCKETS - max_exact)).astype(np.int32)
    large = np.minimum(large, NUM_BUCKETS - 1)
    bucket = np.where(n < max_exact, n, large)
    return [int(np.argmax(bucket >= b)) for b in range(NUM_BUCKETS)]


BUCKET_LOWER = _bucket_lower_bounds()


def _dot(a, b):
    return jnp.dot(a.astype(BF16), b.astype(BF16), preferred_element_type=F32)


def _dot_nt(a, b):
    return lax.dot_general(a.astype(BF16), b.astype(BF16), (((1,), (1,)), ((), ())),
                           preferred_element_type=F32)


def _dot_split(a, b):
    hi = a.astype(BF16)
    lo = (a - hi.astype(F32)).astype(BF16)
    return (jnp.dot(hi, b, preferred_element_type=F32) + jnp.dot(lo, b, preferred_element_type=F32))


def _sigmoid(x):
    return 1.0 / (1.0 + jnp.exp(-x))


def _silu(x):
    return x * _sigmoid(x)


def _masked_softmax(s, mask):
    s = jnp.where(mask, s, NEG_INF)
    m = jnp.max(s, axis=-1, keepdims=True)
    e = jnp.exp(s - m) * mask.astype(F32)
    l = jnp.sum(e, axis=-1, keepdims=True)
    return e / jnp.maximum(l, TINY)


def _bias_of(dist, rb_ref, h):
    out = jnp.full(dist.shape, rb_ref[0, h], F32)
    for b in range(1, NUM_BUCKETS):
        out = jnp.where(dist >= BUCKET_LOWER[b], rb_ref[b, h], out)
    return out


def _bias_kernel(rb_ref, nb_ref, cb_ref, fb_ref, sba_ref, swa_ref, sbb_ref, sc_ref, *, n_qt, nc_p, past, wbuf, nq_s):
    h = pl.program_id(0)
    c = lax.broadcasted_iota(jnp.int32, (2 * TK, TQ), 0)
    qi = lax.broadcasted_iota(jnp.int32, (2 * TK, TQ), 1)
    nb_ref[0] = _bias_of(qi + TK - c, rb_ref, h)
    nn = lax.broadcasted_iota(jnp.int32, (nc_p, TQ), 0)
    qn = lax.broadcasted_iota(jnp.int32, (nc_p, TQ), 1)
    for t in range(n_qt):
        cb_ref[t, 0] = _bias_of(t * TQ + qn - (nn * CMP_STRIDE + CMP_LEN - 1), rb_ref, h)
    fb_ref[0] = jnp.full(fb_ref.shape[1:], rb_ref[NUM_BUCKETS - 1, h], F32)
    qs = lax.broadcasted_iota(jnp.int32, (nq_s, past), 0)
    ks = lax.broadcasted_iota(jnp.int32, (nq_s, past), 1)
    sba_ref[0] = _bias_of(past + qs - ks, rb_ref, h)
    qs = lax.broadcasted_iota(jnp.int32, (nq_s, wbuf), 0)
    ks = lax.broadcasted_iota(jnp.int32, (nq_s, wbuf), 1)
    swa_ref[0] = _bias_of(wbuf + qs - ks, rb_ref, h)
    qs = lax.broadcasted_iota(jnp.int32, (nq_s, LANES), 0)
    ks = lax.broadcasted_iota(jnp.int32, (nq_s, LANES), 1)
    sbb_ref[0] = _bias_of(qs - ks, rb_ref, h)
    nc_s = sc_ref.shape[2]
    qs = lax.broadcasted_iota(jnp.int32, (nq_s, nc_s), 0)
    ks = lax.broadcasted_iota(jnp.int32, (nq_s, nc_s), 1)
    sc_ref[0] = _bias_of(past + qs - (ks * CMP_STRIDE + CMP_LEN - 1), rb_ref, h)


def _bias_tables(rel_bias, *, seq, past, wbuf, nq_s):
    n_qt = seq // TQ
    nc_p = seq // CMP_STRIDE
    nc_s = past // CMP_STRIDE
    kern = functools.partial(_bias_kernel, n_qt=n_qt, nc_p=nc_p, past=past, wbuf=wbuf, nq_s=nq_s)
    shapes = (
        jax.ShapeDtypeStruct((N_HEADS, 2 * TK, TQ), F32),
        jax.ShapeDtypeStruct((n_qt, N_HEADS, nc_p, TQ), F32),
        jax.ShapeDtypeStruct((N_HEADS, 8, TQ), F32),
        jax.ShapeDtypeStruct((N_HEADS, nq_s, past), F32),
        jax.ShapeDtypeStruct((N_HEADS, nq_s, wbuf), F32),
        jax.ShapeDtypeStruct((N_HEADS, nq_s, LANES), F32),
        jax.ShapeDtypeStruct((N_HEADS, nq_s, nc_s), F32),
    )
    out_specs = (
        pl.BlockSpec((1, 2 * TK, TQ), lambda h: (h, 0, 0)),
        pl.BlockSpec((n_qt, 1, nc_p, TQ), lambda h: (0, h, 0, 0)),
        pl.BlockSpec((1, 8, TQ), lambda h: (h, 0, 0)),
        pl.BlockSpec((1, nq_s, past), lambda h: (h, 0, 0)),
        pl.BlockSpec((1, nq_s, wbuf), lambda h: (h, 0, 0)),
        pl.BlockSpec((1, nq_s, LANES), lambda h: (h, 0, 0)),
        pl.BlockSpec((1, nq_s, nc_s), lambda h: (h, 0, 0)),
    )
    return pl.pallas_call(
        kern, grid=(N_HEADS,),
        in_specs=[pl.BlockSpec(memory_space=pltpu.SMEM)],
        out_specs=out_specs, out_shape=shapes, name="bias_tables",
        compiler_params=pltpu.CompilerParams(dimension_semantics=("arbitrary",)),
    )(rel_bias)


KV_SECTIONS = (SEC_KVC, SEC_KVS, SEC_KVW)
ATTN_KV_SECTIONS = (SEC_KVS, SEC_KVW)


def _in_proj_kernel(x_ref, g_ref, w_ref, *refs, kv_transposed):
    x = x_ref[...]
    u = x * lax.rsqrt(jnp.mean(x * x, axis=-1, keepdims=True) + EPS) * g_ref[...]
    ub = u.astype(BF16)
    if kv_transposed:
        wkv_t_ref, out_refs, attn_refs = refs[0], refs[1:1 + len(SECTIONS)], refs[1 + len(SECTIONS):]
    else:
        out_refs = refs
    for ref, sec in zip(out_refs, SECTIONS):
        if kv_transposed and sec in KV_SECTIONS:
            k = KV_SECTIONS.index(sec)
            kv_t = lax.dot_general(wkv_t_ref[k * KV_DIM:(k + 1) * KV_DIM, :], ub, (((1,), (1,)), ((), ())),
                                   preferred_element_type=F32)
            ref[0] = kv_t
            if sec in ATTN_KV_SECTIONS:
                j = ATTN_KV_SECTIONS.index(sec)
                attn_refs[2 * j][...] = kv_t[0:LANES, :].T.astype(BF16)
                attn_refs[2 * j + 1][0] = kv_t[LANES:2 * LANES, :].astype(BF16)
        else:
            ref[...] = jnp.dot(ub, w_ref[:, sec[0]:sec[1]], preferred_element_type=F32)


def _in_proj(x2d, g, w_pad, wkv_t=None, *, seq=None):
    n = x2d.shape[0]
    tm = min(TM, n)
    kv_transposed = wkv_t is not None
    in_specs = [pl.BlockSpec((tm, D_MODEL), lambda i: (i, 0)),
                pl.BlockSpec((1, D_MODEL), lambda i: (0, 0)),
                pl.BlockSpec((D_MODEL, D_IN_PAD), lambda i: (0, 0))]
    args = [x2d, g, w_pad]
    out_specs, out_shape = [], []
    if kv_transposed:
        in_specs.append(pl.BlockSpec((len(KV_SECTIONS) * KV_DIM, D_MODEL), lambda i: (0, 0)))
        args.append(wkv_t)
        tiles_per_seq = seq // tm
    for sec in SECTIONS:
        w = sec[1] - sec[0]
        if kv_transposed and sec in KV_SECTIONS:
            out_specs.append(pl.BlockSpec((1, w, tm), lambda i: (i // tiles_per_seq, 0, i % tiles_per_seq)))
            out_shape.append(jax.ShapeDtypeStruct((n // seq, w, seq), F32))
        else:
            out_specs.append(pl.BlockSpec((tm, w), lambda i: (i, 0)))
            out_shape.append(jax.ShapeDtypeStruct((n, w), F32))
    if kv_transposed:
        for _ in ATTN_KV_SECTIONS:
            out_specs.append(pl.BlockSpec((tm, LANES), lambda i: (i, 0)))
            out_shape.append(jax.ShapeDtypeStruct((n, LANES), BF16))
            out_specs.append(pl.BlockSpec((1, LANES, tm), lambda i: (i // tiles_per_seq, 0, i % tiles_per_seq)))
            out_shape.append(jax.ShapeDtypeStruct((n // seq, LANES, seq), BF16))
    return pl.pallas_call(
        functools.partial(_in_proj_kernel, kv_transposed=kv_transposed), grid=(n // tm,),
        in_specs=in_specs, out_specs=out_specs, out_shape=out_shape, name="in_proj",
        compiler_params=pltpu.CompilerParams(dimension_semantics=("arbitrary",), vmem_limit_bytes=VMEM_LIMIT),
    )(*args)


HIST_PAD = 32


def _conv_kernel(c3_ref, hist_ref, w_ref, b_ref, lg_ref, lb_ref, out_ref, new_ref, xbuf, sbuf, *, tt, n_t, n_seq):
    t = pl.program_id(1)
    off = HIST_PAD - CONV_HIST

    def load_history(i):
        xbuf[0:HIST_PAD, :] = jnp.zeros((HIST_PAD, D_CONV), F32)
        xbuf[off:HIST_PAD, :] = hist_ref[i]

    def one_sequence(i):
        if n_t == 1:
            load_history(i)
        else:
            pl.when(t == 0)(lambda: load_history(i))
        glu = c3_ref[i, :, 0:D_CONV] * _sigmoid(c3_ref[i, :, D_CONV:2 * D_CONV])
        xbuf[HIST_PAD:HIST_PAD + tt, :] = glu
        acc = jnp.zeros((tt, D_CONV), F32) + b_ref[...]
        for b in range(8):
            a_max = (CONV_WIDTH - 1 - b) // 8
            sbuf[0:tt + 8 * a_max, :] = xbuf[off + b:off + b + tt + 8 * a_max, :]
            for a in range(a_max + 1):
                k = 8 * a + b
                acc = acc + sbuf[8 * a:8 * a + tt, :] * w_ref[k:k + 1, :]
        mu = jnp.mean(acc, axis=-1, keepdims=True)
        xc = acc - mu
        var = jnp.mean(xc * xc, axis=-1, keepdims=True)
        y = xc * lax.rsqrt(var + EPS) * lg_ref[...] + lb_ref[...]
        out_ref[i] = _silu(y) * _silu(c3_ref[i, :, 2 * D_CONV:3 * D_CONV])
        tail = xbuf[off + tt:HIST_PAD + tt, :]
        if n_t == 1:
            new_ref[i] = tail
        else:
            @pl.when(t == n_t - 1)
            def _():
                new_ref[i] = tail

            @pl.when(t < n_t - 1)
            def _():
                xbuf[off:HIST_PAD, :] = tail

    if n_seq == 1:
        one_sequence(0)
    else:
        def body(i, carry):
            one_sequence(i)
            return carry

        lax.fori_loop(0, n_seq, body, 0)


def _conv_module(c3, hist, conv_w, conv_b, ln_g, ln_b):
    bsz, t_len, _ = c3.shape
    tt = min(256, t_len)
    n_t = t_len // tt
    n_seq = 1 if n_t > 1 else math.gcd(bsz, 16)
    kern = functools.partial(_conv_kernel, tt=tt, n_t=n_t, n_seq=n_seq)
    vec = pl.BlockSpec((1, D_CONV), lambda b, t: (0, 0))
    return pl.pallas_call(
        kern, grid=(bsz // n_seq, n_t),
        in_specs=[pl.BlockSpec((n_seq, tt, 3 * D_CONV), lambda b, t: (b, t, 0)),
                  pl.BlockSpec((n_seq, CONV_HIST, D_CONV), lambda b, t: (b, 0, 0)),
                  pl.BlockSpec((CONV_WIDTH, D_CONV), lambda b, t: (0, 0)),
                  vec, vec, vec],
        out_specs=[pl.BlockSpec((n_seq, tt, D_CONV), lambda b, t: (b, t, 0)),
                   pl.BlockSpec((n_seq, CONV_HIST, D_CONV), lambda b, t: (b, 0, 0))],
        out_shape=[jax.ShapeDtypeStruct((bsz, t_len, D_CONV), F32),
                   jax.ShapeDtypeStruct((bsz, CONV_HIST, D_CONV), F32)],
        scratch_shapes=[pltpu.VMEM((HIST_PAD + tt, D_CONV), F32), pltpu.VMEM((HIST_PAD + tt, D_CONV), F32)],
        name="conv_module",
        compiler_params=pltpu.CompilerParams(dimension_semantics=("arbitrary", "arbitrary")),
    )(c3, hist, conv_w, conv_b, ln_g, ln_b)


HALVES_PER_PAGE = PAGE_SIZE // CMP_STRIDE
N_TG = 2 * KV_HEADS
N_LT = KV_DIM // LANES
CMP_FEAT = CMP_STRIDE * HEAD_DIM
X_PITCH = 24
SEQS_PER_STEP = 4


def _compress_body(page_tile, n_pages, pe_ref, w1_ref, w2_ref, out_ref, x_scr, y_scr, h_scr):
    n_half = n_pages * HALVES_PER_PAGE
    n_seq = out_ref.shape[0]
    low = lax.broadcasted_iota(jnp.int32, (n_half, LANES), 1) < HALF
    for i in range(n_seq):
        for p in range(n_pages):
            for t in range(N_LT):
                tile = page_tile(i, p, t).T
                for n in range(HALVES_PER_PAGE):
                    dst = (p * HALVES_PER_PAGE + n) * X_PITCH
                    x_scr[t, dst:dst + CMP_STRIDE, :] = tile[n * CMP_STRIDE:(n + 1) * CMP_STRIDE, :]
        rows = slice(i * n_half, (i + 1) * n_half)
        for j in range(CMP_STRIDE // 2):
            for t in range(N_LT):
                a = x_scr[t, pl.ds(2 * j, n_half, stride=X_PITCH), :]
                b = x_scr[t, pl.ds(2 * j + 1, n_half, stride=X_PITCH), :]
                y_scr[2 * t, rows, j * LANES:(j + 1) * LANES] = jnp.where(low, a, pltpu.roll(b, HALF, 1))
                y_scr[2 * t + 1, rows, j * LANES:(j + 1) * LANES] = jnp.where(low, pltpu.roll(a, HALF, 1), b)
    n_rows = n_seq * n_half
    h_scr[n_rows:n_rows + 8, :] = jnp.zeros((8, 2 * CMP_HIDDEN), F32)
    acts = []
    for tg in range(N_TG):
        t = tg // KV_HEADS
        w1 = w1_ref[t]
        c = jnp.dot(pe_ref[t].astype(BF16), w1, preferred_element_type=F32)
        cvec = c[0:1, 0:CMP_HIDDEN] + c[1:2, CMP_HIDDEN:2 * CMP_HIDDEN]
        h_scr[0:n_rows, :] = jnp.dot(y_scr[tg].astype(BF16), w1, preferred_element_type=F32)
        hid = h_scr[0:n_rows, 0:CMP_HIDDEN] + h_scr[1:n_rows + 1, CMP_HIDDEN:2 * CMP_HIDDEN] + cvec
        acts.append(_silu(hid))
    out = jnp.dot(jnp.concatenate(acts, axis=1).astype(BF16), w2_ref[...], preferred_element_type=F32)
    row = lax.broadcasted_iota(jnp.int32, out.shape, 0) & (n_half - 1)
    out_ref[...] = jnp.where(row < n_half - 1, out, 0.0).reshape(n_seq, n_half, KV_DIM)


def _compress_paged_kernel(pt_ref, *refs, n_pages, n_seq):
    del pt_ref
    pages = refs[:n_seq * n_pages]

    def page_tile(i, p, t):
        return pages[i * n_pages + p][0, 0, t].reshape(LANES, PAGE_SIZE)

    _compress_body(page_tile, n_pages, *refs[n_seq * n_pages:])


def _compress_seq_kernel(kv_ref, *refs, n_pages):
    def page_tile(i, p, t):
        return kv_ref[i, t * LANES:(t + 1) * LANES, p * PAGE_SIZE:(p + 1) * PAGE_SIZE]

    _compress_body(page_tile, n_pages, *refs)


def _compress_specs(n_half, n_seq):
    assert n_half & (n_half - 1) == 0
    const = lambda shape: pl.BlockSpec(shape, lambda *a: (0,) * len(shape))
    weight_specs = [const((2, 8, CMP_FEAT)), const((2, CMP_FEAT, 2 * CMP_HIDDEN)), const((N_TG * CMP_HIDDEN, KV_DIM))]
    out_spec = pl.BlockSpec((n_seq, n_half, KV_DIM), lambda b, *a: (b, 0, 0))
    scratch = [pltpu.VMEM((N_LT, n_half * X_PITCH, LANES), F32),
               pltpu.VMEM((N_TG, n_seq * n_half, CMP_FEAT), F32),
               pltpu.VMEM((n_seq * n_half + 8, 2 * CMP_HIDDEN), F32)]
    return weight_specs, out_spec, scratch


def _compress_paged(cache_t, layer, table, pe_pad, w1cat, w2big):
    bsz, n_pages = table.shape
    n_half = n_pages * HALVES_PER_PAGE
    n_seq = math.gcd(bsz, SEQS_PER_STEP)
    page_specs = [pl.BlockSpec((1, 1, 2, KV_HEADS, HEAD_DIM, PAGE_SIZE),
                               lambda b, pt, i=i, p=p: (layer, pt[b * n_seq + i, p], 0, 0, 0, 0))
                  for i in range(n_seq) for p in range(n_pages)]
    weight_specs, out_spec, scratch = _compress_specs(n_half, n_seq)
    grid_spec = pltpu.PrefetchScalarGridSpec(
        num_scalar_prefetch=1, grid=(bsz // n_seq,), in_specs=page_specs + weight_specs, out_specs=out_spec,
        scratch_shapes=scratch)
    return pl.pallas_call(
        functools.partial(_compress_paged_kernel, n_pages=n_pages, n_seq=n_seq), grid_spec=grid_spec,
        out_shape=jax.ShapeDtypeStruct((bsz, n_half, KV_DIM), F32), name="compress_paged",
        compiler_params=pltpu.CompilerParams(dimension_semantics=("arbitrary",), vmem_limit_bytes=VMEM_LIMIT),
    )(table, *([cache_t] * (n_seq * n_pages)), pe_pad, w1cat, w2big)


def _compress_seq(kv_t, pe_pad, w1cat, w2big):
    bsz, _, seq = kv_t.shape
    n_pages = seq // PAGE_SIZE
    n_half = n_pages * HALVES_PER_PAGE
    weight_specs, out_spec, scratch = _compress_specs(n_half, 1)
    return pl.pallas_call(
        functools.partial(_compress_seq_kernel, n_pages=n_pages), grid=(bsz,),
        in_specs=[pl.BlockSpec((1, KV_DIM, seq), lambda b: (b, 0, 0))] + weight_specs, out_specs=out_spec,
        scratch_shapes=scratch,
        out_shape=jax.ShapeDtypeStruct((bsz, n_half, KV_DIM), F32), name="compress_seq",
        compiler_params=pltpu.CompilerParams(dimension_semantics=("arbitrary",), vmem_limit_bytes=VMEM_LIMIT),
    )(kv_t, pe_pad, w1cat, w2big)


def _head_rows(q, g, low):
    parts = []
    for r in range(GROUP):
        h = GROUP * g + r
        tile = q[:, LANES * (h // 2):LANES * (h // 2 + 1)]
        if (h % 2) != g:
            tile = pltpu.roll(tile, HALF, 1)
        parts.append(jnp.where(low, tile, 0.0) if g == 0 else jnp.where(low, 0.0, tile))
    return parts


def _assemble_heads(o_heads, low):
    tiles = []
    for j in range(N_HEADS // 2):
        a, b = o_heads[2 * j], o_heads[2 * j + 1]
        if (2 * j) // GROUP == 0:
            tiles.append(jnp.where(low, a, pltpu.roll(b, HALF, 1)))
        else:
            tiles.append(jnp.where(low, pltpu.roll(a, HALF, 1), b))
    return jnp.concatenate(tiles, axis=1)


def _select_blocks_t(score_t, allowed_t, n_blk, top_n):
    idx = lax.broadcasted_iota(jnp.int32, score_t.shape, 0)
    cnt = jnp.zeros(score_t.shape, jnp.int32)
    for i in range(n_blk):
        row = score_t[i:i + 1, :]
        ahead = (row > score_t) | ((row == score_t) & (idx > i))
        cnt = cnt + ahead.astype(jnp.int32)
    return (cnt < top_n) & allowed_t


def _flash_step(q_t, k_tile, v_tile, bias, mask, state):
    m, l, acc = state
    s = jnp.dot(k_tile, q_t, preferred_element_type=F32)
    mask_add = jnp.where(mask, 0.0, NEG_INF)
    ms, ls, ps, alphas = [], [], [], []
    for r in range(GROUP):
        cols = slice(r * TQ, (r + 1) * TQ)
        s_r = s[:, cols] + bias(r) + mask_add
        m_r = jnp.maximum(m[:, cols], jnp.max(s_r, axis=0, keepdims=True))
        m_use = jnp.where(m_r > 0.5 * NEG_INF, m_r, 0.0)
        alpha = jnp.exp(m[:, cols] - m_use)
        p_r = jnp.exp(s_r - m_use)
        ls.append(alpha * l[:, cols] + jnp.sum(p_r, axis=0, keepdims=True))
        ms.append(m_r)
        alphas.append(alpha)
        ps.append(p_r.astype(BF16))
    pv = jnp.dot(v_tile, jnp.concatenate(ps, axis=1), preferred_element_type=F32)
    acc = jnp.concatenate(alphas, axis=1) * acc + pv
    return jnp.concatenate(ms, axis=1), jnp.concatenate(ls, axis=1), acc


def _flash_init():
    return (jnp.full((1, GROUP * TQ), NEG_INF, F32), jnp.zeros((1, GROUP * TQ), F32),
            jnp.zeros((LANES, GROUP * TQ), F32))


def _flash_out(state):
    _, l, acc = state
    return acc * (1.0 / jnp.maximum(l, TINY))


def _attn_prompt_kernel(q_ref, gate_ref, z_ref, kcvc_ref, ks_ref, vs_ref, kw_ref, vw_ref, nb_ref, cb_ref, fb_ref,
                        cover_ref, out_ref, *, n_slc):
    qt = pl.program_id(1)
    nc = kcvc_ref.shape[1]
    ki = lax.broadcasted_iota(jnp.int32, (TK, TQ), 0)
    qi = lax.broadcasted_iota(jnp.int32, (TK, TQ), 1)
    causal = qi >= ki
    q_t = q_ref[...].T
    g_t = _sigmoid(gate_ref[...]).T
    kc = kcvc_ref[0, :, 0:LANES].astype(BF16)
    vc_t = kcvc_ref[0, :, LANES:2 * LANES].T.astype(BF16)
    c_end = lax.broadcasted_iota(jnp.int32, (nc, TQ), 0) * CMP_STRIDE + (CMP_LEN - 1)
    cmask = qt * TQ + lax.broadcasted_iota(jnp.int32, (nc, TQ), 1) >= c_end
    blk_t = lax.broadcasted_iota(jnp.int32, (n_slc, TQ), 0)
    cur_t = jnp.right_shift(qt * TQ + lax.broadcasted_iota(jnp.int32, (n_slc, TQ), 1), SLC_SHIFT)
    allowed_t = blk_t <= cur_t
    forced_t = (blk_t == 0) | (blk_t == cur_t) | (blk_t == cur_t - 1)
    e_row = jnp.right_shift(lax.broadcasted_iota(jnp.int32, (TK, LANES), 0), SLC_SHIFT)
    e_col = lax.broadcasted_iota(jnp.int32, (TK, LANES), 1)
    kt_prev = jnp.maximum(qt - 1, 0)
    kt_far = jnp.maximum(qt - 2, 0)
    zero_half = jnp.zeros((HEAD_DIM, TQ), F32)

    def kv_tile(k_ref, v_ref, kt):
        start = pl.multiple_of(kt * TK, TK)
        return k_ref[pl.ds(start, TK), :], v_ref[0, :, pl.ds(start, TK)]

    pieces = []
    for g in range(KV_HEADS):
        heads = [GROUP * g + r for r in range(GROUP)]
        parts = []
        for h in heads:
            x = q_t[h * HEAD_DIM:(h + 1) * HEAD_DIM, :] * SCALE
            parts.append(jnp.concatenate([x, zero_half] if g == 0 else [zero_half, x], axis=0))
        qg = jnp.concatenate(parts, axis=1).astype(BF16)
        s_c = jnp.dot(kc, qg, preferred_element_type=F32)
        p_parts, p_sum = [], None
        for r, h in enumerate(heads):
            s_r = jnp.where(cmask, s_c[:, r * TQ:(r + 1) * TQ] + cb_ref[0, h], NEG_INF)
            e = jnp.where(cmask, jnp.exp(s_r - jnp.max(s_r, axis=0, keepdims=True)), 0.0)
            p_r = e * (1.0 / jnp.maximum(jnp.sum(e, axis=0, keepdims=True), TINY))
            p_sum = p_r if p_sum is None else p_sum + p_r
            p_parts.append(p_r.astype(BF16))
        o_cmp = jnp.dot(vc_t, jnp.concatenate(p_parts, axis=1), preferred_element_type=F32)
        hi = p_sum.astype(BF16)
        lo = (p_sum - hi.astype(F32)).astype(BF16)
        imp_t = (jnp.dot(cover_ref[...], hi, preferred_element_type=F32)
                 + jnp.dot(cover_ref[...], lo, preferred_element_type=F32))
        score_t = jnp.where(allowed_t, jnp.where(forced_t, FORCED_SCORE, imp_t[0:n_slc]), MASKED_SCORE)
        sel_t = _select_blocks_t(score_t, allowed_t, n_slc, min(TOP_N, n_slc)).astype(F32)
        sel_pad = jnp.concatenate([sel_t, jnp.zeros((LANES - n_slc, TQ), F32)], axis=0).astype(BF16)

        def sel_mask(kt):
            expand = (e_row + kt * (TK // SLC_BLOCK) == e_col).astype(BF16)
            return jnp.dot(expand, sel_pad, preferred_element_type=F32) > 0.5

        far_bias = lambda r: fb_ref[heads[r], 0:1, :]
        prev_bias = lambda r: nb_ref[heads[r], 0:TK, :]
        diag_bias = lambda r: nb_ref[heads[r], TK:2 * TK, :]

        def far_body(kt, state):
            k_t, v_t = kv_tile(ks_ref, vs_ref, kt)
            return _flash_step(qg, k_t, v_t, far_bias, sel_mask(kt), state)

        state = lax.fori_loop(0, jnp.maximum(qt - 1, 0), far_body, _flash_init())
        k_t, v_t = kv_tile(ks_ref, vs_ref, kt_prev)
        state = _flash_step(qg, k_t, v_t, prev_bias, sel_mask(kt_prev) & (qt >= 1), state)
        k_t, v_t = kv_tile(ks_ref, vs_ref, qt)
        state = _flash_step(qg, k_t, v_t, diag_bias, sel_mask(qt) & causal, state)
        o_slc = _flash_out(state)
        state = _flash_init()
        k_t, v_t = kv_tile(kw_ref, vw_ref, kt_far)
        state = _flash_step(qg, k_t, v_t, far_bias, (ki > qi) & (qt >= 2), state)
        k_t, v_t = kv_tile(kw_ref, vw_ref, kt_prev)
        state = _flash_step(qg, k_t, v_t, prev_bias, (ki >= 0) & (qt >= 1), state)
        k_t, v_t = kv_tile(kw_ref, vw_ref, qt)
        state = _flash_step(qg, k_t, v_t, diag_bias, causal, state)
        o_win = _flash_out(state)
        rows = slice(g * HEAD_DIM, (g + 1) * HEAD_DIM)
        for r, h in enumerate(heads):
            cols = slice(r * TQ, (r + 1) * TQ)
            pieces.append(g_t[h:h + 1, :] * o_cmp[rows, cols]
                          + g_t[N_HEADS + h:N_HEADS + h + 1, :] * o_slc[rows, cols]
                          + g_t[2 * N_HEADS + h:2 * N_HEADS + h + 1, :] * o_win[rows, cols])
    out_ref[...] = jnp.concatenate(pieces, axis=0).T * _silu(z_ref[...])


def _attn_prompt(q2d, gate2d, z2d, kcvc, ks, vs_t, kw, vw_t, nb, cb, fb, cover_t, *, bsz, seq):
    assert WINDOW == 2 * TK and seq % TQ == 0
    n_qt = seq // TQ
    nc = seq // CMP_STRIDE
    n_slc = seq // SLC_BLOCK
    kern = functools.partial(_attn_prompt_kernel, n_slc=n_slc)
    tok = lambda w: pl.BlockSpec((TQ, w), lambda b, t: (b * n_qt + t, 0))
    k_spec = pl.BlockSpec((seq, LANES), lambda b, t: (b, 0))
    v_spec = pl.BlockSpec((1, LANES, seq), lambda b, t: (b, 0, 0))
    return pl.pallas_call(
        kern, grid=(bsz, n_qt),
        in_specs=[tok(D_ATTN), tok(LANES), tok(D_ATTN),
                  pl.BlockSpec((1, nc, KV_DIM), lambda b, t: (b, 0, 0)), k_spec, v_spec, k_spec, v_spec,
                  pl.BlockSpec((N_HEADS, 2 * TK, TQ), lambda b, t: (0, 0, 0)),
                  pl.BlockSpec((1, N_HEADS, nc, TQ), lambda b, t: (t, 0, 0, 0)),
                  pl.BlockSpec((N_HEADS, 8, TQ), lambda b, t: (0, 0, 0)),
                  pl.BlockSpec((LANES, nc), lambda b, t: (0, 0))],
        out_specs=tok(D_ATTN),
        out_shape=jax.ShapeDtypeStruct((bsz * seq, D_ATTN), F32),
        name="attn_prompt",
        compiler_params=pltpu.CompilerParams(dimension_semantics=("arbitrary", "arbitrary"),
                                             vmem_limit_bytes=VMEM_LIMIT),
    )(q2d, gate2d, z2d, kcvc, ks, vs_t, kw, vw_t, nb, cb, fb, cover_t)


def _attn_sample_kernel(pt_ref, *refs, n_pages, nq, n_seq, aliased_state):
    del pt_ref
    shared = list(refs[n_seq * n_pages:])
    if aliased_state:
        del shared[-3]
    for i in range(n_seq):
        _attn_sample_one(i, refs[i * n_pages:(i + 1) * n_pages], *shared, n_pages=n_pages, nq=nq)


def _attn_sample_one(i, pages, win_ref, kvsn_ref, kvwn_ref, q_ref, gate_ref, z_ref, kcvc_ref, sba_ref, swa_ref,
                     sbb_ref, sc_ref, cover_ref, expand_ref, out_ref, nwin_ref, *, n_pages, nq):
    past = n_pages * PAGE_SIZE
    wbuf = win_ref.shape[-1]
    nc = kcvc_ref.shape[1]
    rows = KV_HEADS * GROUP * nq
    cur = past // SLC_BLOCK
    n_slc = cur + 1
    low = lax.broadcasted_iota(jnp.int32, (nq, LANES), 1) < HALF
    q = q_ref[i]
    q_left = (jnp.concatenate(_head_rows(q, 0, low) + _head_rows(q, 1, low), axis=0) * SCALE).astype(BF16)
    qi = lax.broadcasted_iota(jnp.int32, (rows, LANES), 0) & (nq - 1)
    ki = lax.broadcasted_iota(jnp.int32, (rows, LANES), 1)
    new_mask = (ki <= qi) & (ki < nq)
    pad_rows = jnp.zeros((LANES - nq, LANES), F32)

    def new_tile(ref, t):
        return jnp.concatenate([ref[i, :, t * LANES:(t + 1) * LANES], pad_rows], axis=0)

    kc = kcvc_ref[i, :, 0:LANES].astype(BF16)
    vc = kcvc_ref[i, :, LANES:2 * LANES].astype(BF16)
    n_idx = lax.broadcasted_iota(jnp.int32, (rows, nc), 1)
    p_c = _masked_softmax(_dot_nt(q_left, kc) + sc_ref[...], n_idx < nc - 1)
    o_cmp = _dot(p_c, vc)
    blk = lax.broadcasted_iota(jnp.int32, (nq, LANES), 1)
    is_blk = blk < n_slc
    forced = (blk == 0) | (blk == cur) | (blk == cur - 1)
    sel_rows = []
    for g in range(KV_HEADS):
        p_sum = p_c[g * GROUP * nq:g * GROUP * nq + nq]
        for r in range(1, GROUP):
            p_sum = p_sum + p_c[(g * GROUP + r) * nq:(g * GROUP + r + 1) * nq]
        imp = _dot_split(p_sum, cover_ref[...])
        score = jnp.where(is_blk, jnp.where(forced, FORCED_SCORE, imp), PAD_SCORE)
        cnt = jnp.zeros((nq, LANES), jnp.int32)
        for j in range(n_slc):
            col = score[:, j:j + 1]
            cnt = cnt + ((col > score) | ((col == score) & (blk > j))).astype(jnp.int32)
        sel_g = ((cnt < min(TOP_N, n_slc)) & is_blk).astype(F32)
        sel_rows += [sel_g] * GROUP
    sel = jnp.concatenate(sel_rows, axis=0)
    k_pages = [pages[p][0, 0, 0].reshape(LANES, PAGE_SIZE).astype(BF16) for p in range(n_pages)]
    v_pages = [pages[p][0, 0, 1].reshape(LANES, PAGE_SIZE).astype(BF16) for p in range(n_pages)]
    k_new = new_tile(kvsn_ref, 0).astype(BF16)
    v_new = new_tile(kvsn_ref, 1).astype(BF16)
    s_a = jnp.concatenate([_dot(q_left, k) for k in k_pages], axis=1) + sba_ref[...]
    mask_a = jnp.dot(sel.astype(BF16), expand_ref[...], preferred_element_type=F32) > 0.5
    s_b = _dot_nt(q_left, k_new) + sbb_ref[...]
    mask_b = new_mask & (sel[:, cur:cur + 1] > 0.5)
    s_a = jnp.where(mask_a, s_a, NEG_INF)
    s_b = jnp.where(mask_b, s_b, NEG_INF)
    m = jnp.maximum(jnp.max(s_a, axis=-1, keepdims=True), jnp.max(s_b, axis=-1, keepdims=True))
    p_a = jnp.exp(s_a - m) * mask_a.astype(F32)
    p_b = jnp.exp(s_b - m) * mask_b.astype(F32)
    l = jnp.sum(p_a, axis=-1, keepdims=True) + jnp.sum(p_b, axis=-1, keepdims=True)
    acc = _dot(p_b, v_new)
    for p in range(n_pages):
        acc = acc + _dot_nt(p_a[:, p * PAGE_SIZE:(p + 1) * PAGE_SIZE], v_pages[p])
    o_slc = acc / jnp.maximum(l, TINY)
    win_t = [win_ref[0, i, t].reshape(LANES, wbuf) for t in range(2)]
    new_w = [new_tile(kvwn_ref, t) for t in range(2)]
    lane_w = lax.broadcasted_iota(jnp.int32, (LANES, wbuf), 1)
    for t in range(2):
        placed = jnp.concatenate([jnp.zeros((LANES, wbuf - LANES), F32), pltpu.roll(new_w[t].T, LANES - nq, 1)],
                                 axis=1)
        shifted = pltpu.roll(win_t[t], wbuf - nq, 1)
        nwin_ref[0, i, t] = jnp.where(lane_w < wbuf - nq, shifted, placed).reshape(KV_HEADS, HEAD_DIM, wbuf)
    kw_t, vw_t = win_t[0].astype(BF16), win_t[1].astype(BF16)
    kw_new, vw_new = new_w[0].astype(BF16), new_w[1].astype(BF16)
    jw = lax.broadcasted_iota(jnp.int32, (rows, wbuf), 1)
    qw = lax.broadcasted_iota(jnp.int32, (rows, wbuf), 0) & (nq - 1)
    mask_wa = jw > qw
    s_wa = jnp.where(mask_wa, _dot(q_left, kw_t) + swa_ref[...], NEG_INF)
    s_wb = jnp.where(new_mask, _dot_nt(q_left, kw_new) + sbb_ref[...], NEG_INF)
    m = jnp.maximum(jnp.max(s_wa, axis=-1, keepdims=True), jnp.max(s_wb, axis=-1, keepdims=True))
    p_wa = jnp.exp(s_wa - m) * mask_wa.astype(F32)
    p_wb = jnp.exp(s_wb - m) * new_mask.astype(F32)
    l = jnp.sum(p_wa, axis=-1, keepdims=True) + jnp.sum(p_wb, axis=-1, keepdims=True)
    acc = _dot_nt(p_wa, vw_t) + _dot(p_wb, vw_new)
    o_win = acc / jnp.maximum(l, TINY)
    gsig = _sigmoid(gate_ref[i])
    o_heads = []
    for h in range(N_HEADS):
        rs = slice(h * nq, (h + 1) * nq)
        o_heads.append(gsig[:, h:h + 1] * o_cmp[rs] + gsig[:, N_HEADS + h:N_HEADS + h + 1] * o_slc[rs]
                       + gsig[:, 2 * N_HEADS + h:2 * N_HEADS + h + 1] * o_win[rs])
    out_ref[i] = _assemble_heads(o_heads, low) * _silu(z_ref[i])


def _attn_sample(cache_t, win_t, layer, table, kvs_new, kvw_new, q, gate, z, kcvc, sba, swa, sbb, sc, cover, expand,
                 new_state=None):
    bsz, n_pages = table.shape
    nq = q.shape[1]
    wbuf = win_t.shape[-1]
    nc = kcvc.shape[1]
    past = n_pages * PAGE_SIZE
    rows = N_HEADS * nq
    assert nq <= SLC_BLOCK and nq & (nq - 1) == 0 and past % SLC_BLOCK == 0 and wbuf == WINDOW
    n_seq = math.gcd(bsz, SEQS_PER_STEP)
    aliased_state = new_state is not None
    kern = functools.partial(_attn_sample_kernel, n_pages=n_pages, nq=nq, n_seq=n_seq, aliased_state=aliased_state)
    page_specs = [pl.BlockSpec((1, 1, 2, KV_HEADS, HEAD_DIM, PAGE_SIZE),
                               lambda b, pt, i=i, p=p: (layer, pt[b * n_seq + i, p], 0, 0, 0, 0))
                  for i in range(n_seq) for p in range(n_pages)]
    per_b = lambda r, w: pl.BlockSpec((n_seq, r, w), lambda b, pt: (b, 0, 0))
    const = lambda r, w: pl.BlockSpec((r, w), lambda b, pt: (0, 0))
    state_spec = pl.BlockSpec((1, n_seq, 2, KV_HEADS, HEAD_DIM, wbuf), lambda b, pt: (layer, b, 0, 0, 0, 0))
    in_specs = page_specs + [
        state_spec, per_b(nq, KV_DIM), per_b(nq, KV_DIM), per_b(nq, D_ATTN), per_b(nq, LANES),
        per_b(nq, D_ATTN), per_b(nc, KV_DIM),
        const(rows, past), const(rows, wbuf), const(rows, LANES), const(rows, nc),
        const(nc, LANES), const(LANES, past)]
    args = [table, *([cache_t] * (n_seq * n_pages)), win_t, kvs_new, kvw_new, q, gate, z, kcvc, sba, swa, sbb, sc,
            cover, expand]
    aliases = {}
    if aliased_state:
        in_specs.append(pl.BlockSpec(memory_space=pl.ANY))
        aliases = {len(args): 1}
        args.append(new_state)
    grid_spec = pltpu.PrefetchScalarGridSpec(
        num_scalar_prefetch=1, grid=(bsz // n_seq,), in_specs=in_specs,
        out_specs=[per_b(nq, D_ATTN), state_spec])
    return pl.pallas_call(
        kern, grid_spec=grid_spec,
        out_shape=[jax.ShapeDtypeStruct((bsz, nq, D_ATTN), F32),
                   jax.ShapeDtypeStruct(win_t.shape, F32)],
        input_output_aliases=aliases, name="attn_sample",
        compiler_params=pltpu.CompilerParams(dimension_semantics=("arbitrary",), vmem_limit_bytes=VMEM_LIMIT),
    )(*args)


def _out_proj_kernel(h_ref, conv_ref, attn_ref, ple_ref, wo_ref, wg_ref, wp_ref, fg_ref, out_ref, *, final):
    h = h_ref[...]
    h = h + jnp.dot(conv_ref[...].astype(BF16), wo_ref[0:D_CONV, :], preferred_element_type=F32)
    h = h + jnp.dot(attn_ref[...].astype(BF16), wo_ref[D_CONV:D_CONV + D_ATTN, :], preferred_element_type=F32)
    gate = _sigmoid(jnp.dot(h.astype(BF16), wg_ref[...], preferred_element_type=F32))
    h = h + gate * jnp.dot(ple_ref[0].astype(BF16), wp_ref[...], preferred_element_type=F32)
    if final:
        h = h * lax.rsqrt(jnp.mean(h * h, axis=-1, keepdims=True) + EPS) * fg_ref[...]
    out_ref[...] = h


def _out_proj(h2d, conv2d, attn2d, ple3d, layer, wo, wg, wp, fg, *, final):
    n = h2d.shape[0]
    tm = min(TM, n)
    ple_dim = ple3d.shape[-1]
    kern = functools.partial(_out_proj_kernel, final=final)
    tok = lambda w: pl.BlockSpec((tm, w), lambda i: (i, 0))
    const = lambda r, w: pl.BlockSpec((r, w), lambda i: (0, 0))
    return pl.pallas_call(
        kern, grid=(n // tm,),
        in_specs=[tok(D_MODEL), tok(D_CONV), tok(D_ATTN), pl.BlockSpec((1, tm, ple_dim), lambda i: (layer, i, 0)),
                  const(D_CONV + D_ATTN, D_MODEL), const(D_MODEL, D_MODEL), const(ple_dim, D_MODEL),
                  const(1, D_MODEL)],
        out_specs=tok(D_MODEL),
        out_shape=jax.ShapeDtypeStruct((n, D_MODEL), F32),
        name="out_proj",
        compiler_params=pltpu.CompilerParams(dimension_semantics=("arbitrary",), vmem_limit_bytes=VMEM_LIMIT),
    )(h2d, conv2d, attn2d, ple3d, wo, wg, wp, fg)


def _cover_matrix(n_cmp_rows, n_cmp, n_slc):
    c_start = np.arange(n_cmp_rows) * CMP_STRIDE
    c_end = c_start + CMP_LEN - 1
    s_start = np.arange(LANES) * SLC_BLOCK
    cover = (c_start[:, None] < s_start[None, :] + SLC_BLOCK) & (c_end[:, None] >= s_start[None, :])
    cover &= (np.arange(n_cmp_rows)[:, None] < n_cmp) & (np.arange(LANES)[None, :] < n_slc)
    return jnp.asarray(cover, dtype=BF16)


def _expand_matrix(past):
    e = np.arange(LANES)[:, None] == (np.arange(past)[None, :] // SLC_BLOCK)
    return jnp.asarray(e, dtype=BF16)


def kernel(x_prompt, x_sample, cache_cmp_kv, cache_slc_kv, page_table, state_win_kv, state_conv, p_prompt, p_sample, norm_g, w_in, conv_w, conv_b, conv_ln_g, conv_ln_b, cmp_pe, cmp_w1, cmp_w2, w_out, w_ple, w_ple_gate, rel_bias, final_norm_g):
    bp, seq, _ = x_prompt.shape
    bs, nq, _ = x_sample.shape
    depth = w_in.shape[0]
    n_pages = page_table.shape[1]
    past = n_pages * PAGE_SIZE
    wbuf = state_win_kv.shape[2]
    n_pool = cache_cmp_kv.shape[1]
    win_p = min(WINDOW, seq)

    nb, cb, fb, sba, swa, sbb, sc = _bias_tables(rel_bias, seq=seq, past=past, wbuf=wbuf, nq_s=nq)
    rows = N_HEADS * nq
    sba, swa, sbb, sc = (a.reshape(rows, a.shape[-1]) for a in (sba, swa, sbb, sc))
    nc_p, nc_s = seq // CMP_STRIDE, past // CMP_STRIDE
    cover_p = _cover_matrix(nc_p, nc_p - 1, seq // SLC_BLOCK)
    cover_s = _cover_matrix(nc_s, nc_s - 1, past // SLC_BLOCK + 1)
    expand_s = _expand_matrix(past)
    conv_zero = jnp.zeros((bp, CONV_HIST, D_CONV), F32)
    fg = final_norm_g.reshape(1, D_MODEL)
    to_t = lambda a: jnp.transpose(a, (0, 1, 3, 4, 5, 2))
    from_t = lambda a: jnp.transpose(a, (0, 1, 5, 2, 3, 4))
    cmp_t, slc_t, win_t = to_t(cache_cmp_kv), to_t(cache_slc_kv), to_t(state_win_kv)
    ple_p = p_prompt.reshape(depth, bp * seq, -1)
    ple_s = p_sample.reshape(depth, bs * nq, -1)

    hp = x_prompt.reshape(bp * seq, D_MODEL)
    hs = x_sample.reshape(bs * nq, D_MODEL)
    outs = [[] for _ in range(8)]
    new_win = None
    for i in range(depth):
        w_pad = jnp.pad(w_in[i], ((0, 0), (0, D_IN_PAD - D_IN))).astype(BF16)
        wkv_t = w_in[i][:, SEC_KVC[0]:SEC_KVW[1]].T.astype(BF16)
        g = norm_g[i].reshape(1, D_MODEL)
        w1 = cmp_w1[i].reshape(2, 2, CMP_FEAT, CMP_HIDDEN)
        w1cat = jnp.concatenate([w1[:, 0], w1[:, 1]], axis=-1).astype(BF16)
        pe_pad = jnp.pad(cmp_pe[i].reshape(2, 2, CMP_FEAT), ((0, 0), (0, 6), (0, 0)))
        w2big = jnp.zeros((N_TG * CMP_HIDDEN, KV_DIM), F32)
        for tg in range(N_TG):
            w2big = w2big.at[tg * CMP_HIDDEN:(tg + 1) * CMP_HIDDEN, tg * HEAD_DIM:(tg + 1) * HEAD_DIM].set(
                cmp_w2[i, tg // KV_HEADS])
        w2big = w2big.astype(BF16)
        wo, wg, wp = w_out[i].astype(BF16), w_ple_gate[i].astype(BF16), w_ple[i].astype(BF16)
        cw, cbias = conv_w[i], conv_b[i].reshape(1, D_CONV)
        lg, lb = conv_ln_g[i].reshape(1, D_CONV), conv_ln_b[i].reshape(1, D_CONV)
        final = i == depth - 1

        c3, q, kvc_t, kvs_t, kvw_t, z, gate, ks, vs_t, kw, vw_t = _in_proj(hp, g, w_pad, wkv_t, seq=seq)
        conv_out, new_conv = _conv_module(c3.reshape(bp, seq, 3 * D_CONV), conv_zero, cw, cbias, lg, lb)
        kcvc = _compress_seq(kvc_t, pe_pad, w1cat, w2big)
        attn = _attn_prompt(q, gate, z, kcvc, ks, vs_t, kw, vw_t, nb, cb, fb, cover_p.T, bsz=bp, seq=seq)
        hp = _out_proj(hp, conv_out.reshape(bp * seq, D_CONV), attn, ple_p, i, wo, wg, wp, fg, final=final)
        six_d = lambda a: a.reshape(bp, 2, KV_HEADS, HEAD_DIM, a.shape[-1])
        outs[0].append(six_d(kvc_t))
        outs[2].append(six_d(kvs_t))
        outs[4].append(six_d(kvw_t[:, :, seq - win_p:]))
        outs[6].append(new_conv)

        c3, q, kvc, kvs, kvw, z, gate = _in_proj(hs, g, w_pad)
        conv_out, new_conv = _conv_module(c3.reshape(bs, nq, 3 * D_CONV), state_conv[i], cw, cbias, lg, lb)
        kcvc = _compress_paged(cmp_t, i, page_table, pe_pad, w1cat, w2big)
        attn, new_win = _attn_sample(
            slc_t, win_t, i, page_table, kvs.reshape(bs, nq, KV_DIM), kvw.reshape(bs, nq, KV_DIM),
            q.reshape(bs, nq, D_ATTN), gate.reshape(bs, nq, LANES), z.reshape(bs, nq, D_ATTN), kcvc,
            sba, swa, sbb, sc, cover_s, expand_s, new_state=new_win)
        hs = _out_proj(hs, conv_out.reshape(bs * nq, D_CONV), attn.reshape(bs * nq, D_ATTN), ple_s, i,
                       wo, wg, wp, fg, final=final)
        outs[1].append(kvc.reshape(bs, nq, 2, KV_HEADS, HEAD_DIM))
        outs[3].append(kvs.reshape(bs, nq, 2, KV_HEADS, HEAD_DIM))
        outs[7].append(new_conv)

    outs[5] = None
    stacked = [new_win if o is None else jnp.stack(o) for o in outs]
    for k in (0, 2, 4, 5):
        stacked[k] = from_t(stacked[k])
    return (hp.reshape(bp, seq, D_MODEL), hs.reshape(bs, nq, D_MODEL)) + tuple(stacked)
```

```python
import functools
import math

import numpy as np
import jax
import jax.numpy as jnp
from jax import lax
from jax.experimental import pallas as pl
from jax.experimental.pallas import tpu as pltpu

F32 = jnp.float32
BF16 = jnp.bfloat16

D_MODEL = 1024
D_CONV = 512
CONV_WIDTH = 31
CONV_HIST = CONV_WIDTH - 1
HEAD_DIM = 64
N_HEADS = 8
KV_HEADS = 2
GROUP = N_HEADS // KV_HEADS
D_ATTN = N_HEADS * HEAD_DIM
KV_DIM = 2 * KV_HEADS * HEAD_DIM
N_BRANCH = 3
CMP_STRIDE = 16
CMP_LEN = 2 * CMP_STRIDE
CMP_HIDDEN = 128
SLC_BLOCK = 64
SLC_SHIFT = 6
TOP_N = 8
WINDOW = 512
NUM_BUCKETS = 32
MAX_DISTANCE = 128
PAGE_SIZE = 128
EPS = 1e-6
NEG_INF = -1e30
FORCED_SCORE = 1e4
MASKED_SCORE = -1e4
PAD_SCORE = -3e4
TINY = 1e-30
SCALE = HEAD_DIM ** -0.5
LOG2E = math.log2(math.e)

LANES = 128
HALF = LANES // 2
TQ = 256
TK = 256
TM = 512
VMEM_LIMIT = 56 * 1024 * 1024

SEC_CONV = (0, 3 * D_CONV)
SEC_Q = (SEC_CONV[1], SEC_CONV[1] + D_ATTN)
SEC_KVC = (SEC_Q[1], SEC_Q[1] + KV_DIM)
SEC_KVS = (SEC_KVC[1], SEC_KVC[1] + KV_DIM)
SEC_KVW = (SEC_KVS[1], SEC_KVS[1] + KV_DIM)
SEC_Z = (SEC_KVW[1], SEC_KVW[1] + D_ATTN)
SEC_GATE = (SEC_Z[1], SEC_Z[1] + LANES)
D_IN = SEC_Z[1] + N_BRANCH * N_HEADS
D_IN_PAD = SEC_GATE[1]
SECTIONS = (SEC_CONV, SEC_Q, SEC_KVC, SEC_KVS, SEC_KVW, SEC_Z, SEC_GATE)


def _bucket_lower_bounds():
    n = np.arange(0, 4 * MAX_DISTANCE, dtype=np.int64)
    max_exact = NUM_BUCKETS // 2
    nf = np.maximum(n, 1).astype(np.float32)
    large = max_exact + (np.log(nf / np.float32(max_exact)) / np.float32(math.log(MAX_DISTANCE / max_exact))
                         * np.float32(NUM_BUCKETS - max_exact)).astype(np.int32)
    large = np.minimum(large, NUM_BUCKETS - 1)
    bucket = np.where(n < max_exact, n, large)
    return [int(np.argmax(bucket >= b)) for b in range(NUM_BUCKETS)]


BUCKET_LOWER = _bucket_lower_bounds()


def _dot(a, b):
    return jnp.dot(a.astype(BF16), b.astype(BF16), preferred_element_type=F32)


def _dot_nt(a, b):
    return lax.dot_general(a.astype(BF16), b.astype(BF16), (((1,), (1,)), ((), ())),
                           preferred_element_type=F32)


def _dot_split(a, b):
    hi = a.astype(BF16)
    lo = (a - hi.astype(F32)).astype(BF16)
    return (jnp.dot(hi, b, preferred_element_type=F32) + jnp.dot(lo, b, preferred_element_type=F32))


def _sigmoid(x):
    return 1.0 / (1.0 + jnp.exp(-x))


def _silu(x):
    return x * _sigmoid(x)


def _masked_softmax(s, mask):
    s = jnp.where(mask, s, NEG_INF)
    m = jnp.max(s, axis=-1, keepdims=True)
    e = jnp.exp(s - m) * mask.astype(F32)
    l = jnp.sum(e, axis=-1, keepdims=True)
    return e / jnp.maximum(l, TINY)


def _bias_of(dist, rb_ref, h):
    out = jnp.full(dist.shape, rb_ref[0, h], F32)
    for b in range(1, NUM_BUCKETS):
        out = jnp.where(dist >= BUCKET_LOWER[b], rb_ref[b, h], out)
    return out


def _bias_kernel(rb_ref, nb_ref, cb_ref, fb_ref, sba_ref, swa_ref, sbb_ref, sc_ref, *, n_qt, nc_p, past, wbuf, nq_s):
    h = pl.program_id(0)
    c = lax.broadcasted_iota(jnp.int32, (2 * TK, TQ), 0)
    qi = lax.broadcasted_iota(jnp.int32, (2 * TK, TQ), 1)
    nb_ref[0] = _bias_of(qi + TK - c, rb_ref, h) * LOG2E
    nn = lax.broadcasted_iota(jnp.int32, (nc_p, TQ), 0)
    qn = lax.broadcasted_iota(jnp.int32, (nc_p, TQ), 1)
    for t in range(n_qt):
        cb_ref[t, 0] = _bias_of(t * TQ + qn - (nn * CMP_STRIDE + CMP_LEN - 1), rb_ref, h) * LOG2E
    fb_ref[0] = jnp.full(fb_ref.shape[1:], rb_ref[NUM_BUCKETS - 1, h] * LOG2E, F32)
    qs = lax.broadcasted_iota(jnp.int32, (nq_s, past), 0)
    ks = lax.broadcasted_iota(jnp.int32, (nq_s, past), 1)
    sba_ref[0] = _bias_of(past + qs - ks, rb_ref, h)
    qs = lax.broadcasted_iota(jnp.int32, (nq_s, wbuf), 0)
    ks = lax.broadcasted_iota(jnp.int32, (nq_s, wbuf), 1)
    swa_ref[0] = _bias_of(wbuf + qs - ks, rb_ref, h)
    qs = lax.broadcasted_iota(jnp.int32, (nq_s, LANES), 0)
    ks = lax.broadcasted_iota(jnp.int32, (nq_s, LANES), 1)
    sbb_ref[0] = _bias_of(qs - ks, rb_ref, h)
    nc_s = sc_ref.shape[2]
    qs = lax.broadcasted_iota(jnp.int32, (nq_s, nc_s), 0)
    ks = lax.broadcasted_iota(jnp.int32, (nq_s, nc_s), 1)
    sc_ref[0] = _bias_of(past + qs - (ks * CMP_STRIDE + CMP_LEN - 1), rb_ref, h)


def _bias_tables(rel_bias, *, seq, past, wbuf, nq_s):
    n_qt = seq // TQ
    nc_p = seq // CMP_STRIDE
    nc_s = past // CMP_STRIDE
    kern = functools.partial(_bias_kernel, n_qt=n_qt, nc_p=nc_p, past=past, wbuf=wbuf, nq_s=nq_s)
    shapes = (
        jax.ShapeDtypeStruct((N_HEADS, 2 * TK, TQ), F32),
        jax.ShapeDtypeStruct((n_qt, N_HEADS, nc_p, TQ), F32),
        jax.ShapeDtypeStruct((N_HEADS, 8, TQ), F32),
        jax.ShapeDtypeStruct((N_HEADS, nq_s, past), F32),
        jax.ShapeDtypeStruct((N_HEADS, nq_s, wbuf), F32),
        jax.ShapeDtypeStruct((N_HEADS, nq_s, LANES), F32),
        jax.ShapeDtypeStruct((N_HEADS, nq_s, nc_s), F32),
    )
    out_specs = (
        pl.BlockSpec((1, 2 * TK, TQ), lambda h: (h, 0, 0)),
        pl.BlockSpec((n_qt, 1, nc_p, TQ), lambda h: (0, h, 0, 0)),
        pl.BlockSpec((1, 8, TQ), lambda h: (h, 0, 0)),
        pl.BlockSpec((1, nq_s, past), lambda h: (h, 0, 0)),
        pl.BlockSpec((1, nq_s, wbuf), lambda h: (h, 0, 0)),
        pl.BlockSpec((1, nq_s, LANES), lambda h: (h, 0, 0)),
        pl.BlockSpec((1, nq_s, nc_s), lambda h: (h, 0, 0)),
    )
    return pl.pallas_call(
        kern, grid=(N_HEADS,),
        in_specs=[pl.BlockSpec(memory_space=pltpu.SMEM)],
        out_specs=out_specs, out_shape=shapes, name="bias_tables",
        compiler_params=pltpu.CompilerParams(dimension_semantics=("arbitrary",)),
    )(rel_bias)


KV_SECTIONS = (SEC_KVC, SEC_KVS, SEC_KVW)
ATTN_KV_SECTIONS = (SEC_KVS, SEC_KVW)


def _in_proj_kernel(x_ref, g_ref, w_ref, *refs, kv_transposed):
    x = x_ref[...]
    u = x * lax.rsqrt(jnp.mean(x * x, axis=-1, keepdims=True) + EPS) * g_ref[...]
    ub = u.astype(BF16)
    if kv_transposed:
        wkv_t_ref, out_refs, attn_refs = refs[0], refs[1:1 + len(SECTIONS)], refs[1 + len(SECTIONS):]
    else:
        out_refs = refs
    for ref, sec in zip(out_refs, SECTIONS):
        if kv_transposed and sec in KV_SECTIONS:
            k = KV_SECTIONS.index(sec)
            kv_t = lax.dot_general(wkv_t_ref[k * KV_DIM:(k + 1) * KV_DIM, :], ub, (((1,), (1,)), ((), ())),
                                   preferred_element_type=F32)
            ref[0] = kv_t
            if sec in ATTN_KV_SECTIONS:
                j = ATTN_KV_SECTIONS.index(sec)
                attn_refs[2 * j][...] = kv_t[0:LANES, :].T.astype(BF16)
                attn_refs[2 * j + 1][0] = kv_t[LANES:2 * LANES, :].astype(BF16)
        else:
            ref[...] = jnp.dot(ub, w_ref[:, sec[0]:sec[1]], preferred_element_type=F32)


def _in_proj(x2d, g, w_pad, wkv_t=None, *, seq=None):
    n = x2d.shape[0]
    tm = min(TM, n)
    kv_transposed = wkv_t is not None
    in_specs = [pl.BlockSpec((tm, D_MODEL), lambda i: (i, 0)),
                pl.BlockSpec((1, D_MODEL), lambda i: (0, 0)),
                pl.BlockSpec((D_MODEL, D_IN_PAD), lambda i: (0, 0))]
    args = [x2d, g, w_pad]
    out_specs, out_shape = [], []
    if kv_transposed:
        in_specs.append(pl.BlockSpec((len(KV_SECTIONS) * KV_DIM, D_MODEL), lambda i: (0, 0)))
        args.append(wkv_t)
        tiles_per_seq = seq // tm
    for sec in SECTIONS:
        w = sec[1] - sec[0]
        if kv_transposed and sec in KV_SECTIONS:
            out_specs.append(pl.BlockSpec((1, w, tm), lambda i: (i // tiles_per_seq, 0, i % tiles_per_seq)))
            out_shape.append(jax.ShapeDtypeStruct((n // seq, w, seq), F32))
        else:
            out_specs.append(pl.BlockSpec((tm, w), lambda i: (i, 0)))
            out_shape.append(jax.ShapeDtypeStruct((n, w), F32))
    if kv_transposed:
        for _ in ATTN_KV_SECTIONS:
            out_specs.append(pl.BlockSpec((tm, LANES), lambda i: (i, 0)))
            out_shape.append(jax.ShapeDtypeStruct((n, LANES), BF16))
            out_specs.append(pl.BlockSpec((1, LANES, tm), lambda i: (i // tiles_per_seq, 0, i % tiles_per_seq)))
            out_shape.append(jax.ShapeDtypeStruct((n // seq, LANES, seq), BF16))
    return pl.pallas_call(
        functools.partial(_in_proj_kernel, kv_transposed=kv_transposed), grid=(n // tm,),
        in_specs=in_specs, out_specs=out_specs, out_shape=out_shape, name="in_proj",
        compiler_params=pltpu.CompilerParams(dimension_semantics=("arbitrary",), vmem_limit_bytes=VMEM_LIMIT),
    )(*args)


HIST_PAD = 32


def _conv_kernel(c3_ref, hist_ref, w_ref, b_ref, lg_ref, lb_ref, out_ref, new_ref, xbuf, sbuf, *, tt, n_t, n_seq):
    t = pl.program_id(1)
    off = HIST_PAD - CONV_HIST

    def load_history(i):
        xbuf[0:HIST_PAD, :] = jnp.zeros((HIST_PAD, D_CONV), F32)
        xbuf[off:HIST_PAD, :] = hist_ref[i]

    def one_sequence(i):
        if n_t == 1:
            load_history(i)
        else:
            pl.when(t == 0)(lambda: load_history(i))
        glu = c3_ref[i, :, 0:D_CONV] * _sigmoid(c3_ref[i, :, D_CONV:2 * D_CONV])
        xbuf[HIST_PAD:HIST_PAD + tt, :] = glu
        acc = jnp.zeros((tt, D_CONV), F32) + b_ref[...]
        for b in range(8):
            a_max = (CONV_WIDTH - 1 - b) // 8
            sbuf[0:tt + 8 * a_max, :] = xbuf[off + b:off + b + tt + 8 * a_max, :]
            for a in range(a_max + 1):
                k = 8 * a + b
                acc = acc + sbuf[8 * a:8 * a + tt, :] * w_ref[k:k + 1, :]
        mu = jnp.mean(acc, axis=-1, keepdims=True)
        xc = acc - mu
        var = jnp.mean(xc * xc, axis=-1, keepdims=True)
        y = xc * lax.rsqrt(var + EPS) * lg_ref[...] + lb_ref[...]
        out_ref[i] = _silu(y) * _silu(c3_ref[i, :, 2 * D_CONV:3 * D_CONV])
        tail = xbuf[off + tt:HIST_PAD + tt, :]
        if n_t == 1:
            new_ref[i] = tail
        else:
            @pl.when(t == n_t - 1)
            def _():
                new_ref[i] = tail

            @pl.when(t < n_t - 1)
            def _():
                xbuf[off:HIST_PAD, :] = tail

    if n_seq == 1:
        one_sequence(0)
    else:
        def body(i, carry):
            one_sequence(i)
            return carry

        lax.fori_loop(0, n_seq, body, 0)


def _conv_module(c3, hist, conv_w, conv_b, ln_g, ln_b):
    bsz, t_len, _ = c3.shape
    tt = min(256, t_len)
    n_t = t_len // tt
    n_seq = 1 if n_t > 1 else math.gcd(bsz, 16)
    kern = functools.partial(_conv_kernel, tt=tt, n_t=n_t, n_seq=n_seq)
    vec = pl.BlockSpec((1, D_CONV), lambda b, t: (0, 0))
    return pl.pallas_call(
        kern, grid=(bsz // n_seq, n_t),
        in_specs=[pl.BlockSpec((n_seq, tt, 3 * D_CONV), lambda b, t: (b, t, 0)),
                  pl.BlockSpec((n_seq, CONV_HIST, D_CONV), lambda b, t: (b, 0, 0)),
                  pl.BlockSpec((CONV_WIDTH, D_CONV), lambda b, t: (0, 0)),
                  vec, vec, vec],
        out_specs=[pl.BlockSpec((n_seq, tt, D_CONV), lambda b, t: (b, t, 0)),
                   pl.BlockSpec((n_seq, CONV_HIST, D_CONV), lambda b, t: (b, 0, 0))],
        out_shape=[jax.ShapeDtypeStruct((bsz, t_len, D_CONV), F32),
                   jax.ShapeDtypeStruct((bsz, CONV_HIST, D_CONV), F32)],
        scratch_shapes=[pltpu.VMEM((HIST_PAD + tt, D_CONV), F32), pltpu.VMEM((HIST_PAD + tt, D_CONV), F32)],
        name="conv_module",
        compiler_params=pltpu.CompilerParams(dimension_semantics=("arbitrary", "arbitrary")),
    )(c3, hist, conv_w, conv_b, ln_g, ln_b)


HALVES_PER_PAGE = PAGE_SIZE // CMP_STRIDE
N_TG = 2 * KV_HEADS
N_LT = KV_DIM // LANES
CMP_FEAT = CMP_STRIDE * HEAD_DIM
X_PITCH = 24
SEQS_PER_STEP = 4


def _compress_body(page_tile, n_pages, pe_ref, w1_ref, w2_ref, out_ref, x_scr, y_scr, h_scr):
    n_half = n_pages * HALVES_PER_PAGE
    n_seq = out_ref.shape[0]
    n_rows = n_seq * n_half
    for i in range(n_seq):
        for p in range(n_pages):
            for t in range(N_LT):
                tile = page_tile(i, p, t).T
                for n in range(HALVES_PER_PAGE):
                    dst = (p * HALVES_PER_PAGE + n) * X_PITCH
                    x_scr[N_LT * i + t, dst:dst + CMP_STRIDE, :] = tile[n * CMP_STRIDE:(n + 1) * CMP_STRIDE, :]
        rows = slice(i * n_half, (i + 1) * n_half)
        for s in range(CMP_STRIDE):
            for t in range(N_LT):
                y_scr[t, rows, s * LANES:(s + 1) * LANES] = (
                    x_scr[N_LT * i + t, pl.ds(s, n_half, stride=X_PITCH), :].astype(BF16))
    two_h = 2 * CMP_HIDDEN
    h_scr[n_rows:n_rows + 8, :] = jnp.zeros((8, KV_HEADS * two_h), F32)
    acts = []
    for t in range(N_LT):
        w1 = w1_ref[t]
        c = jnp.dot(pe_ref[t].astype(BF16), w1, preferred_element_type=F32)
        h_scr[0:n_rows, :] = jnp.dot(y_scr[t], w1, preferred_element_type=F32)
        for g in range(KV_HEADS):
            a0, b0 = g * two_h, g * two_h + CMP_HIDDEN
            cvec = c[0:1, a0:a0 + CMP_HIDDEN] + c[1:2, b0:b0 + CMP_HIDDEN]
            hid = h_scr[0:n_rows, a0:a0 + CMP_HIDDEN] + h_scr[1:n_rows + 1, b0:b0 + CMP_HIDDEN] + cvec
            acts.append(_silu(hid))
    out = jnp.dot(jnp.concatenate(acts, axis=1).astype(BF16), w2_ref[...], preferred_element_type=F32)
    row = lax.broadcasted_iota(jnp.int32, out.shape, 0) & (n_half - 1)
    out_ref[...] = jnp.where(row < n_half - 1, out, 0.0).reshape(n_seq, n_half, KV_DIM)


def _compress_paged_kernel(pt_ref, *refs, n_pages, n_seq):
    del pt_ref
    pages = refs[:n_seq * n_pages]

    def page_tile(i, p, t):
        return pages[i * n_pages + p][0, 0, t].reshape(LANES, PAGE_SIZE)

    _compress_body(page_tile, n_pages, *refs[n_seq * n_pages:])


def _compress_seq_kernel(kv_ref, *refs, n_pages):
    def page_tile(i, p, t):
        return kv_ref[i, t * LANES:(t + 1) * LANES, p * PAGE_SIZE:(p + 1) * PAGE_SIZE]

    _compress_body(page_tile, n_pages, *refs)


def _compress_specs(n_half, n_seq):
    assert n_half & (n_half - 1) == 0
    const = lambda shape: pl.BlockSpec(shape, lambda *a: (0,) * len(shape))
    feat = KV_HEADS * CMP_FEAT
    hid = KV_HEADS * 2 * CMP_HIDDEN
    weight_specs = [const((2, 8, feat)), const((2, feat, hid)), const((N_TG * CMP_HIDDEN, KV_DIM))]
    scratch = [pltpu.VMEM((n_seq * N_LT, n_half * X_PITCH, LANES), F32),
               pltpu.VMEM((N_LT, n_seq * n_half, feat), BF16),
               pltpu.VMEM((n_seq * n_half + 8, hid), F32)]
    return weight_specs, scratch


def _compress_paged(cache_t, layer, table, pe_pad, w1cat, w2big):
    bsz, n_pages = table.shape
    n_half = n_pages * HALVES_PER_PAGE
    n_seq = math.gcd(bsz, SEQS_PER_STEP)
    page_specs = [pl.BlockSpec((1, 1, 2, KV_HEADS, HEAD_DIM, PAGE_SIZE),
                               lambda b, pt, i=i, p=p: (layer, pt[b * n_seq + i, p], 0, 0, 0, 0))
                  for i in range(n_seq) for p in range(n_pages)]
    weight_specs, scratch = _compress_specs(n_half, n_seq)
    out_spec = pl.BlockSpec((n_seq, n_half, KV_DIM), lambda b, pt: (b, 0, 0))
    grid_spec = pltpu.PrefetchScalarGridSpec(
        num_scalar_prefetch=1, grid=(bsz // n_seq,), in_specs=page_specs + weight_specs, out_specs=out_spec,
        scratch_shapes=scratch)
    return pl.pallas_call(
        functools.partial(_compress_paged_kernel, n_pages=n_pages, n_seq=n_seq), grid_spec=grid_spec,
        out_shape=jax.ShapeDtypeStruct((bsz, n_half, KV_DIM), F32), name="compress_paged",
        compiler_params=pltpu.CompilerParams(dimension_semantics=("arbitrary",), vmem_limit_bytes=VMEM_LIMIT),
    )(table, *([cache_t] * (n_seq * n_pages)), pe_pad, w1cat, w2big)


def _compress_seq(kv_t, pe_pad, w1cat, w2big):
    bsz, _, seq = kv_t.shape
    n_pages = seq // PAGE_SIZE
    n_half = n_pages * HALVES_PER_PAGE
    weight_specs, scratch = _compress_specs(n_half, 1)
    out_spec = pl.BlockSpec((1, n_half, KV_DIM), lambda b: (b, 0, 0))
    return pl.pallas_call(
        functools.partial(_compress_seq_kernel, n_pages=n_pages), grid=(bsz,),
        in_specs=[pl.BlockSpec((1, KV_DIM, seq), lambda b: (b, 0, 0))] + weight_specs, out_specs=out_spec,
        scratch_shapes=scratch,
        out_shape=jax.ShapeDtypeStruct((bsz, n_half, KV_DIM), F32), name="compress_seq",
        compiler_params=pltpu.CompilerParams(dimension_semantics=("arbitrary",), vmem_limit_bytes=VMEM_LIMIT),
    )(kv_t, pe_pad, w1cat, w2big)


def _head_rows(q, g, low):
    parts = []
    for r in range(GROUP):
        h = GROUP * g + r
        tile = q[:, LANES * (h // 2):LANES * (h // 2 + 1)]
        if (h % 2) != g:
            tile = pltpu.roll(tile, HALF, 1)
        parts.append(jnp.where(low, tile, 0.0) if g == 0 else jnp.where(low, 0.0, tile))
    return parts


def _assemble_heads(o_heads, low):
    tiles = []
    for j in range(N_HEADS // 2):
        a, b = o_heads[2 * j], o_heads[2 * j + 1]
        if (2 * j) // GROUP == 0:
            tiles.append(jnp.where(low, a, pltpu.roll(b, HALF, 1)))
        else:
            tiles.append(jnp.where(low, pltpu.roll(a, HALF, 1), b))
    return jnp.concatenate(tiles, axis=1)


def _select_blocks_t(score_t, allowed_t, n_blk, top_n):
    idx = lax.broadcasted_iota(jnp.int32, score_t.shape, 0)
    cnt = jnp.zeros(score_t.shape, jnp.int32)
    for i in range(n_blk):
        row = score_t[i:i + 1, :]
        ahead = (row > score_t) | ((row == score_t) & (idx > i))
        cnt = cnt + ahead.astype(jnp.int32)
    return (cnt < top_n) & allowed_t


def _flash_step(q_t, k_tile, v_tile, bias, mask, state):
    m, l, acc = state
    s = jnp.dot(k_tile, q_t, preferred_element_type=F32)
    mask_add = jnp.where(mask, 0.0, NEG_INF)
    ms, ls, ps, alphas = [], [], [], []
    for r in range(GROUP):
        cols = slice(r * TQ, (r + 1) * TQ)
        s_r = s[:, cols] + bias(r) + mask_add
        m_r = jnp.maximum(m[:, cols], jnp.max(s_r, axis=0, keepdims=True))
        m_use = jnp.where(m_r > 0.5 * NEG_INF, m_r, 0.0)
        alpha = jnp.exp2(m[:, cols] - m_use)
        p_r = jnp.exp2(s_r - m_use)
        ls.append(alpha * l[:, cols] + jnp.sum(p_r, axis=0, keepdims=True))
        ms.append(m_r)
        alphas.append(alpha)
        ps.append(p_r.astype(BF16))
    pv = jnp.dot(v_tile, jnp.concatenate(ps, axis=1), preferred_element_type=F32)
    acc = jnp.concatenate(alphas, axis=1) * acc + pv
    return jnp.concatenate(ms, axis=1), jnp.concatenate(ls, axis=1), acc


def _flash_init():
    return (jnp.full((1, GROUP * TQ), NEG_INF, F32), jnp.zeros((1, GROUP * TQ), F32),
            jnp.zeros((LANES, GROUP * TQ), F32))


def _flash_out(state):
    _, l, acc = state
    return acc * (1.0 / jnp.maximum(l, TINY))


def _attn_prompt_kernel(q_ref, gate_ref, z_ref, kcvc_ref, ks_ref, vs_ref, kw_ref, vw_ref, nb_ref, cb_ref, fb_ref,
                        cover_ref, out_ref, *, n_slc):
    qt = pl.program_id(1)
    nc = kcvc_ref.shape[1]
    ki = lax.broadcasted_iota(jnp.int32, (TK, TQ), 0)
    qi = lax.broadcasted_iota(jnp.int32, (TK, TQ), 1)
    causal = qi >= ki
    q_t = q_ref[...].T
    g_t = _sigmoid(gate_ref[...]).T
    kc = kcvc_ref[0, :, 0:LANES].astype(BF16)
    vc_t = kcvc_ref[0, :, LANES:2 * LANES].T.astype(BF16)
    c_end = lax.broadcasted_iota(jnp.int32, (nc, TQ), 0) * CMP_STRIDE + (CMP_LEN - 1)
    cmask = qt * TQ + lax.broadcasted_iota(jnp.int32, (nc, TQ), 1) >= c_end
    blk_t = lax.broadcasted_iota(jnp.int32, (n_slc, TQ), 0)
    cur_t = jnp.right_shift(qt * TQ + lax.broadcasted_iota(jnp.int32, (n_slc, TQ), 1), SLC_SHIFT)
    allowed_t = blk_t <= cur_t
    forced_t = (blk_t == 0) | (blk_t == cur_t) | (blk_t == cur_t - 1)
    e_row = jnp.right_shift(lax.broadcasted_iota(jnp.int32, (TK, LANES), 0), SLC_SHIFT)
    e_col = lax.broadcasted_iota(jnp.int32, (TK, LANES), 1)
    kt_prev = jnp.maximum(qt - 1, 0)
    kt_far = jnp.maximum(qt - 2, 0)
    zero_half = jnp.zeros((HEAD_DIM, TQ), F32)

    def kv_tile(k_ref, v_ref, kt):
        start = pl.multiple_of(kt * TK, TK)
        return k_ref[pl.ds(start, TK), :], v_ref[0, :, pl.ds(start, TK)]

    pieces = []
    for g in range(KV_HEADS):
        heads = [GROUP * g + r for r in range(GROUP)]
        parts = []
        for h in heads:
            x = q_t[h * HEAD_DIM:(h + 1) * HEAD_DIM, :] * (SCALE * LOG2E)
            parts.append(jnp.concatenate([x, zero_half] if g == 0 else [zero_half, x], axis=0))
        qg = jnp.concatenate(parts, axis=1).astype(BF16)
        s_c = jnp.dot(kc, qg, preferred_element_type=F32)
        p_parts, p_sum = [], None
        for r, h in enumerate(heads):
            s_r = jnp.where(cmask, s_c[:, r * TQ:(r + 1) * TQ] + cb_ref[0, h], NEG_INF)
            e = jnp.where(cmask, jnp.exp2(s_r - jnp.max(s_r, axis=0, keepdims=True)), 0.0)
            p_r = e * (1.0 / jnp.maximum(jnp.sum(e, axis=0, keepdims=True), TINY))
            p_sum = p_r if p_sum is None else p_sum + p_r
            p_parts.append(p_r.astype(BF16))
        o_cmp = jnp.dot(vc_t, jnp.concatenate(p_parts, axis=1), preferred_element_type=F32)
        hi = p_sum.astype(BF16)
        lo = (p_sum - hi.astype(F32)).astype(BF16)
        imp_t = (jnp.dot(cover_ref[...], hi, preferred_element_type=F32)
                 + jnp.dot(cover_ref[...], lo, preferred_element_type=F32))
        score_t = jnp.where(allowed_t, jnp.where(forced_t, FORCED_SCORE, imp_t[0:n_slc]), MASKED_SCORE)
        sel_t = _select_blocks_t(score_t, allowed_t, n_slc, min(TOP_N, n_slc)).astype(F32)
        sel_pad = jnp.concatenate([sel_t, jnp.zeros((LANES - n_slc, TQ), F32)], axis=0).astype(BF16)

        def sel_mask(kt):
            expand = (e_row + kt * (TK // SLC_BLOCK) == e_col).astype(BF16)
            return jnp.dot(expand, sel_pad, preferred_element_type=F32) > 0.5

        far_bias = lambda r: fb_ref[heads[r], 0:1, :]
        prev_bias = lambda r: nb_ref[heads[r], 0:TK, :]
        diag_bias = lambda r: nb_ref[heads[r], TK:2 * TK, :]

        def far_body(kt, state):
            k_t, v_t = kv_tile(ks_ref, vs_ref, kt)
            return _flash_step(qg, k_t, v_t, far_bias, sel_mask(kt), state)

        state = lax.fori_loop(0, jnp.maximum(qt - 1, 0), far_body, _flash_init())
        k_t, v_t = kv_tile(ks_ref, vs_ref, kt_prev)
        state = _flash_step(qg, k_t, v_t, prev_bias, sel_mask(kt_prev) & (qt >= 1), state)
        k_t, v_t = kv_tile(ks_ref, vs_ref, qt)
        state = _flash_step(qg, k_t, v_t, diag_bias, sel_mask(qt) & causal, state)
        o_slc = _flash_out(state)
        state = _flash_init()
        k_t, v_t = kv_tile(kw_ref, vw_ref, kt_far)
        state = _flash_step(qg, k_t, v_t, far_bias, (ki > qi) & (qt >= 2), state)
        k_t, v_t = kv_tile(kw_ref, vw_ref, kt_prev)
        state = _flash_step(qg, k_t, v_t, prev_bias, (ki >= 0) & (qt >= 1), state)
        k_t, v_t = kv_tile(kw_ref, vw_ref, qt)
        state = _flash_step(qg, k_t, v_t, diag_bias, causal, state)
        o_win = _flash_out(state)
        rows = slice(g * HEAD_DIM, (g + 1) * HEAD_DIM)
        for r, h in enumerate(heads):
            cols = slice(r * TQ, (r + 1) * TQ)
            pieces.append(g_t[h:h + 1, :] * o_cmp[rows, cols]
                          + g_t[N_HEADS + h:N_HEADS + h + 1, :] * o_slc[rows, cols]
                          + g_t[2 * N_HEADS + h:2 * N_HEADS + h + 1, :] * o_win[rows, cols])
    out_ref[...] = jnp.concatenate(pieces, axis=0).T * _silu(z_ref[...])


def _attn_prompt(q2d, gate2d, z2d, kcvc, ks, vs_t, kw, vw_t, nb, cb, fb, cover_t, *, bsz, seq):
    assert WINDOW == 2 * TK and seq % TQ == 0
    n_qt = seq // TQ
    nc = seq // CMP_STRIDE
    n_slc = seq // SLC_BLOCK
    kern = functools.partial(_attn_prompt_kernel, n_slc=n_slc)
    tok = lambda w: pl.BlockSpec((TQ, w), lambda b, t: (b * n_qt + t, 0))
    k_spec = pl.BlockSpec((seq, LANES), lambda b, t: (b, 0))
    v_spec = pl.BlockSpec((1, LANES, seq), lambda b, t: (b, 0, 0))
    return pl.pallas_call(
        kern, grid=(bsz, n_qt),
        in_specs=[tok(D_ATTN), tok(LANES), tok(D_ATTN),
                  pl.BlockSpec((1, nc, KV_DIM), lambda b, t: (b, 0, 0)), k_spec, v_spec, k_spec, v_spec,
                  pl.BlockSpec((N_HEADS, 2 * TK, TQ), lambda b, t: (0, 0, 0)),
                  pl.BlockSpec((1, N_HEADS, nc, TQ), lambda b, t: (t, 0, 0, 0)),
                  pl.BlockSpec((N_HEADS, 8, TQ), lambda b, t: (0, 0, 0)),
                  pl.BlockSpec((LANES, nc), lambda b, t: (0, 0))],
        out_specs=tok(D_ATTN),
        out_shape=jax.ShapeDtypeStruct((bsz * seq, D_ATTN), F32),
        name="attn_prompt",
        compiler_params=pltpu.CompilerParams(dimension_semantics=("arbitrary", "arbitrary"),
                                             vmem_limit_bytes=VMEM_LIMIT),
    )(q2d, gate2d, z2d, kcvc, ks, vs_t, kw, vw_t, nb, cb, fb, cover_t)


def _attn_sample_kernel(pt_ref, *refs, n_pages, nq, n_seq, aliased_state):
    del pt_ref
    shared = list(refs[n_seq * n_pages:])
    if aliased_state:
        del shared[-3]
    for i in range(n_seq):
        _attn_sample_one(i, refs[i * n_pages:(i + 1) * n_pages], *shared, n_pages=n_pages, nq=nq)


def _attn_sample_one(i, pages, win_ref, kvsn_ref, kvwn_ref, q_ref, gate_ref, z_ref, kcvc_ref, sba_ref, swa_ref,
                     sbb_ref, sc_ref, cover_ref, expand_ref, out_ref, nwin_ref, *, n_pages, nq):
    past = n_pages * PAGE_SIZE
    wbuf = win_ref.shape[-1]
    nc = kcvc_ref.shape[1]
    rows = KV_HEADS * GROUP * nq
    cur = past // SLC_BLOCK
    n_slc = cur + 1
    low = lax.broadcasted_iota(jnp.int32, (nq, LANES), 1) < HALF
    q = q_ref[i]
    q_left = (jnp.concatenate(_head_rows(q, 0, low) + _head_rows(q, 1, low), axis=0) * SCALE).astype(BF16)
    qi = lax.broadcasted_iota(jnp.int32, (rows, LANES), 0) & (nq - 1)
    ki = lax.broadcasted_iota(jnp.int32, (rows, LANES), 1)
    new_mask = (ki <= qi) & (ki < nq)
    pad_rows = jnp.zeros((LANES - nq, LANES), F32)

    def new_tile(ref, t):
        return jnp.concatenate([ref[i, :, t * LANES:(t + 1) * LANES], pad_rows], axis=0)

    kc = kcvc_ref[i, :, 0:LANES].astype(BF16)
    vc = kcvc_ref[i, :, LANES:2 * LANES].astype(BF16)
    n_idx = lax.broadcasted_iota(jnp.int32, (rows, nc), 1)
    p_c = _masked_softmax(_dot_nt(q_left, kc) + sc_ref[...], n_idx < nc - 1)
    o_cmp = _dot(p_c, vc)
    blk = lax.broadcasted_iota(jnp.int32, (nq, LANES), 1)
    is_blk = blk < n_slc
    forced = (blk == 0) | (blk == cur) | (blk == cur - 1)
    sel_rows = []
    for g in range(KV_HEADS):
        p_sum = p_c[g * GROUP * nq:g * GROUP * nq + nq]
        for r in range(1, GROUP):
            p_sum = p_sum + p_c[(g * GROUP + r) * nq:(g * GROUP + r + 1) * nq]
        imp = _dot_split(p_sum, cover_ref[...])
        score = jnp.where(is_blk, jnp.where(forced, FORCED_SCORE, imp), PAD_SCORE)
        cnt = jnp.zeros((nq, LANES), jnp.int32)
        for j in range(n_slc):
            col = score[:, j:j + 1]
            cnt = cnt + ((col > score) | ((col == score) & (blk > j))).astype(jnp.int32)
        sel_g = ((cnt < min(TOP_N, n_slc)) & is_blk).astype(F32)
        sel_rows += [sel_g] * GROUP
    sel = jnp.concatenate(sel_rows, axis=0)
    k_pages = [pages[p][0, 0, 0].reshape(LANES, PAGE_SIZE).astype(BF16) for p in range(n_pages)]
    v_pages = [pages[p][0, 0, 1].reshape(LANES, PAGE_SIZE).astype(BF16) for p in range(n_pages)]
    k_new = new_tile(kvsn_ref, 0).astype(BF16)
    v_new = new_tile(kvsn_ref, 1).astype(BF16)
    s_a = jnp.concatenate([_dot(q_left, k) for k in k_pages], axis=1) + sba_ref[...]
    mask_a = jnp.dot(sel.astype(BF16), expand_ref[...], preferred_element_type=F32) > 0.5
    s_b = _dot_nt(q_left, k_new) + sbb_ref[...]
    mask_b = new_mask & (sel[:, cur:cur + 1] > 0.5)
    s_a = jnp.where(mask_a, s_a, NEG_INF)
    s_b = jnp.where(mask_b, s_b, NEG_INF)
    m = jnp.maximum(jnp.max(s_a, axis=-1, keepdims=True), jnp.max(s_b, axis=-1, keepdims=True))
    p_a = jnp.exp(s_a - m) * mask_a.astype(F32)
    p_b = jnp.exp(s_b - m) * mask_b.astype(F32)
    l = jnp.sum(p_a, axis=-1, keepdims=True) + jnp.sum(p_b, axis=-1, keepdims=True)
    acc = _dot(p_b, v_new)
    for p in range(n_pages):
        acc = acc + _dot_nt(p_a[:, p * PAGE_SIZE:(p + 1) * PAGE_SIZE], v_pages[p])
    o_slc = acc / jnp.maximum(l, TINY)
    win_t = [win_ref[0, i, t].reshape(LANES, wbuf) for t in range(2)]
    new_w = [new_tile(kvwn_ref, t) for t in range(2)]
    lane_w = lax.broadcasted_iota(jnp.int32, (LANES, wbuf), 1)
    for t in range(2):
        placed = jnp.concatenate([jnp.zeros((LANES, wbuf - LANES), F32), pltpu.roll(new_w[t].T, LANES - nq, 1)],
                                 axis=1)
        shifted = pltpu.roll(win_t[t], wbuf - nq, 1)
        nwin_ref[0, i, t] = jnp.where(lane_w < wbuf - nq, shifted, placed).reshape(KV_HEADS, HEAD_DIM, wbuf)
    kw_t, vw_t = win_t[0].astype(BF16), win_t[1].astype(BF16)
    kw_new, vw_new = new_w[0].astype(BF16), new_w[1].astype(BF16)
    jw = lax.broadcasted_iota(jnp.int32, (rows, wbuf), 1)
    qw = lax.broadcasted_iota(jnp.int32, (rows, wbuf), 0) & (nq - 1)
    mask_wa = jw > qw
    s_wa = jnp.where(mask_wa, _dot(q_left, kw_t) + swa_ref[...], NEG_INF)
    s_wb = jnp.where(new_mask, _dot_nt(q_left, kw_new) + sbb_ref[...], NEG_INF)
    m = jnp.maximum(jnp.max(s_wa, axis=-1, keepdims=True), jnp.max(s_wb, axis=-1, keepdims=True))
    p_wa = jnp.exp(s_wa - m) * mask_wa.astype(F32)
    p_wb = jnp.exp(s_wb - m) * new_mask.astype(F32)
    l = jnp.sum(p_wa, axis=-1, keepdims=True) + jnp.sum(p_wb, axis=-1, keepdims=True)
    acc = _dot_nt(p_wa, vw_t) + _dot(p_wb, vw_new)
    o_win = acc / jnp.maximum(l, TINY)
    gsig = _sigmoid(gate_ref[i])
    o_heads = []
    for h in range(N_HEADS):
        rs = slice(h * nq, (h + 1) * nq)
        o_heads.append(gsig[:, h:h + 1] * o_cmp[rs] + gsig[:, N_HEADS + h:N_HEADS + h + 1] * o_slc[rs]
                       + gsig[:, 2 * N_HEADS + h:2 * N_HEADS + h + 1] * o_win[rs])
    out_ref[i] = _assemble_heads(o_heads, low) * _silu(z_ref[i])


def _attn_sample(cache_t, win_t, layer, table, kvs_new, kvw_new, q, gate, z, kcvc, sba, swa, sbb, sc, cover, expand,
                 new_state=None):
    bsz, n_pages = table.shape
    nq = q.shape[1]
    wbuf = win_t.shape[-1]
    nc = kcvc.shape[1]
    past = n_pages * PAGE_SIZE
    rows = N_HEADS * nq
    assert nq <= SLC_BLOCK and nq & (nq - 1) == 0 and past % SLC_BLOCK == 0 and wbuf == WINDOW
    n_seq = math.gcd(bsz, SEQS_PER_STEP)
    aliased_state = new_state is not None
    kern = functools.partial(_attn_sample_kernel, n_pages=n_pages, nq=nq, n_seq=n_seq, aliased_state=aliased_state)
    page_specs = [pl.BlockSpec((1, 1, 2, KV_HEADS, HEAD_DIM, PAGE_SIZE),
                               lambda b, pt, i=i, p=p: (layer, pt[b * n_seq + i, p], 0, 0, 0, 0))
                  for i in range(n_seq) for p in range(n_pages)]
    per_b = lambda r, w: pl.BlockSpec((n_seq, r, w), lambda b, pt: (b, 0, 0))
    const = lambda r, w: pl.BlockSpec((r, w), lambda b, pt: (0, 0))
    state_spec = pl.BlockSpec((1, n_seq, 2, KV_HEADS, HEAD_DIM, wbuf), lambda b, pt: (layer, b, 0, 0, 0, 0))
    in_specs = page_specs + [
        state_spec, per_b(nq, KV_DIM), per_b(nq, KV_DIM), per_b(nq, D_ATTN), per_b(nq, LANES),
        per_b(nq, D_ATTN), per_b(nc, KV_DIM),
        const(rows, past), const(rows, wbuf), const(rows, LANES), const(rows, nc),
        const(nc, LANES), const(LANES, past)]
    args = [table, *([cache_t] * (n_seq * n_pages)), win_t, kvs_new, kvw_new, q, gate, z, kcvc, sba, swa, sbb, sc,
            cover, expand]
    aliases = {}
    if aliased_state:
        in_specs.append(pl.BlockSpec(memory_space=pl.ANY))
        aliases = {len(args): 1}
        args.append(new_state)
    grid_spec = pltpu.PrefetchScalarGridSpec(
        num_scalar_prefetch=1, grid=(bsz // n_seq,), in_specs=in_specs,
        out_specs=[per_b(nq, D_ATTN), state_spec])
    return pl.pallas_call(
        kern, grid_spec=grid_spec,
        out_shape=[jax.ShapeDtypeStruct((bsz, nq, D_ATTN), F32),
                   jax.ShapeDtypeStruct(win_t.shape, F32)],
        input_output_aliases=aliases, name="attn_sample",
        compiler_params=pltpu.CompilerParams(dimension_semantics=("arbitrary",), vmem_limit_bytes=VMEM_LIMIT),
    )(*args)


def _out_proj_kernel(h_ref, conv_ref, attn_ref, ple_ref, wo_ref, wg_ref, wp_ref, fg_ref, out_ref, *, final):
    h = h_ref[...]
    h = h + jnp.dot(conv_ref[...].astype(BF16), wo_ref[0:D_CONV, :], preferred_element_type=F32)
    h = h + jnp.dot(attn_ref[...].astype(BF16), wo_ref[D_CONV:D_CONV + D_ATTN, :], preferred_element_type=F32)
    gate = _sigmoid(jnp.dot(h.astype(BF16), wg_ref[...], preferred_element_type=F32))
    h = h + gate * jnp.dot(ple_ref[0].astype(BF16), wp_ref[...], preferred_element_type=F32)
    if final:
        h = h * lax.rsqrt(jnp.mean(h * h, axis=-1, keepdims=True) + EPS) * fg_ref[...]
    out_ref[...] = h


def _out_proj(h2d, conv2d, attn2d, ple3d, layer, wo, wg, wp, fg, *, final):
    n = h2d.shape[0]
    tm = min(TM, n)
    ple_dim = ple3d.shape[-1]
    kern = functools.partial(_out_proj_kernel, final=final)
    tok = lambda w: pl.BlockSpec((tm, w), lambda i: (i, 0))
    const = lambda r, w: pl.BlockSpec((r, w), lambda i: (0, 0))
    return pl.pallas_call(
        kern, grid=(n // tm,),
        in_specs=[tok(D_MODEL), tok(D_CONV), tok(D_ATTN), pl.BlockSpec((1, tm, ple_dim), lambda i: (layer, i, 0)),
                  const(D_CONV + D_ATTN, D_MODEL), const(D_MODEL, D_MODEL), const(ple_dim, D_MODEL),
                  const(1, D_MODEL)],
        out_specs=tok(D_MODEL),
        out_shape=jax.ShapeDtypeStruct((n, D_MODEL), F32),
        name="out_proj",
        compiler_params=pltpu.CompilerParams(dimension_semantics=("arbitrary",), vmem_limit_bytes=VMEM_LIMIT),
    )(h2d, conv2d, attn2d, ple3d, wo, wg, wp, fg)


def _cover_matrix(n_cmp_rows, n_cmp, n_slc):
    c_start = np.arange(n_cmp_rows) * CMP_STRIDE
    c_end = c_start + CMP_LEN - 1
    s_start = np.arange(LANES) * SLC_BLOCK
    cover = (c_start[:, None] < s_start[None, :] + SLC_BLOCK) & (c_end[:, None] >= s_start[None, :])
    cover &= (np.arange(n_cmp_rows)[:, None] < n_cmp) & (np.arange(LANES)[None, :] < n_slc)
    return jnp.asarray(cover, dtype=BF16)


def _expand_matrix(past):
    e = np.arange(LANES)[:, None] == (np.arange(past)[None, :] // SLC_BLOCK)
    return jnp.asarray(e, dtype=BF16)


def kernel(x_prompt, x_sample, cache_cmp_kv, cache_slc_kv, page_table, state_win_kv, state_conv, p_prompt, p_sample, norm_g, w_in, conv_w, conv_b, conv_ln_g, conv_ln_b, cmp_pe, cmp_w1, cmp_w2, w_out, w_ple, w_ple_gate, rel_bias, final_norm_g):
    bp, seq, _ = x_prompt.shape
    bs, nq, _ = x_sample.shape
    depth = w_in.shape[0]
    n_pages = page_table.shape[1]
    past = n_pages * PAGE_SIZE
    wbuf = state_win_kv.shape[2]
    n_pool = cache_cmp_kv.shape[1]
    win_p = min(WINDOW, seq)

    nb, cb, fb, sba, swa, sbb, sc = _bias_tables(rel_bias, seq=seq, past=past, wbuf=wbuf, nq_s=nq)
    rows = N_HEADS * nq
    sba, swa, sbb, sc = (a.reshape(rows, a.shape[-1]) for a in (sba, swa, sbb, sc))
    nc_p, nc_s = seq // CMP_STRIDE, past // CMP_STRIDE
    cover_p = _cover_matrix(nc_p, nc_p - 1, seq // SLC_BLOCK)
    cover_s = _cover_matrix(nc_s, nc_s - 1, past // SLC_BLOCK + 1)
    expand_s = _expand_matrix(past)
    conv_zero = jnp.zeros((bp, CONV_HIST, D_CONV), F32)
    fg = final_norm_g.reshape(1, D_MODEL)
    to_t = lambda a: jnp.transpose(a, (0, 1, 3, 4, 5, 2))
    from_t = lambda a: jnp.transpose(a, (0, 1, 5, 2, 3, 4))
    cmp_t, slc_t, win_t = to_t(cache_cmp_kv), to_t(cache_slc_kv), to_t(state_win_kv)
    ple_p = p_prompt.reshape(depth, bp * seq, -1)
    ple_s = p_sample.reshape(depth, bs * nq, -1)

    hp = x_prompt.reshape(bp * seq, D_MODEL)
    hs = x_sample.reshape(bs * nq, D_MODEL)
    outs = [[] for _ in range(8)]
    new_win = None
    for i in range(depth):
        w_pad = jnp.pad(w_in[i], ((0, 0), (0, D_IN_PAD - D_IN))).astype(BF16)
        wkv_t = w_in[i][:, SEC_KVC[0]:SEC_KVW[1]].T.astype(BF16)
        g = norm_g[i].reshape(1, D_MODEL)
        w1 = jnp.transpose(cmp_w1[i].reshape(2, 2, CMP_STRIDE, HEAD_DIM, CMP_HIDDEN), (0, 2, 3, 1, 4))
        w1 = w1.reshape(2, CMP_STRIDE, HEAD_DIM, 2 * CMP_HIDDEN)
        w1cat = jnp.stack([jnp.concatenate([w1, jnp.zeros_like(w1)], axis=-1),
                           jnp.concatenate([jnp.zeros_like(w1), w1], axis=-1)], axis=2)
        w1cat = w1cat.reshape(2, KV_HEADS * CMP_FEAT, KV_HEADS * 2 * CMP_HIDDEN).astype(BF16)
        pe = cmp_pe[i].reshape(2, 2, CMP_STRIDE, 1, HEAD_DIM)
        pe = jnp.broadcast_to(pe, (2, 2, CMP_STRIDE, KV_HEADS, HEAD_DIM)).reshape(2, 2, KV_HEADS * CMP_FEAT)
        pe_pad = jnp.pad(pe, ((0, 0), (0, 6), (0, 0)))
        w2big = jnp.zeros((N_TG * CMP_HIDDEN, KV_DIM), F32)
        for tg in range(N_TG):
            w2big = w2big.at[tg * CMP_HIDDEN:(tg + 1) * CMP_HIDDEN, tg * HEAD_DIM:(tg + 1) * HEAD_DIM].set(
                cmp_w2[i, tg // KV_HEADS])
        w2big = w2big.astype(BF16)
        wo, wg, wp = w_out[i].astype(BF16), w_ple_gate[i].astype(BF16), w_ple[i].astype(BF16)
        cw, cbias = conv_w[i], conv_b[i].reshape(1, D_CONV)
        lg, lb = conv_ln_g[i].reshape(1, D_CONV), conv_ln_b[i].reshape(1, D_CONV)
        final = i == depth - 1

        c3, q, kvc_t, kvs_t, kvw_t, z, gate, ks, vs_t, kw, vw_t = _in_proj(hp, g, w_pad, wkv_t, seq=seq)
        conv_out, new_conv = _conv_module(c3.reshape(bp, seq, 3 * D_CONV), conv_zero, cw, cbias, lg, lb)
        kcvc = _compress_seq(kvc_t, pe_pad, w1cat, w2big)
        attn = _attn_prompt(q, gate, z, kcvc, ks, vs_t, kw, vw_t, nb, cb, fb, cover_p.T, bsz=bp, seq=seq)
        hp = _out_proj(hp, conv_out.reshape(bp * seq, D_CONV), attn, ple_p, i, wo, wg, wp, fg, final=final)
        six_d = lambda a: a.reshape(bp, 2, KV_HEADS, HEAD_DIM, a.shape[-1])
        outs[0].append(six_d(kvc_t))
        outs[2].append(six_d(kvs_t))
        outs[4].append(six_d(kvw_t[:, :, seq - win_p:]))
        outs[6].append(new_conv)

        c3, q, kvc, kvs, kvw, z, gate = _in_proj(hs, g, w_pad)
        conv_out, new_conv = _conv_module(c3.reshape(bs, nq, 3 * D_CONV), state_conv[i], cw, cbias, lg, lb)
        kcvc = _compress_paged(cmp_t, i, page_table, pe_pad, w1cat, w2big)
        attn, new_win = _attn_sample(
            slc_t, win_t, i, page_table, kvs.reshape(bs, nq, KV_DIM), kvw.reshape(bs, nq, KV_DIM),
            q.reshape(bs, nq, D_ATTN), gate.reshape(bs, nq, LANES), z.reshape(bs, nq, D_ATTN), kcvc,
            sba, swa, sbb, sc, cover_s, expand_s, new_state=new_win)
        hs = _out_proj(hs, conv_out.reshape(bs * nq, D_CONV), attn.reshape(bs * nq, D_ATTN), ple_s, i,
                       wo, wg, wp, fg, final=final)
        outs[1].append(kvc.reshape(bs, nq, 2, KV_HEADS, HEAD_DIM))
        outs[3].append(kvs.reshape(bs, nq, 2, KV_HEADS, HEAD_DIM))
        outs[7].append(new_conv)

    outs[5] = None
    stacked = [new_win if o is None else jnp.stack(o) for o in outs]
    for k in (0, 2, 4, 5):
        stacked[k] = from_t(stacked[k])
    return (hp.reshape(bp, seq, D_MODEL), hs.reshape(bs, nq, D_MODEL)) + tuple(stacked)
```

```python
import functools
import math

import numpy as np
import jax
import jax.numpy as jnp
from jax import lax
from jax.experimental import pallas as pl
from jax.experimental.pallas import tpu as pltpu

F32 = jnp.float32
BF16 = jnp.bfloat16

D_MODEL = 1024
D_CONV = 512
CONV_WIDTH = 31
CONV_HIST = CONV_WIDTH - 1
HEAD_DIM = 64
N_HEADS = 8
KV_HEADS = 2
GROUP = N_HEADS // KV_HEADS
D_ATTN = N_HEADS * HEAD_DIM
KV_DIM = 2 * KV_HEADS * HEAD_DIM
N_BRANCH = 3
CMP_STRIDE = 16
CMP_LEN = 2 * CMP_STRIDE
CMP_HIDDEN = 128
SLC_BLOCK = 64
SLC_SHIFT = 6
TOP_N = 8
WINDOW = 512
NUM_BUCKETS = 32
MAX_DISTANCE = 128
PAGE_SIZE = 128
EPS = 1e-6
NEG_INF = -1e30
FORCED_SCORE = 1e4
MASKED_SCORE = -1e4
PAD_SCORE = -3e4
TINY = 1e-30
SCALE = HEAD_DIM ** -0.5
LOG2E = math.log2(math.e)

LANES = 128
HALF = LANES // 2
TQ = 256
TK = 256
TM = 512
VMEM_LIMIT = 56 * 1024 * 1024

SEC_CONV = (0, 3 * D_CONV)
SEC_Q = (SEC_CONV[1], SEC_CONV[1] + D_ATTN)
SEC_KVC = (SEC_Q[1], SEC_Q[1] + KV_DIM)
SEC_KVS = (SEC_KVC[1], SEC_KVC[1] + KV_DIM)
SEC_KVW = (SEC_KVS[1], SEC_KVS[1] + KV_DIM)
SEC_Z = (SEC_KVW[1], SEC_KVW[1] + D_ATTN)
SEC_GATE = (SEC_Z[1], SEC_Z[1] + LANES)
D_IN = SEC_Z[1] + N_BRANCH * N_HEADS
D_IN_PAD = SEC_GATE[1]
SECTIONS = (SEC_CONV, SEC_Q, SEC_KVC, SEC_KVS, SEC_KVW, SEC_Z, SEC_GATE)


def _bucket_lower_bounds():
    n = np.arange(0, 4 * MAX_DISTANCE, dtype=np.int64)
    max_exact = NUM_BUCKETS // 2
    nf = np.maximum(n, 1).astype(np.float32)
    large = max_exact + (np.log(nf / np.float32(max_exact)) / np.float32(math.log(MAX_DISTANCE / max_exact))
                         * np.float32(NUM_BUCKETS - max_exact)).astype(np.int32)
    large = np.minimum(large, NUM_BUCKETS - 1)
    bucket = np.where(n < max_exact, n, large)
    return [int(np.argmax(bucket >= b)) for b in range(NUM_BUCKETS)]


BUCKET_LOWER = _bucket_lower_bounds()


def _dot(a, b):
    return jnp.dot(a.astype(BF16), b.astype(BF16), preferred_element_type=F32)


def _dot_nt(a, b):
    return lax.dot_general(a.astype(BF16), b.astype(BF16), (((1,), (1,)), ((), ())),
                           preferred_element_type=F32)


def _dot_split(a, b):
    hi = a.astype(BF16)
    lo = (a - hi.astype(F32)).astype(BF16)
    return (jnp.dot(hi, b, preferred_element_type=F32) + jnp.dot(lo, b, preferred_element_type=F32))


def _sigmoid(x):
    return 1.0 / (1.0 + jnp.exp(-x))


def _silu(x):
    return x * _sigmoid(x)


def _masked_softmax(s, mask):
    s = jnp.where(mask, s, NEG_INF)
    m = jnp.max(s, axis=-1, keepdims=True)
    e = jnp.exp(s - m) * mask.astype(F32)
    l = jnp.sum(e, axis=-1, keepdims=True)
    return e / jnp.maximum(l, TINY)


def _bias_of(dist, rb_ref, h):
    out = jnp.full(dist.shape, rb_ref[0, h], F32)
    for b in range(1, NUM_BUCKETS):
        out = jnp.where(dist >= BUCKET_LOWER[b], rb_ref[b, h], out)
    return out


def _bias_kernel(rb_ref, nb_ref, cb_ref, fb_ref, sba_ref, swa_ref, sbb_ref, sc_ref, *, n_qt, nc_p, past, wbuf, nq_s):
    h = pl.program_id(0)
    c = lax.broadcasted_iota(jnp.int32, (2 * TK, TQ), 0)
    qi = lax.broadcasted_iota(jnp.int32, (2 * TK, TQ), 1)
    nb_ref[0] = _bias_of(qi + TK - c, rb_ref, h) * LOG2E
    nn = lax.broadcasted_iota(jnp.int32, (nc_p, TQ), 0)
    qn = lax.broadcasted_iota(jnp.int32, (nc_p, TQ), 1)
    for t in range(n_qt):
        cb_ref[t, 0] = _bias_of(t * TQ + qn - (nn * CMP_STRIDE + CMP_LEN - 1), rb_ref, h) * LOG2E
    fb_ref[0] = jnp.full(fb_ref.shape[1:], rb_ref[NUM_BUCKETS - 1, h] * LOG2E, F32)
    qs = lax.broadcasted_iota(jnp.int32, (nq_s, past), 0)
    ks = lax.broadcasted_iota(jnp.int32, (nq_s, past), 1)
    sba_ref[0] = _bias_of(past + qs - ks, rb_ref, h)
    qs = lax.broadcasted_iota(jnp.int32, (nq_s, wbuf), 0)
    ks = lax.broadcasted_iota(jnp.int32, (nq_s, wbuf), 1)
    swa_ref[0] = _bias_of(wbuf + qs - ks, rb_ref, h)
    qs = lax.broadcasted_iota(jnp.int32, (nq_s, LANES), 0)
    ks = lax.broadcasted_iota(jnp.int32, (nq_s, LANES), 1)
    sbb_ref[0] = _bias_of(qs - ks, rb_ref, h)
    nc_s = sc_ref.shape[2]
    qs = lax.broadcasted_iota(jnp.int32, (nq_s, nc_s), 0)
    ks = lax.broadcasted_iota(jnp.int32, (nq_s, nc_s), 1)
    sc_ref[0] = _bias_of(past + qs - (ks * CMP_STRIDE + CMP_LEN - 1), rb_ref, h)


def _bias_tables(rel_bias, *, seq, past, wbuf, nq_s):
    n_qt = seq // TQ
    nc_p = seq // CMP_STRIDE
    nc_s = past // CMP_STRIDE
    kern = functools.partial(_bias_kernel, n_qt=n_qt, nc_p=nc_p, past=past, wbuf=wbuf, nq_s=nq_s)
    shapes = (
        jax.ShapeDtypeStruct((N_HEADS, 2 * TK, TQ), F32),
        jax.ShapeDtypeStruct((n_qt, N_HEADS, nc_p, TQ), F32),
        jax.ShapeDtypeStruct((N_HEADS, 8, TQ), F32),
        jax.ShapeDtypeStruct((N_HEADS, nq_s, past), F32),
        jax.ShapeDtypeStruct((N_HEADS, nq_s, wbuf), F32),
        jax.ShapeDtypeStruct((N_HEADS, nq_s, LANES), F32),
        jax.ShapeDtypeStruct((N_HEADS, nq_s, nc_s), F32),
    )
    out_specs = (
        pl.BlockSpec((1, 2 * TK, TQ), lambda h: (h, 0, 0)),
        pl.BlockSpec((n_qt, 1, nc_p, TQ), lambda h: (0, h, 0, 0)),
        pl.BlockSpec((1, 8, TQ), lambda h: (h, 0, 0)),
        pl.BlockSpec((1, nq_s, past), lambda h: (h, 0, 0)),
        pl.BlockSpec((1, nq_s, wbuf), lambda h: (h, 0, 0)),
        pl.BlockSpec((1, nq_s, LANES), lambda h: (h, 0, 0)),
        pl.BlockSpec((1, nq_s, nc_s), lambda h: (h, 0, 0)),
    )
    return pl.pallas_call(
        kern, grid=(N_HEADS,),
        in_specs=[pl.BlockSpec(memory_space=pltpu.SMEM)],
        out_specs=out_specs, out_shape=shapes, name="bias_tables",
        compiler_params=pltpu.CompilerParams(dimension_semantics=("arbitrary",)),
    )(rel_bias)


KV_SECTIONS = (SEC_KVC, SEC_KVS, SEC_KVW)
ATTN_KV_SECTIONS = (SEC_KVS, SEC_KVW)


def _in_proj_kernel(x_ref, g_ref, w_ref, *refs, kv_transposed, conv_tiles):
    x = x_ref[...]
    u = x * lax.rsqrt(jnp.mean(x * x, axis=-1, keepdims=True) + EPS) * g_ref[...]
    ub = u.astype(BF16)
    refs = list(refs)
    wkv_t_ref = refs.pop(0) if kv_transposed else None
    conv_in = [refs.pop(0) for _ in range(5)] if conv_tiles else None
    out_refs = [refs.pop(0) for _ in SECTIONS]
    attn_refs = [refs.pop(0) for _ in range(2 * len(ATTN_KV_SECTIONS))] if kv_transposed else None

    def project(ref, sec):
        if kv_transposed and sec in KV_SECTIONS:
            k = KV_SECTIONS.index(sec)
            kv_t = lax.dot_general(wkv_t_ref[k * KV_DIM:(k + 1) * KV_DIM, :], ub, (((1,), (1,)), ((), ())),
                                   preferred_element_type=F32)
            ref[0] = kv_t
            if sec in ATTN_KV_SECTIONS:
                j = ATTN_KV_SECTIONS.index(sec)
                attn_refs[2 * j][...] = kv_t[0:LANES, :].T.astype(BF16)
                attn_refs[2 * j + 1][0] = kv_t[LANES:2 * LANES, :].astype(BF16)
        else:
            ref[...] = jnp.dot(ub, w_ref[:, sec[0]:sec[1]], preferred_element_type=F32)

    jobs = [functools.partial(project, ref, sec) for ref, sec in zip(out_refs, SECTIONS)
            if not (conv_tiles and sec == SEC_CONV)]
    if not conv_tiles:
        for job in jobs:
            job()
        return
    hist_ref, cw_ref, cb_ref, lg_ref, lb_ref = conv_in
    new_ref, xbuf, sbuf = refs
    tm = x_ref.shape[0]
    t = pl.program_id(0) % conv_tiles
    c3 = jnp.dot(ub, w_ref[:, SEC_CONV[0]:SEC_CONV[1]], preferred_element_type=F32)
    pl.when(t == 0)(lambda: _conv_load_history(xbuf, hist_ref[0]))
    xbuf[HIST_PAD:HIST_PAD + tm, :] = c3[:, 0:D_CONV] * _sigmoid(c3[:, D_CONV:2 * D_CONV])
    acc = jnp.zeros((tm, D_CONV), F32) + cb_ref[...]
    for phase in range(8):
        acc = _conv_phase(xbuf, sbuf, cw_ref, acc, phase, tm)
        if phase < len(jobs):
            jobs[phase]()
    for job in jobs[8:]:
        job()
    out_refs[0][...] = _conv_finish(acc, c3[:, 2 * D_CONV:3 * D_CONV], lg_ref, lb_ref)
    _conv_carry(xbuf, new_ref, 0, t, conv_tiles, tm)


def _in_proj(x2d, g, w_pad, wkv_t=None, conv=None, *, seq=None):
    n = x2d.shape[0]
    tm = min(TM, n)
    kv_transposed = wkv_t is not None
    in_specs = [pl.BlockSpec((tm, D_MODEL), lambda i: (i, 0)),
                pl.BlockSpec((1, D_MODEL), lambda i: (0, 0)),
                pl.BlockSpec((D_MODEL, D_IN_PAD), lambda i: (0, 0))]
    args = [x2d, g, w_pad]
    out_specs, out_shape, scratch = [], [], []
    if kv_transposed:
        in_specs.append(pl.BlockSpec((len(KV_SECTIONS) * KV_DIM, D_MODEL), lambda i: (0, 0)))
        args.append(wkv_t)
        tiles_per_seq = seq // tm
    if conv is not None:
        assert kv_transposed and tm >= CONV_HIST
        vec = pl.BlockSpec((1, D_CONV), lambda i: (0, 0))
        in_specs += [pl.BlockSpec((1, CONV_HIST, D_CONV), lambda i: (i // tiles_per_seq, 0, 0)),
                     pl.BlockSpec((CONV_WIDTH, D_CONV), lambda i: (0, 0)), vec, vec, vec]
        args += list(conv)
        scratch = [pltpu.VMEM((HIST_PAD + tm, D_CONV), F32), pltpu.VMEM((HIST_PAD + tm, D_CONV), F32)]
    for sec in SECTIONS:
        w = sec[1] - sec[0]
        if conv is not None and sec == SEC_CONV:
            out_specs.append(pl.BlockSpec((tm, D_CONV), lambda i: (i, 0)))
            out_shape.append(jax.ShapeDtypeStruct((n, D_CONV), F32))
        elif kv_transposed and sec in KV_SECTIONS:
            out_specs.append(pl.BlockSpec((1, w, tm), lambda i: (i // tiles_per_seq, 0, i % tiles_per_seq)))
            out_shape.append(jax.ShapeDtypeStruct((n // seq, w, seq), F32))
        else:
            out_specs.append(pl.BlockSpec((tm, w), lambda i: (i, 0)))
            out_shape.append(jax.ShapeDtypeStruct((n, w), F32))
    if kv_transposed:
        for _ in ATTN_KV_SECTIONS:
            out_specs.append(pl.BlockSpec((tm, LANES), lambda i: (i, 0)))
            out_shape.append(jax.ShapeDtypeStruct((n, LANES), BF16))
            out_specs.append(pl.BlockSpec((1, LANES, tm), lambda i: (i // tiles_per_seq, 0, i % tiles_per_seq)))
            out_shape.append(jax.ShapeDtypeStruct((n // seq, LANES, seq), BF16))
    if conv is not None:
        out_specs.append(pl.BlockSpec((1, CONV_HIST, D_CONV), lambda i: (i // tiles_per_seq, 0, 0)))
        out_shape.append(jax.ShapeDtypeStruct((n // seq, CONV_HIST, D_CONV), F32))
    kern = functools.partial(_in_proj_kernel, kv_transposed=kv_transposed,
                             conv_tiles=tiles_per_seq if conv is not None else 0)
    return pl.pallas_call(
        kern, grid=(n // tm,), scratch_shapes=scratch,
        in_specs=in_specs, out_specs=out_specs, out_shape=out_shape, name="in_proj",
        compiler_params=pltpu.CompilerParams(dimension_semantics=("arbitrary",), vmem_limit_bytes=VMEM_LIMIT),
    )(*args)


HIST_PAD = 32


CONV_OFF = HIST_PAD - CONV_HIST


def _conv_load_history(xbuf, hist):
    xbuf[0:HIST_PAD, :] = jnp.zeros((HIST_PAD, D_CONV), F32)
    xbuf[CONV_OFF:HIST_PAD, :] = hist


def _conv_phase(xbuf, sbuf, w_ref, acc, phase, tt):
    a_max = (CONV_WIDTH - 1 - phase) // 8
    sbuf[0:tt + 8 * a_max, :] = xbuf[CONV_OFF + phase:CONV_OFF + phase + tt + 8 * a_max, :]
    for a in range(a_max + 1):
        k = 8 * a + phase
        acc = acc + sbuf[8 * a:8 * a + tt, :] * w_ref[k:k + 1, :]
    return acc


def _conv_finish(acc, z, lg_ref, lb_ref):
    mu = jnp.mean(acc, axis=-1, keepdims=True)
    xc = acc - mu
    var = jnp.mean(xc * xc, axis=-1, keepdims=True)
    y = xc * lax.rsqrt(var + EPS) * lg_ref[...] + lb_ref[...]
    return _silu(y) * _silu(z)


def _conv_carry(xbuf, new_ref, i, t, n_t, tt):
    tail = xbuf[CONV_OFF + tt:HIST_PAD + tt, :]
    if n_t == 1:
        new_ref[i] = tail
        return

    @pl.when(t == n_t - 1)
    def _():
        new_ref[i] = tail

    @pl.when(t < n_t - 1)
    def _():
        xbuf[CONV_OFF:HIST_PAD, :] = tail


def _conv_kernel(c3_ref, hist_ref, w_ref, b_ref, lg_ref, lb_ref, out_ref, new_ref, xbuf, sbuf, *, tt, n_t, n_seq):
    t = pl.program_id(1)

    def one_sequence(i):
        if n_t == 1:
            _conv_load_history(xbuf, hist_ref[i])
        else:
            pl.when(t == 0)(lambda: _conv_load_history(xbuf, hist_ref[i]))
        xbuf[HIST_PAD:HIST_PAD + tt, :] = c3_ref[i, :, 0:D_CONV] * _sigmoid(c3_ref[i, :, D_CONV:2 * D_CONV])
        acc = jnp.zeros((tt, D_CONV), F32) + b_ref[...]
        for phase in range(8):
            acc = _conv_phase(xbuf, sbuf, w_ref, acc, phase, tt)
        out_ref[i] = _conv_finish(acc, c3_ref[i, :, 2 * D_CONV:3 * D_CONV], lg_ref, lb_ref)
        _conv_carry(xbuf, new_ref, i, t, n_t, tt)

    if n_seq == 1:
        one_sequence(0)
    else:
        def body(i, carry):
            one_sequence(i)
            return carry

        lax.fori_loop(0, n_seq, body, 0)


def _conv_module(c3, hist, conv_w, conv_b, ln_g, ln_b):
    bsz, t_len, _ = c3.shape
    tt = min(256, t_len)
    n_t = t_len // tt
    n_seq = 1 if n_t > 1 else math.gcd(bsz, 16)
    kern = functools.partial(_conv_kernel, tt=tt, n_t=n_t, n_seq=n_seq)
    vec = pl.BlockSpec((1, D_CONV), lambda b, t: (0, 0))
    return pl.pallas_call(
        kern, grid=(bsz // n_seq, n_t),
        in_specs=[pl.BlockSpec((n_seq, tt, 3 * D_CONV), lambda b, t: (b, t, 0)),
                  pl.BlockSpec((n_seq, CONV_HIST, D_CONV), lambda b, t: (b, 0, 0)),
                  pl.BlockSpec((CONV_WIDTH, D_CONV), lambda b, t: (0, 0)),
                  vec, vec, vec],
        out_specs=[pl.BlockSpec((n_seq, tt, D_CONV), lambda b, t: (b, t, 0)),
                   pl.BlockSpec((n_seq, CONV_HIST, D_CONV), lambda b, t: (b, 0, 0))],
        out_shape=[jax.ShapeDtypeStruct((bsz, t_len, D_CONV), F32),
                   jax.ShapeDtypeStruct((bsz, CONV_HIST, D_CONV), F32)],
        scratch_shapes=[pltpu.VMEM((HIST_PAD + tt, D_CONV), F32), pltpu.VMEM((HIST_PAD + tt, D_CONV), F32)],
        name="conv_module",
        compiler_params=pltpu.CompilerParams(dimension_semantics=("arbitrary", "arbitrary")),
    )(c3, hist, conv_w, conv_b, ln_g, ln_b)


HALVES_PER_PAGE = PAGE_SIZE // CMP_STRIDE
N_TG = 2 * KV_HEADS
N_LT = KV_DIM // LANES
CMP_FEAT = CMP_STRIDE * HEAD_DIM
X_PITCH = 24
SEQS_PER_STEP = 4


def _compress_body(page_tile, n_pages, pe_ref, w1_ref, w2_ref, out_ref, x_scr, y_scr, h_scr):
    n_half = n_pages * HALVES_PER_PAGE
    n_seq = out_ref.shape[0]
    n_rows = n_seq * n_half
    for i in range(n_seq):
        for p in range(n_pages):
            for t in range(N_LT):
                tile = page_tile(i, p, t).T
                for n in range(HALVES_PER_PAGE):
                    dst = (p * HALVES_PER_PAGE + n) * X_PITCH
                    x_scr[N_LT * i + t, dst:dst + CMP_STRIDE, :] = tile[n * CMP_STRIDE:(n + 1) * CMP_STRIDE, :]
        rows = slice(i * n_half, (i + 1) * n_half)
        for s in range(CMP_STRIDE):
            for t in range(N_LT):
                y_scr[t, rows, s * LANES:(s + 1) * LANES] = (
                    x_scr[N_LT * i + t, pl.ds(s, n_half, stride=X_PITCH), :].astype(BF16))
    two_h = 2 * CMP_HIDDEN
    h_scr[n_rows:n_rows + 8, :] = jnp.zeros((8, KV_HEADS * two_h), F32)
    acts = []
    for t in range(N_LT):
        w1 = w1_ref[t]
        c = jnp.dot(pe_ref[t].astype(BF16), w1, preferred_element_type=F32)
        h_scr[0:n_rows, :] = jnp.dot(y_scr[t], w1, preferred_element_type=F32)
        for g in range(KV_HEADS):
            a0, b0 = g * two_h, g * two_h + CMP_HIDDEN
            cvec = c[0:1, a0:a0 + CMP_HIDDEN] + c[1:2, b0:b0 + CMP_HIDDEN]
            hid = h_scr[0:n_rows, a0:a0 + CMP_HIDDEN] + h_scr[1:n_rows + 1, b0:b0 + CMP_HIDDEN] + cvec
            acts.append(_silu(hid))
    out = jnp.dot(jnp.concatenate(acts, axis=1).astype(BF16), w2_ref[...], preferred_element_type=F32)
    row = lax.broadcasted_iota(jnp.int32, out.shape, 0) & (n_half - 1)
    out_ref[...] = jnp.where(row < n_half - 1, out, 0.0).reshape(n_seq, n_half, KV_DIM)


def _compress_paged_kernel(pt_ref, *refs, n_pages, n_seq):
    del pt_ref
    pages = refs[:n_seq * n_pages]

    def page_tile(i, p, t):
        return pages[i * n_pages + p][0, 0, t].reshape(LANES, PAGE_SIZE)

    _compress_body(page_tile, n_pages, *refs[n_seq * n_pages:])


def _compress_seq_kernel(kv_ref, *refs, n_pages):
    def page_tile(i, p, t):
        return kv_ref[i, t * LANES:(t + 1) * LANES, p * PAGE_SIZE:(p + 1) * PAGE_SIZE]

    _compress_body(page_tile, n_pages, *refs)


def _compress_specs(n_half, n_seq):
    assert n_half & (n_half - 1) == 0
    const = lambda shape: pl.BlockSpec(shape, lambda *a: (0,) * len(shape))
    feat = KV_HEADS * CMP_FEAT
    hid = KV_HEADS * 2 * CMP_HIDDEN
    weight_specs = [const((2, 8, feat)), const((2, feat, hid)), const((N_TG * CMP_HIDDEN, KV_DIM))]
    scratch = [pltpu.VMEM((n_seq * N_LT, n_half * X_PITCH, LANES), F32),
               pltpu.VMEM((N_LT, n_seq * n_half, feat), BF16),
               pltpu.VMEM((n_seq * n_half + 8, hid), F32)]
    return weight_specs, scratch


def _compress_paged(cache_t, layer, table, pe_pad, w1cat, w2big):
    bsz, n_pages = table.shape
    n_half = n_pages * HALVES_PER_PAGE
    n_seq = math.gcd(bsz, SEQS_PER_STEP)
    page_specs = [pl.BlockSpec((1, 1, 2, KV_HEADS, HEAD_DIM, PAGE_SIZE),
                               lambda b, pt, i=i, p=p: (layer, pt[b * n_seq + i, p], 0, 0, 0, 0))
                  for i in range(n_seq) for p in range(n_pages)]
    weight_specs, scratch = _compress_specs(n_half, n_seq)
    out_spec = pl.BlockSpec((n_seq, n_half, KV_DIM), lambda b, pt: (b, 0, 0))
    grid_spec = pltpu.PrefetchScalarGridSpec(
        num_scalar_prefetch=1, grid=(bsz // n_seq,), in_specs=page_specs + weight_specs, out_specs=out_spec,
        scratch_shapes=scratch)
    return pl.pallas_call(
        functools.partial(_compress_paged_kernel, n_pages=n_pages, n_seq=n_seq), grid_spec=grid_spec,
        out_shape=jax.ShapeDtypeStruct((bsz, n_half, KV_DIM), F32), name="compress_paged",
        compiler_params=pltpu.CompilerParams(dimension_semantics=("arbitrary",), vmem_limit_bytes=VMEM_LIMIT),
    )(table, *([cache_t] * (n_seq * n_pages)), pe_pad, w1cat, w2big)


def _compress_seq(kv_t, pe_pad, w1cat, w2big):
    bsz, _, seq = kv_t.shape
    n_pages = seq // PAGE_SIZE
    n_half = n_pages * HALVES_PER_PAGE
    weight_specs, scratch = _compress_specs(n_half, 1)
    out_spec = pl.BlockSpec((1, n_half, KV_DIM), lambda b: (b, 0, 0))
    return pl.pallas_call(
        functools.partial(_compress_seq_kernel, n_pages=n_pages), grid=(bsz,),
        in_specs=[pl.BlockSpec((1, KV_DIM, seq), lambda b: (b, 0, 0))] + weight_specs, out_specs=out_spec,
        scratch_shapes=scratch,
        out_shape=jax.ShapeDtypeStruct((bsz, n_half, KV_DIM), F32), name="compress_seq",
        compiler_params=pltpu.CompilerParams(dimension_semantics=("arbitrary",), vmem_limit_bytes=VMEM_LIMIT),
    )(kv_t, pe_pad, w1cat, w2big)


def _head_rows(q, g, low):
    parts = []
    for r in range(GROUP):
        h = GROUP * g + r
        tile = q[:, LANES * (h // 2):LANES * (h // 2 + 1)]
        if (h % 2) != g:
            tile = pltpu.roll(tile, HALF, 1)
        parts.append(jnp.where(low, tile, 0.0) if g == 0 else jnp.where(low, 0.0, tile))
    return parts


def _assemble_heads(o_heads, low):
    tiles = []
    for j in range(N_HEADS // 2):
        a, b = o_heads[2 * j], o_heads[2 * j + 1]
        if (2 * j) // GROUP == 0:
            tiles.append(jnp.where(low, a, pltpu.roll(b, HALF, 1)))
        else:
            tiles.append(jnp.where(low, pltpu.roll(a, HALF, 1), b))
    return jnp.concatenate(tiles, axis=1)


def _select_blocks_t(score_t, allowed_t, n_blk, top_n):
    idx = lax.broadcasted_iota(jnp.int32, score_t.shape, 0)
    cnt = jnp.zeros(score_t.shape, jnp.int32)
    for i in range(n_blk):
        row = score_t[i:i + 1, :]
        ahead = (row > score_t) | ((row == score_t) & (idx > i))
        cnt = cnt + ahead.astype(jnp.int32)
    return (cnt < top_n) & allowed_t


def _flash_step(q_t, k_tile, v_tile, bias, mask, state):
    m, l, acc = state
    s = jnp.dot(k_tile, q_t, preferred_element_type=F32)
    mask_add = [jnp.where(mask(c), 0.0, NEG_INF) for c in range(TQ // LANES)]
    ms, ls, ps, alphas = [], [], [], []
    for j in range(N_CHUNK):
        r, c = divmod(j, TQ // LANES)
        s_j = s[:, j * LANES:(j + 1) * LANES] + bias(r, pl.ds(c * LANES, LANES)) + mask_add[c]
        m_j = jnp.maximum(m[j], jnp.max(s_j, axis=0, keepdims=True))
        m_use = jnp.where(m_j > 0.5 * NEG_INF, m_j, 0.0)
        alpha = jnp.exp2(m[j] - m_use)
        p_j = jnp.exp2(s_j - m_use)
        ls.append(alpha * l[j] + jnp.sum(p_j, axis=0, keepdims=True))
        ms.append(m_j)
        alphas.append(alpha)
        ps.append(p_j.astype(BF16))
    pv = jnp.dot(v_tile, jnp.concatenate(ps, axis=1), preferred_element_type=F32)
    acc = jnp.concatenate(alphas, axis=1) * acc + pv
    return tuple(ms), tuple(ls), acc


N_CHUNK = GROUP * TQ // LANES


def _flash_init():
    return (tuple(jnp.full((1, LANES), NEG_INF, F32) for _ in range(N_CHUNK)),
            tuple(jnp.zeros((1, LANES), F32) for _ in range(N_CHUNK)),
            jnp.zeros((LANES, GROUP * TQ), F32))


def _flash_out(state):
    _, l, acc = state
    return acc * (1.0 / jnp.maximum(jnp.concatenate(l, axis=1), TINY))


def _attn_prompt_kernel(q_ref, gate_ref, z_ref, kcvc_ref, ks_ref, vs_ref, kw_ref, vw_ref, nb_ref, cb_ref, fb_ref,
                        cover_ref, out_ref, *, n_slc):
    qt = pl.program_id(1)
    nc = kcvc_ref.shape[1]
    ki = lax.broadcasted_iota(jnp.int32, (TK, LANES), 0)
    qi = lax.broadcasted_iota(jnp.int32, (TK, LANES), 1)
    causal = lambda c: qi + c * LANES >= ki
    window_edge = lambda c: (ki > qi + c * LANES) & (qt >= 2)
    q_t = q_ref[...].T
    g_t = _sigmoid(gate_ref[...]).T
    kc = kcvc_ref[0, :, 0:LANES].astype(BF16)
    vc_t = kcvc_ref[0, :, LANES:2 * LANES].T.astype(BF16)
    c_end = lax.broadcasted_iota(jnp.int32, (nc, TQ), 0) * CMP_STRIDE + (CMP_LEN - 1)
    cmask = qt * TQ + lax.broadcasted_iota(jnp.int32, (nc, TQ), 1) >= c_end
    blk_t = lax.broadcasted_iota(jnp.int32, (n_slc, TQ), 0)
    cur_t = jnp.right_shift(qt * TQ + lax.broadcasted_iota(jnp.int32, (n_slc, TQ), 1), SLC_SHIFT)
    allowed_t = blk_t <= cur_t
    forced_t = (blk_t == 0) | (blk_t == cur_t) | (blk_t == cur_t - 1)
    e_row = jnp.right_shift(lax.broadcasted_iota(jnp.int32, (TK, LANES), 0), SLC_SHIFT)
    e_col = lax.broadcasted_iota(jnp.int32, (TK, LANES), 1)
    kt_prev = jnp.maximum(qt - 1, 0)
    kt_far = jnp.maximum(qt - 2, 0)
    zero_half = jnp.zeros((HEAD_DIM, TQ), F32)

    def kv_tile(k_ref, v_ref, kt):
        start = pl.multiple_of(kt * TK, TK)
        return k_ref[pl.ds(start, TK), :], v_ref[0, :, pl.ds(start, TK)]

    pieces = []
    for g in range(KV_HEADS):
        heads = [GROUP * g + r for r in range(GROUP)]
        parts = []
        for h in heads:
            x = q_t[h * HEAD_DIM:(h + 1) * HEAD_DIM, :] * (SCALE * LOG2E)
            parts.append(jnp.concatenate([x, zero_half] if g == 0 else [zero_half, x], axis=0))
        qg = jnp.concatenate(parts, axis=1).astype(BF16)
        s_c = jnp.dot(kc, qg, preferred_element_type=F32)
        p_parts, p_sum = [], None
        for r, h in enumerate(heads):
            s_r = jnp.where(cmask, s_c[:, r * TQ:(r + 1) * TQ] + cb_ref[0, h], NEG_INF)
            e = jnp.where(cmask, jnp.exp2(s_r - jnp.max(s_r, axis=0, keepdims=True)), 0.0)
            p_r = e * (1.0 / jnp.maximum(jnp.sum(e, axis=0, keepdims=True), TINY))
            p_sum = p_r if p_sum is None else p_sum + p_r
            p_parts.append(p_r.astype(BF16))
        o_cmp = jnp.dot(vc_t, jnp.concatenate(p_parts, axis=1), preferred_element_type=F32)
        hi = p_sum.astype(BF16)
        lo = (p_sum - hi.astype(F32)).astype(BF16)
        imp_t = (jnp.dot(cover_ref[...], hi, preferred_element_type=F32)
                 + jnp.dot(cover_ref[...], lo, preferred_element_type=F32))
        score_t = jnp.where(allowed_t, jnp.where(forced_t, FORCED_SCORE, imp_t[0:n_slc]), MASKED_SCORE)
        sel_t = _select_blocks_t(score_t, allowed_t, n_slc, min(TOP_N, n_slc)).astype(F32)
        sel_pad = jnp.concatenate([sel_t, jnp.zeros((LANES - n_slc, TQ), F32)], axis=0).astype(BF16)

        def sel_mask(kt, extra=None):
            expand = (e_row + kt * (TK // SLC_BLOCK) == e_col).astype(BF16)
            hit = jnp.dot(expand, sel_pad, preferred_element_type=F32)

            def mask(c):
                m = hit[:, c * LANES:(c + 1) * LANES] > 0.5
                return m if extra is None else m & extra(c)

            return mask

        far_bias = lambda r, qs: fb_ref[heads[r], 0:1, qs]
        prev_bias = lambda r, qs: nb_ref[heads[r], 0:TK, qs]
        diag_bias = lambda r, qs: nb_ref[heads[r], TK:2 * TK, qs]

        def far_body(kt, state):
            k_t, v_t = kv_tile(ks_ref, vs_ref, kt)
            return _flash_step(qg, k_t, v_t, far_bias, sel_mask(kt), state)

        state = lax.fori_loop(0, jnp.maximum(qt - 1, 0), far_body, _flash_init())
        k_t, v_t = kv_tile(ks_ref, vs_ref, kt_prev)
        state = _flash_step(qg, k_t, v_t, prev_bias, sel_mask(kt_prev, lambda c: qt >= 1), state)
        k_t, v_t = kv_tile(ks_ref, vs_ref, qt)
        state = _flash_step(qg, k_t, v_t, diag_bias, sel_mask(qt, causal), state)
        o_slc = _flash_out(state)
        state = _flash_init()
        k_t, v_t = kv_tile(kw_ref, vw_ref, kt_far)
        state = _flash_step(qg, k_t, v_t, far_bias, window_edge, state)
        k_t, v_t = kv_tile(kw_ref, vw_ref, kt_prev)
        state = _flash_step(qg, k_t, v_t, prev_bias, lambda c: qt >= 1, state)
        k_t, v_t = kv_tile(kw_ref, vw_ref, qt)
        state = _flash_step(qg, k_t, v_t, diag_bias, causal, state)
        o_win = _flash_out(state)
        rows = slice(g * HEAD_DIM, (g + 1) * HEAD_DIM)
        for r, h in enumerate(heads):
            cols = slice(r * TQ, (r + 1) * TQ)
            pieces.append(g_t[h:h + 1, :] * o_cmp[rows, cols]
                          + g_t[N_HEADS + h:N_HEADS + h + 1, :] * o_slc[rows, cols]
                          + g_t[2 * N_HEADS + h:2 * N_HEADS + h + 1, :] * o_win[rows, cols])
    out_ref[...] = jnp.concatenate(pieces, axis=0).T * _silu(z_ref[...])


def _attn_prompt(q2d, gate2d, z2d, kcvc, ks, vs_t, kw, vw_t, nb, cb, fb, cover_t, *, bsz, seq):
    assert WINDOW == 2 * TK and seq % TQ == 0
    n_qt = seq // TQ
    nc = seq // CMP_STRIDE
    n_slc = seq // SLC_BLOCK
    kern = functools.partial(_attn_prompt_kernel, n_slc=n_slc)
    tok = lambda w: pl.BlockSpec((TQ, w), lambda b, t: (b * n_qt + t, 0))
    k_spec = pl.BlockSpec((seq, LANES), lambda b, t: (b, 0))
    v_spec = pl.BlockSpec((1, LANES, seq), lambda b, t: (b, 0, 0))
    return pl.pallas_call(
        kern, grid=(bsz, n_qt),
        in_specs=[tok(D_ATTN), tok(LANES), tok(D_ATTN),
                  pl.BlockSpec((1, nc, KV_DIM), lambda b, t: (b, 0, 0)), k_spec, v_spec, k_spec, v_spec,
                  pl.BlockSpec((N_HEADS, 2 * TK, TQ), lambda b, t: (0, 0, 0)),
                  pl.BlockSpec((1, N_HEADS, nc, TQ), lambda b, t: (t, 0, 0, 0)),
                  pl.BlockSpec((N_HEADS, 8, TQ), lambda b, t: (0, 0, 0)),
                  pl.BlockSpec((LANES, nc), lambda b, t: (0, 0))],
        out_specs=tok(D_ATTN),
        out_shape=jax.ShapeDtypeStruct((bsz * seq, D_ATTN), F32),
        name="attn_prompt",
        compiler_params=pltpu.CompilerParams(dimension_semantics=("arbitrary", "arbitrary"),
                                             vmem_limit_bytes=VMEM_LIMIT),
    )(q2d, gate2d, z2d, kcvc, ks, vs_t, kw, vw_t, nb, cb, fb, cover_t)


def _attn_sample_kernel(pt_ref, *refs, n_pages, nq, n_seq, aliased_state):
    del pt_ref
    shared = list(refs[n_seq * n_pages:])
    if aliased_state:
        del shared[-3]
    for i in range(n_seq):
        _attn_sample_one(i, refs[i * n_pages:(i + 1) * n_pages], *shared, n_pages=n_pages, nq=nq)


def _attn_sample_one(i, pages, win_ref, kvsn_ref, kvwn_ref, q_ref, gate_ref, z_ref, kcvc_ref, sba_ref, swa_ref,
                     sbb_ref, sc_ref, cover_ref, expand_ref, out_ref, nwin_ref, *, n_pages, nq):
    past = n_pages * PAGE_SIZE
    wbuf = win_ref.shape[-1]
    nc = kcvc_ref.shape[1]
    rows = KV_HEADS * GROUP * nq
    cur = past // SLC_BLOCK
    n_slc = cur + 1
    low = lax.broadcasted_iota(jnp.int32, (nq, LANES), 1) < HALF
    q = q_ref[i]
    q_left = (jnp.concatenate(_head_rows(q, 0, low) + _head_rows(q, 1, low), axis=0) * SCALE).astype(BF16)
    qi = lax.broadcasted_iota(jnp.int32, (rows, LANES), 0) & (nq - 1)
    ki = lax.broadcasted_iota(jnp.int32, (rows, LANES), 1)
    new_mask = (ki <= qi) & (ki < nq)
    pad_rows = jnp.zeros((LANES - nq, LANES), F32)

    def new_tile(ref, t):
        return jnp.concatenate([ref[i, :, t * LANES:(t + 1) * LANES], pad_rows], axis=0)

    kc = kcvc_ref[i, :, 0:LANES].astype(BF16)
    vc = kcvc_ref[i, :, LANES:2 * LANES].astype(BF16)
    n_idx = lax.broadcasted_iota(jnp.int32, (rows, nc), 1)
    p_c = _masked_softmax(_dot_nt(q_left, kc) + sc_ref[...], n_idx < nc - 1)
    o_cmp = _dot(p_c, vc)
    blk = lax.broadcasted_iota(jnp.int32, (nq, LANES), 1)
    is_blk = blk < n_slc
    forced = (blk == 0) | (blk == cur) | (blk == cur - 1)
    sel_rows = []
    for g in range(KV_HEADS):
        p_sum = p_c[g * GROUP * nq:g * GROUP * nq + nq]
        for r in range(1, GROUP):
            p_sum = p_sum + p_c[(g * GROUP + r) * nq:(g * GROUP + r + 1) * nq]
        imp = _dot_split(p_sum, cover_ref[...])
        score = jnp.where(is_blk, jnp.where(forced, FORCED_SCORE, imp), PAD_SCORE)
        cnt = jnp.zeros((nq, LANES), jnp.int32)
        for j in range(n_slc):
            col = score[:, j:j + 1]
            cnt = cnt + ((col > score) | ((col == score) & (blk > j))).astype(jnp.int32)
        sel_g = ((cnt < min(TOP_N, n_slc)) & is_blk).astype(F32)
        sel_rows += [sel_g] * GROUP
    sel = jnp.concatenate(sel_rows, axis=0)
    k_pages = [pages[p][0, 0, 0].reshape(LANES, PAGE_SIZE).astype(BF16) for p in range(n_pages)]
    v_pages = [pages[p][0, 0, 1].reshape(LANES, PAGE_SIZE).astype(BF16) for p in range(n_pages)]
    k_new = new_tile(kvsn_ref, 0).astype(BF16)
    v_new = new_tile(kvsn_ref, 1).astype(BF16)
    s_a = jnp.concatenate([_dot(q_left, k) for k in k_pages], axis=1) + sba_ref[...]
    mask_a = jnp.dot(sel.astype(BF16), expand_ref[...], preferred_element_type=F32) > 0.5
    s_b = _dot_nt(q_left, k_new) + sbb_ref[...]
    mask_b = new_mask & (sel[:, cur:cur + 1] > 0.5)
    s_a = jnp.where(mask_a, s_a, NEG_INF)
    s_b = jnp.where(mask_b, s_b, NEG_INF)
    m = jnp.maximum(jnp.max(s_a, axis=-1, keepdims=True), jnp.max(s_b, axis=-1, keepdims=True))
    p_a = jnp.exp(s_a - m) * mask_a.astype(F32)
    p_b = jnp.exp(s_b - m) * mask_b.astype(F32)
    l = jnp.sum(p_a, axis=-1, keepdims=True) + jnp.sum(p_b, axis=-1, keepdims=True)
    acc = _dot(p_b, v_new)
    for p in range(n_pages):
        acc = acc + _dot_nt(p_a[:, p * PAGE_SIZE:(p + 1) * PAGE_SIZE], v_pages[p])
    o_slc = acc / jnp.maximum(l, TINY)
    win_t = [win_ref[0, i, t].reshape(LANES, wbuf) for t in range(2)]
    new_w = [new_tile(kvwn_ref, t) for t in range(2)]
    lane_w = lax.broadcasted_iota(jnp.int32, (LANES, wbuf), 1)
    for t in range(2):
        placed = jnp.concatenate([jnp.zeros((LANES, wbuf - LANES), F32), pltpu.roll(new_w[t].T, LANES - nq, 1)],
                                 axis=1)
        shifted = pltpu.roll(win_t[t], wbuf - nq, 1)
        nwin_ref[0, i, t] = jnp.where(lane_w < wbuf - nq, shifted, placed).reshape(KV_HEADS, HEAD_DIM, wbuf)
    kw_t, vw_t = win_t[0].astype(BF16), win_t[1].astype(BF16)
    kw_new, vw_new = new_w[0].astype(BF16), new_w[1].astype(BF16)
    jw = lax.broadcasted_iota(jnp.int32, (rows, wbuf), 1)
    qw = lax.broadcasted_iota(jnp.int32, (rows, wbuf), 0) & (nq - 1)
    mask_wa = jw > qw
    s_wa = jnp.where(mask_wa, _dot(q_left, kw_t) + swa_ref[...], NEG_INF)
    s_wb = jnp.where(new_mask, _dot_nt(q_left, kw_new) + sbb_ref[...], NEG_INF)
    m = jnp.maximum(jnp.max(s_wa, axis=-1, keepdims=True), jnp.max(s_wb, axis=-1, keepdims=True))
    p_wa = jnp.exp(s_wa - m) * mask_wa.astype(F32)
    p_wb = jnp.exp(s_wb - m) * new_mask.astype(F32)
    l = jnp.sum(p_wa, axis=-1, keepdims=True) + jnp.sum(p_wb, axis=-1, keepdims=True)
    acc = _dot_nt(p_wa, vw_t) + _dot(p_wb, vw_new)
    o_win = acc / jnp.maximum(l, TINY)
    gsig = _sigmoid(gate_ref[i])
    o_heads = []
    for h in range(N_HEADS):
        rs = slice(h * nq, (h + 1) * nq)
        o_heads.append(gsig[:, h:h + 1] * o_cmp[rs] + gsig[:, N_HEADS + h:N_HEADS + h + 1] * o_slc[rs]
                       + gsig[:, 2 * N_HEADS + h:2 * N_HEADS + h + 1] * o_win[rs])
    out_ref[i] = _assemble_heads(o_heads, low) * _silu(z_ref[i])


def _attn_sample(cache_t, win_t, layer, table, kvs_new, kvw_new, q, gate, z, kcvc, sba, swa, sbb, sc, cover, expand,
                 new_state=None):
    bsz, n_pages = table.shape
    nq = q.shape[1]
    wbuf = win_t.shape[-1]
    nc = kcvc.shape[1]
    past = n_pages * PAGE_SIZE
    rows = N_HEADS * nq
    assert nq <= SLC_BLOCK and nq & (nq - 1) == 0 and past % SLC_BLOCK == 0 and wbuf == WINDOW
    n_seq = math.gcd(bsz, SEQS_PER_STEP)
    aliased_state = new_state is not None
    kern = functools.partial(_attn_sample_kernel, n_pages=n_pages, nq=nq, n_seq=n_seq, aliased_state=aliased_state)
    page_specs = [pl.BlockSpec((1, 1, 2, KV_HEADS, HEAD_DIM, PAGE_SIZE),
                               lambda b, pt, i=i, p=p: (layer, pt[b * n_seq + i, p], 0, 0, 0, 0))
                  for i in range(n_seq) for p in range(n_pages)]
    per_b = lambda r, w: pl.BlockSpec((n_seq, r, w), lambda b, pt: (b, 0, 0))
    const = lambda r, w: pl.BlockSpec((r, w), lambda b, pt: (0, 0))
    state_spec = pl.BlockSpec((1, n_seq, 2, KV_HEADS, HEAD_DIM, wbuf), lambda b, pt: (layer, b, 0, 0, 0, 0))
    in_specs = page_specs + [
        state_spec, per_b(nq, KV_DIM), per_b(nq, KV_DIM), per_b(nq, D_ATTN), per_b(nq, LANES),
        per_b(nq, D_ATTN), per_b(nc, KV_DIM),
        const(rows, past), const(rows, wbuf), const(rows, LANES), const(rows, nc),
        const(nc, LANES), const(LANES, past)]
    args = [table, *([cache_t] * (n_seq * n_pages)), win_t, kvs_new, kvw_new, q, gate, z, kcvc, sba, swa, sbb, sc,
            cover, expand]
    aliases = {}
    if aliased_state:
        in_specs.append(pl.BlockSpec(memory_space=pl.ANY))
        aliases = {len(args): 1}
        args.append(new_state)
    grid_spec = pltpu.PrefetchScalarGridSpec(
        num_scalar_prefetch=1, grid=(bsz // n_seq,), in_specs=in_specs,
        out_specs=[per_b(nq, D_ATTN), state_spec])
    return pl.pallas_call(
        kern, grid_spec=grid_spec,
        out_shape=[jax.ShapeDtypeStruct((bsz, nq, D_ATTN), F32),
                   jax.ShapeDtypeStruct(win_t.shape, F32)],
        input_output_aliases=aliases, name="attn_sample",
        compiler_params=pltpu.CompilerParams(dimension_semantics=("arbitrary",), vmem_limit_bytes=VMEM_LIMIT),
    )(*args)


def _out_proj_kernel(h_ref, conv_ref, attn_ref, ple_ref, wo_ref, wg_ref, wp_ref, fg_ref, out_ref, *, final):
    h = h_ref[...]
    h = h + jnp.dot(conv_ref[...].astype(BF16), wo_ref[0:D_CONV, :], preferred_element_type=F32)
    h = h + jnp.dot(attn_ref[...].astype(BF16), wo_ref[D_CONV:D_CONV + D_ATTN, :], preferred_element_type=F32)
    gate = _sigmoid(jnp.dot(h.astype(BF16), wg_ref[...], preferred_element_type=F32))
    h = h + gate * jnp.dot(ple_ref[0].astype(BF16), wp_ref[...], preferred_element_type=F32)
    if final:
        h = h * lax.rsqrt(jnp.mean(h * h, axis=-1, keepdims=True) + EPS) * fg_ref[...]
    out_ref[...] = h


def _out_proj(h2d, conv2d, attn2d, ple3d, layer, wo, wg, wp, fg, *, final):
    n = h2d.shape[0]
    tm = min(TM, n)
    ple_dim = ple3d.shape[-1]
    kern = functools.partial(_out_proj_kernel, final=final)
    tok = lambda w: pl.BlockSpec((tm, w), lambda i: (i, 0))
    const = lambda r, w: pl.BlockSpec((r, w), lambda i: (0, 0))
    return pl.pallas_call(
        kern, grid=(n // tm,),
        in_specs=[tok(D_MODEL), tok(D_CONV), tok(D_ATTN), pl.BlockSpec((1, tm, ple_dim), lambda i: (layer, i, 0)),
                  const(D_CONV + D_ATTN, D_MODEL), const(D_MODEL, D_MODEL), const(ple_dim, D_MODEL),
                  const(1, D_MODEL)],
        out_specs=tok(D_MODEL),
        out_shape=jax.ShapeDtypeStruct((n, D_MODEL), F32),
        name="out_proj",
        compiler_params=pltpu.CompilerParams(dimension_semantics=("arbitrary",), vmem_limit_bytes=VMEM_LIMIT),
    )(h2d, conv2d, attn2d, ple3d, wo, wg, wp, fg)


def _cover_matrix(n_cmp_rows, n_cmp, n_slc):
    c_start = np.arange(n_cmp_rows) * CMP_STRIDE
    c_end = c_start + CMP_LEN - 1
    s_start = np.arange(LANES) * SLC_BLOCK
    cover = (c_start[:, None] < s_start[None, :] + SLC_BLOCK) & (c_end[:, None] >= s_start[None, :])
    cover &= (np.arange(n_cmp_rows)[:, None] < n_cmp) & (np.arange(LANES)[None, :] < n_slc)
    return jnp.asarray(cover, dtype=BF16)


def _expand_matrix(past):
    e = np.arange(LANES)[:, None] == (np.arange(past)[None, :] // SLC_BLOCK)
    return jnp.asarray(e, dtype=BF16)


def kernel(x_prompt, x_sample, cache_cmp_kv, cache_slc_kv, page_table, state_win_kv, state_conv, p_prompt, p_sample, norm_g, w_in, conv_w, conv_b, conv_ln_g, conv_ln_b, cmp_pe, cmp_w1, cmp_w2, w_out, w_ple, w_ple_gate, rel_bias, final_norm_g):
    bp, seq, _ = x_prompt.shape
    bs, nq, _ = x_sample.shape
    depth = w_in.shape[0]
    n_pages = page_table.shape[1]
    past = n_pages * PAGE_SIZE
    wbuf = state_win_kv.shape[2]
    n_pool = cache_cmp_kv.shape[1]
    win_p = min(WINDOW, seq)

    nb, cb, fb, sba, swa, sbb, sc = _bias_tables(rel_bias, seq=seq, past=past, wbuf=wbuf, nq_s=nq)
    rows = N_HEADS * nq
    sba, swa, sbb, sc = (a.reshape(rows, a.shape[-1]) for a in (sba, swa, sbb, sc))
    nc_p, nc_s = seq // CMP_STRIDE, past // CMP_STRIDE
    cover_p = _cover_matrix(nc_p, nc_p - 1, seq // SLC_BLOCK)
    cover_s = _cover_matrix(nc_s, nc_s - 1, past // SLC_BLOCK + 1)
    expand_s = _expand_matrix(past)
    conv_zero = jnp.zeros((bp, CONV_HIST, D_CONV), F32)
    fg = final_norm_g.reshape(1, D_MODEL)
    to_t = lambda a: jnp.transpose(a, (0, 1, 3, 4, 5, 2))
    from_t = lambda a: jnp.transpose(a, (0, 1, 5, 2, 3, 4))
    cmp_t, slc_t, win_t = to_t(cache_cmp_kv), to_t(cache_slc_kv), to_t(state_win_kv)
    ple_p = p_prompt.reshape(depth, bp * seq, -1)
    ple_s = p_sample.reshape(depth, bs * nq, -1)

    hp = x_prompt.reshape(bp * seq, D_MODEL)
    hs = x_sample.reshape(bs * nq, D_MODEL)
    outs = [[] for _ in range(8)]
    new_win = None
    for i in range(depth):
        w_pad = jnp.pad(w_in[i], ((0, 0), (0, D_IN_PAD - D_IN))).astype(BF16)
        wkv_t = w_in[i][:, SEC_KVC[0]:SEC_KVW[1]].T.astype(BF16)
        g = norm_g[i].reshape(1, D_MODEL)
        w1 = jnp.transpose(cmp_w1[i].reshape(2, 2, CMP_STRIDE, HEAD_DIM, CMP_HIDDEN), (0, 2, 3, 1, 4))
        w1 = w1.reshape(2, CMP_STRIDE, HEAD_DIM, 2 * CMP_HIDDEN)
        w1cat = jnp.stack([jnp.concatenate([w1, jnp.zeros_like(w1)], axis=-1),
                           jnp.concatenate([jnp.zeros_like(w1), w1], axis=-1)], axis=2)
        w1cat = w1cat.reshape(2, KV_HEADS * CMP_FEAT, KV_HEADS * 2 * CMP_HIDDEN).astype(BF16)
        pe = cmp_pe[i].reshape(2, 2, CMP_STRIDE, 1, HEAD_DIM)
        pe = jnp.broadcast_to(pe, (2, 2, CMP_STRIDE, KV_HEADS, HEAD_DIM)).reshape(2, 2, KV_HEADS * CMP_FEAT)
        pe_pad = jnp.pad(pe, ((0, 0), (0, 6), (0, 0)))
        w2big = jnp.zeros((N_TG * CMP_HIDDEN, KV_DIM), F32)
        for tg in range(N_TG):
            w2big = w2big.at[tg * CMP_HIDDEN:(tg + 1) * CMP_HIDDEN, tg * HEAD_DIM:(tg + 1) * HEAD_DIM].set(
                cmp_w2[i, tg // KV_HEADS])
        w2big = w2big.astype(BF16)
        wo, wg, wp = w_out[i].astype(BF16), w_ple_gate[i].astype(BF16), w_ple[i].astype(BF16)
        cw, cbias = conv_w[i], conv_b[i].reshape(1, D_CONV)
        lg, lb = conv_ln_g[i].reshape(1, D_CONV), conv_ln_b[i].reshape(1, D_CONV)
        final = i == depth - 1

        conv_out, q, kvc_t, kvs_t, kvw_t, z, gate, ks, vs_t, kw, vw_t, new_conv = _in_proj(
            hp, g, w_pad, wkv_t, (conv_zero, cw, cbias, lg, lb), seq=seq)
        kcvc = _compress_seq(kvc_t, pe_pad, w1cat, w2big)
        attn = _attn_prompt(q, gate, z, kcvc, ks, vs_t, kw, vw_t, nb, cb, fb, cover_p.T, bsz=bp, seq=seq)
        hp = _out_proj(hp, conv_out.reshape(bp * seq, D_CONV), attn, ple_p, i, wo, wg, wp, fg, final=final)
        six_d = lambda a: a.reshape(bp, 2, KV_HEADS, HEAD_DIM, a.shape[-1])
        outs[0].append(six_d(kvc_t))
        outs[2].append(six_d(kvs_t))
        outs[4].append(six_d(kvw_t[:, :, seq - win_p:]))
        outs[6].append(new_conv)

        c3, q, kvc, kvs, kvw, z, gate = _in_proj(hs, g, w_pad)
        conv_out, new_conv = _conv_module(c3.reshape(bs, nq, 3 * D_CONV), state_conv[i], cw, cbias, lg, lb)
        kcvc = _compress_paged(cmp_t, i, page_table, pe_pad, w1cat, w2big)
        attn, new_win = _attn_sample(
            slc_t, win_t, i, page_table, kvs.reshape(bs, nq, KV_DIM), kvw.reshape(bs, nq, KV_DIM),
            q.reshape(bs, nq, D_ATTN), gate.reshape(bs, nq, LANES), z.reshape(bs, nq, D_ATTN), kcvc,
            sba, swa, sbb, sc, cover_s, expand_s, new_state=new_win)
        hs = _out_proj(hs, conv_out.reshape(bs * nq, D_CONV), attn.reshape(bs * nq, D_ATTN), ple_s, i,
                       wo, wg, wp, fg, final=final)
        outs[1].append(kvc.reshape(bs, nq, 2, KV_HEADS, HEAD_DIM))
        outs[3].append(kvs.reshape(bs, nq, 2, KV_HEADS, HEAD_DIM))
        outs[7].append(new_conv)

    outs[5] = None
    stacked = [new_win if o is None else jnp.stack(o) for o in outs]
    for k in (0, 2, 4, 5):
        stacked[k] = from_t(stacked[k])
    return (hp.reshape(bp, seq, D_MODEL), hs.reshape(bs, nq, D_MODEL)) + tuple(stacked)
```

```python
import functools
import math

import numpy as np
import jax
import jax.numpy as jnp
from jax import lax
from jax.experimental import pallas as pl
from jax.experimental.pallas import tpu as pltpu

F32 = jnp.float32
BF16 = jnp.bfloat16

D_MODEL = 1024
D_CONV = 512
CONV_WIDTH = 31
CONV_HIST = CONV_WIDTH - 1
HEAD_DIM = 64
N_HEADS = 8
KV_HEADS = 2
GROUP = N_HEADS // KV_HEADS
D_ATTN = N_HEADS * HEAD_DIM
KV_DIM = 2 * KV_HEADS * HEAD_DIM
N_BRANCH = 3
CMP_STRIDE = 16
CMP_LEN = 2 * CMP_STRIDE
CMP_HIDDEN = 128
SLC_BLOCK = 64
SLC_SHIFT = 6
TOP_N = 8
WINDOW = 512
NUM_BUCKETS = 32
MAX_DISTANCE = 128
PAGE_SIZE = 128
EPS = 1e-6
NEG_INF = -1e30
FORCED_SCORE = 1e4
MASKED_SCORE = -1e4
PAD_SCORE = -3e4
TINY = 1e-30
SCALE = HEAD_DIM ** -0.5
LOG2E = math.log2(math.e)

LANES = 128
HALF = LANES // 2
TQ = 256
TK = 256
TM = 512
VMEM_LIMIT = 56 * 1024 * 1024

SEC_CONV = (0, 3 * D_CONV)
SEC_Q = (SEC_CONV[1], SEC_CONV[1] + D_ATTN)
SEC_KVC = (SEC_Q[1], SEC_Q[1] + KV_DIM)
SEC_KVS = (SEC_KVC[1], SEC_KVC[1] + KV_DIM)
SEC_KVW = (SEC_KVS[1], SEC_KVS[1] + KV_DIM)
SEC_Z = (SEC_KVW[1], SEC_KVW[1] + D_ATTN)
SEC_GATE = (SEC_Z[1], SEC_Z[1] + LANES)
D_IN = SEC_Z[1] + N_BRANCH * N_HEADS
D_IN_PAD = SEC_GATE[1]
SECTIONS = (SEC_CONV, SEC_Q, SEC_KVC, SEC_KVS, SEC_KVW, SEC_Z, SEC_GATE)


def _bucket_lower_bounds():
    n = np.arange(0, 4 * MAX_DISTANCE, dtype=np.int64)
    max_exact = NUM_BUCKETS // 2
    nf = np.maximum(n, 1).astype(np.float32)
    large = max_exact + (np.log(nf / np.float32(max_exact)) / np.float32(math.log(MAX_DISTANCE / max_exact))
                         * np.float32(NUM_BUCKETS - max_exact)).astype(np.int32)
    large = np.minimum(large, NUM_BUCKETS - 1)
    bucket = np.where(n < max_exact, n, large)
    return [int(np.argmax(bucket >= b)) for b in range(NUM_BUCKETS)]


BUCKET_LOWER = _bucket_lower_bounds()


def _dot(a, b):
    return jnp.dot(a.astype(BF16), b.astype(BF16), preferred_element_type=F32)


def _dot_nt(a, b):
    return lax.dot_general(a.astype(BF16), b.astype(BF16), (((1,), (1,)), ((), ())),
                           preferred_element_type=F32)


def _dot_split(a, b):
    hi = a.astype(BF16)
    lo = (a - hi.astype(F32)).astype(BF16)
    return (jnp.dot(hi, b, preferred_element_type=F32) + jnp.dot(lo, b, preferred_element_type=F32))


def _sigmoid(x):
    return 1.0 / (1.0 + jnp.exp(-x))


def _silu(x):
    return x * _sigmoid(x)


def _masked_softmax(s, mask):
    s = jnp.where(mask, s, NEG_INF)
    m = jnp.max(s, axis=-1, keepdims=True)
    e = jnp.exp(s - m) * mask.astype(F32)
    l = jnp.sum(e, axis=-1, keepdims=True)
    return e / jnp.maximum(l, TINY)


def _bias_of(dist, rb_ref, h):
    out = jnp.full(dist.shape, rb_ref[0, h], F32)
    for b in range(1, NUM_BUCKETS):
        out = jnp.where(dist >= BUCKET_LOWER[b], rb_ref[b, h], out)
    return out


def _bias_kernel(rb_ref, nb_ref, cb_ref, fb_ref, sba_ref, swa_ref, sbb_ref, sc_ref, *, n_qt, nc_p, past, wbuf, nq_s):
    h = pl.program_id(0)
    c = lax.broadcasted_iota(jnp.int32, (2 * TK, TQ), 0)
    qi = lax.broadcasted_iota(jnp.int32, (2 * TK, TQ), 1)
    nb_ref[0] = _bias_of(qi + TK - c, rb_ref, h) * LOG2E
    nn = lax.broadcasted_iota(jnp.int32, (nc_p, TQ), 0)
    qn = lax.broadcasted_iota(jnp.int32, (nc_p, TQ), 1)
    for t in range(n_qt):
        cb_ref[t, 0] = _bias_of(t * TQ + qn - (nn * CMP_STRIDE + CMP_LEN - 1), rb_ref, h) * LOG2E
    fb_ref[0] = jnp.full(fb_ref.shape[1:], rb_ref[NUM_BUCKETS - 1, h] * LOG2E, F32)
    qs = lax.broadcasted_iota(jnp.int32, (nq_s, past), 0)
    ks = lax.broadcasted_iota(jnp.int32, (nq_s, past), 1)
    sba_ref[0] = _bias_of(past + qs - ks, rb_ref, h)
    qs = lax.broadcasted_iota(jnp.int32, (nq_s, wbuf), 0)
    ks = lax.broadcasted_iota(jnp.int32, (nq_s, wbuf), 1)
    swa_ref[0] = _bias_of(wbuf + qs - ks, rb_ref, h)
    qs = lax.broadcasted_iota(jnp.int32, (nq_s, LANES), 0)
    ks = lax.broadcasted_iota(jnp.int32, (nq_s, LANES), 1)
    sbb_ref[0] = _bias_of(qs - ks, rb_ref, h)
    nc_s = sc_ref.shape[2]
    qs = lax.broadcasted_iota(jnp.int32, (nq_s, nc_s), 0)
    ks = lax.broadcasted_iota(jnp.int32, (nq_s, nc_s), 1)
    sc_ref[0] = _bias_of(past + qs - (ks * CMP_STRIDE + CMP_LEN - 1), rb_ref, h)


def _bias_tables(rel_bias, *, seq, past, wbuf, nq_s):
    n_qt = seq // TQ
    nc_p = seq // CMP_STRIDE
    nc_s = past // CMP_STRIDE
    kern = functools.partial(_bias_kernel, n_qt=n_qt, nc_p=nc_p, past=past, wbuf=wbuf, nq_s=nq_s)
    shapes = (
        jax.ShapeDtypeStruct((N_HEADS, 2 * TK, TQ), F32),
        jax.ShapeDtypeStruct((n_qt, N_HEADS, nc_p, TQ), F32),
        jax.ShapeDtypeStruct((N_HEADS, 8, TQ), F32),
        jax.ShapeDtypeStruct((N_HEADS, nq_s, past), F32),
        jax.ShapeDtypeStruct((N_HEADS, nq_s, wbuf), F32),
        jax.ShapeDtypeStruct((N_HEADS, nq_s, LANES), F32),
        jax.ShapeDtypeStruct((N_HEADS, nq_s, nc_s), F32),
    )
    out_specs = (
        pl.BlockSpec((1, 2 * TK, TQ), lambda h: (h, 0, 0)),
        pl.BlockSpec((n_qt, 1, nc_p, TQ), lambda h: (0, h, 0, 0)),
        pl.BlockSpec((1, 8, TQ), lambda h: (h, 0, 0)),
        pl.BlockSpec((1, nq_s, past), lambda h: (h, 0, 0)),
        pl.BlockSpec((1, nq_s, wbuf), lambda h: (h, 0, 0)),
        pl.BlockSpec((1, nq_s, LANES), lambda h: (h, 0, 0)),
        pl.BlockSpec((1, nq_s, nc_s), lambda h: (h, 0, 0)),
    )
    return pl.pallas_call(
        kern, grid=(N_HEADS,),
        in_specs=[pl.BlockSpec(memory_space=pltpu.SMEM)],
        out_specs=out_specs, out_shape=shapes, name="bias_tables",
        compiler_params=pltpu.CompilerParams(dimension_semantics=("arbitrary",)),
    )(rel_bias)


KV_SECTIONS = (SEC_KVC, SEC_KVS, SEC_KVW)
ATTN_KV_SECTIONS = (SEC_KVS, SEC_KVW)


def _in_proj_kernel(x_ref, g_ref, w_ref, *refs, kv_transposed, conv_tiles):
    x = x_ref[...]
    u = x * lax.rsqrt(jnp.mean(x * x, axis=-1, keepdims=True) + EPS) * g_ref[...]
    ub = u.astype(BF16)
    refs = list(refs)
    wkv_t_ref = refs.pop(0) if kv_transposed else None
    conv_in = [refs.pop(0) for _ in range(5)] if conv_tiles else None
    out_refs = [refs.pop(0) for _ in SECTIONS]
    attn_refs = [refs.pop(0) for _ in range(2 * len(ATTN_KV_SECTIONS))] if kv_transposed else None

    def project(ref, sec):
        if kv_transposed and sec in KV_SECTIONS:
            k = KV_SECTIONS.index(sec)
            kv_t = lax.dot_general(wkv_t_ref[k * KV_DIM:(k + 1) * KV_DIM, :], ub, (((1,), (1,)), ((), ())),
                                   preferred_element_type=F32)
            ref[0] = kv_t
            if sec in ATTN_KV_SECTIONS:
                j = ATTN_KV_SECTIONS.index(sec)
                attn_refs[2 * j][...] = kv_t[0:LANES, :].T.astype(BF16)
                attn_refs[2 * j + 1][0] = kv_t[LANES:2 * LANES, :].astype(BF16)
        else:
            ref[...] = jnp.dot(ub, w_ref[:, sec[0]:sec[1]], preferred_element_type=F32)

    jobs = [functools.partial(project, ref, sec) for ref, sec in zip(out_refs, SECTIONS)
            if not (conv_tiles and sec == SEC_CONV)]
    if not conv_tiles:
        for job in jobs:
            job()
        return
    hist_ref, cw_ref, cb_ref, lg_ref, lb_ref = conv_in
    new_ref, xbuf, sbuf = refs
    tm = x_ref.shape[0]
    t = pl.program_id(0) % conv_tiles
    c3 = jnp.dot(ub, w_ref[:, SEC_CONV[0]:SEC_CONV[1]], preferred_element_type=F32)
    pl.when(t == 0)(lambda: _conv_load_history(xbuf, hist_ref[0]))
    xbuf[HIST_PAD:HIST_PAD + tm, :] = c3[:, 0:D_CONV] * _sigmoid(c3[:, D_CONV:2 * D_CONV])
    acc = jnp.zeros((tm, D_CONV), F32) + cb_ref[...]
    for phase in range(8):
        acc = _conv_phase(xbuf, sbuf, cw_ref, acc, phase, tm)
        if phase < len(jobs):
            jobs[phase]()
    for job in jobs[8:]:
        job()
    out_refs[0][...] = _conv_finish(acc, c3[:, 2 * D_CONV:3 * D_CONV], lg_ref, lb_ref)
    _conv_carry(xbuf, new_ref, 0, t, conv_tiles, tm)


def _in_proj(x2d, g, w_pad, wkv_t=None, conv=None, *, seq=None):
    n = x2d.shape[0]
    tm = min(TM, n)
    kv_transposed = wkv_t is not None
    in_specs = [pl.BlockSpec((tm, D_MODEL), lambda i: (i, 0)),
                pl.BlockSpec((1, D_MODEL), lambda i: (0, 0)),
                pl.BlockSpec((D_MODEL, D_IN_PAD), lambda i: (0, 0))]
    args = [x2d, g, w_pad]
    out_specs, out_shape, scratch = [], [], []
    if kv_transposed:
        in_specs.append(pl.BlockSpec((len(KV_SECTIONS) * KV_DIM, D_MODEL), lambda i: (0, 0)))
        args.append(wkv_t)
        tiles_per_seq = seq // tm
    if conv is not None:
        assert kv_transposed and tm >= CONV_HIST
        vec = pl.BlockSpec((1, D_CONV), lambda i: (0, 0))
        in_specs += [pl.BlockSpec((1, CONV_HIST, D_CONV), lambda i: (i // tiles_per_seq, 0, 0)),
                     pl.BlockSpec((CONV_WIDTH, D_CONV), lambda i: (0, 0)), vec, vec, vec]
        args += list(conv)
        scratch = [pltpu.VMEM((HIST_PAD + tm, D_CONV), F32), pltpu.VMEM((HIST_PAD + tm, D_CONV), F32)]
    for sec in SECTIONS:
        w = sec[1] - sec[0]
        if conv is not None and sec == SEC_CONV:
            out_specs.append(pl.BlockSpec((tm, D_CONV), lambda i: (i, 0)))
            out_shape.append(jax.ShapeDtypeStruct((n, D_CONV), F32))
        elif kv_transposed and sec in KV_SECTIONS:
            out_specs.append(pl.BlockSpec((1, w, tm), lambda i: (i // tiles_per_seq, 0, i % tiles_per_seq)))
            out_shape.append(jax.ShapeDtypeStruct((n // seq, w, seq), F32))
        else:
            out_specs.append(pl.BlockSpec((tm, w), lambda i: (i, 0)))
            out_shape.append(jax.ShapeDtypeStruct((n, w), F32))
    if kv_transposed:
        for _ in ATTN_KV_SECTIONS:
            out_specs.append(pl.BlockSpec((tm, LANES), lambda i: (i, 0)))
            out_shape.append(jax.ShapeDtypeStruct((n, LANES), BF16))
            out_specs.append(pl.BlockSpec((1, LANES, tm), lambda i: (i // tiles_per_seq, 0, i % tiles_per_seq)))
            out_shape.append(jax.ShapeDtypeStruct((n // seq, LANES, seq), BF16))
    if conv is not None:
        out_specs.append(pl.BlockSpec((1, CONV_HIST, D_CONV), lambda i: (i // tiles_per_seq, 0, 0)))
        out_shape.append(jax.ShapeDtypeStruct((n // seq, CONV_HIST, D_CONV), F32))
    kern = functools.partial(_in_proj_kernel, kv_transposed=kv_transposed,
                             conv_tiles=tiles_per_seq if conv is not None else 0)
    return pl.pallas_call(
        kern, grid=(n // tm,), scratch_shapes=scratch,
        in_specs=in_specs, out_specs=out_specs, out_shape=out_shape, name="in_proj",
        compiler_params=pltpu.CompilerParams(dimension_semantics=("arbitrary",), vmem_limit_bytes=VMEM_LIMIT),
    )(*args)


HIST_PAD = 32


CONV_OFF = HIST_PAD - CONV_HIST


def _conv_load_history(xbuf, hist):
    xbuf[0:HIST_PAD, :] = jnp.zeros((HIST_PAD, D_CONV), F32)
    xbuf[CONV_OFF:HIST_PAD, :] = hist


def _conv_phase(xbuf, sbuf, w_ref, acc, phase, tt):
    a_max = (CONV_WIDTH - 1 - phase) // 8
    sbuf[0:tt + 8 * a_max, :] = xbuf[CONV_OFF + phase:CONV_OFF + phase + tt + 8 * a_max, :]
    for a in range(a_max + 1):
        k = 8 * a + phase
        acc = acc + sbuf[8 * a:8 * a + tt, :] * w_ref[k:k + 1, :]
    return acc


def _conv_finish(acc, z, lg_ref, lb_ref):
    mu = jnp.mean(acc, axis=-1, keepdims=True)
    xc = acc - mu
    var = jnp.mean(xc * xc, axis=-1, keepdims=True)
    y = xc * lax.rsqrt(var + EPS) * lg_ref[...] + lb_ref[...]
    return _silu(y) * _silu(z)


def _conv_carry(xbuf, new_ref, i, t, n_t, tt):
    tail = xbuf[CONV_OFF + tt:HIST_PAD + tt, :]
    if n_t == 1:
        new_ref[i] = tail
        return

    @pl.when(t == n_t - 1)
    def _():
        new_ref[i] = tail

    @pl.when(t < n_t - 1)
    def _():
        xbuf[CONV_OFF:HIST_PAD, :] = tail


def _conv_kernel(c3_ref, hist_ref, w_ref, b_ref, lg_ref, lb_ref, out_ref, new_ref, xbuf, sbuf, *, tt, n_t, n_seq):
    t = pl.program_id(1)

    def one_sequence(i):
        if n_t == 1:
            _conv_load_history(xbuf, hist_ref[i])
        else:
            pl.when(t == 0)(lambda: _conv_load_history(xbuf, hist_ref[i]))
        xbuf[HIST_PAD:HIST_PAD + tt, :] = c3_ref[i, :, 0:D_CONV] * _sigmoid(c3_ref[i, :, D_CONV:2 * D_CONV])
        acc = jnp.zeros((tt, D_CONV), F32) + b_ref[...]
        for phase in range(8):
            acc = _conv_phase(xbuf, sbuf, w_ref, acc, phase, tt)
        out_ref[i] = _conv_finish(acc, c3_ref[i, :, 2 * D_CONV:3 * D_CONV], lg_ref, lb_ref)
        _conv_carry(xbuf, new_ref, i, t, n_t, tt)

    if n_seq == 1:
        one_sequence(0)
    else:
        def body(i, carry):
            one_sequence(i)
            return carry

        lax.fori_loop(0, n_seq, body, 0)


def _conv_module(c3, hist, conv_w, conv_b, ln_g, ln_b):
    bsz, t_len, _ = c3.shape
    tt = min(256, t_len)
    n_t = t_len // tt
    n_seq = 1 if n_t > 1 else math.gcd(bsz, 16)
    kern = functools.partial(_conv_kernel, tt=tt, n_t=n_t, n_seq=n_seq)
    vec = pl.BlockSpec((1, D_CONV), lambda b, t: (0, 0))
    return pl.pallas_call(
        kern, grid=(bsz // n_seq, n_t),
        in_specs=[pl.BlockSpec((n_seq, tt, 3 * D_CONV), lambda b, t: (b, t, 0)),
                  pl.BlockSpec((n_seq, CONV_HIST, D_CONV), lambda b, t: (b, 0, 0)),
                  pl.BlockSpec((CONV_WIDTH, D_CONV), lambda b, t: (0, 0)),
                  vec, vec, vec],
        out_specs=[pl.BlockSpec((n_seq, tt, D_CONV), lambda b, t: (b, t, 0)),
                   pl.BlockSpec((n_seq, CONV_HIST, D_CONV), lambda b, t: (b, 0, 0))],
        out_shape=[jax.ShapeDtypeStruct((bsz, t_len, D_CONV), F32),
                   jax.ShapeDtypeStruct((bsz, CONV_HIST, D_CONV), F32)],
        scratch_shapes=[pltpu.VMEM((HIST_PAD + tt, D_CONV), F32), pltpu.VMEM((HIST_PAD + tt, D_CONV), F32)],
        name="conv_module",
        compiler_params=pltpu.CompilerParams(dimension_semantics=("arbitrary", "arbitrary")),
    )(c3, hist, conv_w, conv_b, ln_g, ln_b)


HALVES_PER_PAGE = PAGE_SIZE // CMP_STRIDE
N_TG = 2 * KV_HEADS
N_LT = KV_DIM // LANES
CMP_FEAT = CMP_STRIDE * HEAD_DIM
X_PITCH = 24
SEQS_PER_STEP = 4


def _compress_body(page_tile, n_pages, pe_ref, w1_ref, w2_ref, out_ref, x_scr, y_scr, h_scr):
    n_half = n_pages * HALVES_PER_PAGE
    n_seq = out_ref.shape[0]
    n_rows = n_seq * n_half
    for i in range(n_seq):
        for p in range(n_pages):
            for t in range(N_LT):
                tile = page_tile(i, p, t).T
                for n in range(HALVES_PER_PAGE):
                    dst = (p * HALVES_PER_PAGE + n) * X_PITCH
                    x_scr[N_LT * i + t, dst:dst + CMP_STRIDE, :] = tile[n * CMP_STRIDE:(n + 1) * CMP_STRIDE, :]
        rows = slice(i * n_half, (i + 1) * n_half)
        for s in range(CMP_STRIDE):
            for t in range(N_LT):
                y_scr[t, rows, s * LANES:(s + 1) * LANES] = (
                    x_scr[N_LT * i + t, pl.ds(s, n_half, stride=X_PITCH), :].astype(BF16))
    two_h = 2 * CMP_HIDDEN
    h_scr[n_rows:n_rows + 8, :] = jnp.zeros((8, KV_HEADS * two_h), F32)
    acts = []
    for t in range(N_LT):
        w1 = w1_ref[t]
        c = jnp.dot(pe_ref[t].astype(BF16), w1, preferred_element_type=F32)
        h_scr[0:n_rows, :] = jnp.dot(y_scr[t], w1, preferred_element_type=F32)
        for g in range(KV_HEADS):
            a0, b0 = g * two_h, g * two_h + CMP_HIDDEN
            cvec = c[0:1, a0:a0 + CMP_HIDDEN] + c[1:2, b0:b0 + CMP_HIDDEN]
            hid = h_scr[0:n_rows, a0:a0 + CMP_HIDDEN] + h_scr[1:n_rows + 1, b0:b0 + CMP_HIDDEN] + cvec
            acts.append(_silu(hid))
    out = jnp.dot(jnp.concatenate(acts, axis=1).astype(BF16), w2_ref[...], preferred_element_type=F32)
    row = lax.broadcasted_iota(jnp.int32, out.shape, 0) & (n_half - 1)
    out_ref[...] = jnp.where(row < n_half - 1, out, 0.0).reshape(n_seq, n_half, KV_DIM)


def _compress_paged_kernel(pt_ref, *refs, n_pages, n_seq):
    del pt_ref
    pages = refs[:n_seq * n_pages]

    def page_tile(i, p, t):
        return pages[i * n_pages + p][0, 0, t].reshape(LANES, PAGE_SIZE)

    _compress_body(page_tile, n_pages, *refs[n_seq * n_pages:])


def _compress_seq_kernel(kv_ref, *refs, n_pages):
    def page_tile(i, p, t):
        return kv_ref[i, t * LANES:(t + 1) * LANES, p * PAGE_SIZE:(p + 1) * PAGE_SIZE]

    _compress_body(page_tile, n_pages, *refs)


def _compress_specs(n_half, n_seq):
    assert n_half & (n_half - 1) == 0
    const = lambda shape: pl.BlockSpec(shape, lambda *a: (0,) * len(shape))
    feat = KV_HEADS * CMP_FEAT
    hid = KV_HEADS * 2 * CMP_HIDDEN
    weight_specs = [const((2, 8, feat)), const((2, feat, hid)), const((N_TG * CMP_HIDDEN, KV_DIM))]
    scratch = [pltpu.VMEM((n_seq * N_LT, n_half * X_PITCH, LANES), F32),
               pltpu.VMEM((N_LT, n_seq * n_half, feat), BF16),
               pltpu.VMEM((n_seq * n_half + 8, hid), F32)]
    return weight_specs, scratch


def _compress_paged(cache_t, layer, table, pe_pad, w1cat, w2big):
    bsz, n_pages = table.shape
    n_half = n_pages * HALVES_PER_PAGE
    n_seq = math.gcd(bsz, SEQS_PER_STEP)
    page_specs = [pl.BlockSpec((1, 1, 2, KV_HEADS, HEAD_DIM, PAGE_SIZE),
                               lambda b, pt, i=i, p=p: (layer, pt[b * n_seq + i, p], 0, 0, 0, 0))
                  for i in range(n_seq) for p in range(n_pages)]
    weight_specs, scratch = _compress_specs(n_half, n_seq)
    out_spec = pl.BlockSpec((n_seq, n_half, KV_DIM), lambda b, pt: (b, 0, 0))
    grid_spec = pltpu.PrefetchScalarGridSpec(
        num_scalar_prefetch=1, grid=(bsz // n_seq,), in_specs=page_specs + weight_specs, out_specs=out_spec,
        scratch_shapes=scratch)
    return pl.pallas_call(
        functools.partial(_compress_paged_kernel, n_pages=n_pages, n_seq=n_seq), grid_spec=grid_spec,
        out_shape=jax.ShapeDtypeStruct((bsz, n_half, KV_DIM), F32), name="compress_paged",
        compiler_params=pltpu.CompilerParams(dimension_semantics=("arbitrary",), vmem_limit_bytes=VMEM_LIMIT),
    )(table, *([cache_t] * (n_seq * n_pages)), pe_pad, w1cat, w2big)


def _compress_seq(kv_t, pe_pad, w1cat, w2big):
    bsz, _, seq = kv_t.shape
    n_pages = seq // PAGE_SIZE
    n_half = n_pages * HALVES_PER_PAGE
    weight_specs, scratch = _compress_specs(n_half, 1)
    out_spec = pl.BlockSpec((1, n_half, KV_DIM), lambda b: (b, 0, 0))
    return pl.pallas_call(
        functools.partial(_compress_seq_kernel, n_pages=n_pages), grid=(bsz,),
        in_specs=[pl.BlockSpec((1, KV_DIM, seq), lambda b: (b, 0, 0))] + weight_specs, out_specs=out_spec,
        scratch_shapes=scratch,
        out_shape=jax.ShapeDtypeStruct((bsz, n_half, KV_DIM), F32), name="compress_seq",
        compiler_params=pltpu.CompilerParams(dimension_semantics=("arbitrary",), vmem_limit_bytes=VMEM_LIMIT),
    )(kv_t, pe_pad, w1cat, w2big)


def _head_rows(q, g, low):
    parts = []
    for r in range(GROUP):
        h = GROUP * g + r
        tile = q[:, LANES * (h // 2):LANES * (h // 2 + 1)]
        if (h % 2) != g:
            tile = pltpu.roll(tile, HALF, 1)
        parts.append(jnp.where(low, tile, 0.0) if g == 0 else jnp.where(low, 0.0, tile))
    return parts


def _assemble_heads(o_heads, low):
    tiles = []
    for j in range(N_HEADS // 2):
        a, b = o_heads[2 * j], o_heads[2 * j + 1]
        if (2 * j) // GROUP == 0:
            tiles.append(jnp.where(low, a, pltpu.roll(b, HALF, 1)))
        else:
            tiles.append(jnp.where(low, pltpu.roll(a, HALF, 1), b))
    return jnp.concatenate(tiles, axis=1)


def _select_blocks_t(score_t, allowed_t, n_blk, top_n):
    idx = lax.broadcasted_iota(jnp.int32, score_t.shape, 0)
    cnt = jnp.zeros(score_t.shape, jnp.int32)
    for i in range(n_blk):
        row = score_t[i:i + 1, :]
        ahead = (row > score_t) | ((row == score_t) & (idx > i))
        cnt = cnt + ahead.astype(jnp.int32)
    return (cnt < top_n) & allowed_t


def _flash_step(q_t, k_tile, v_tile, bias, mask, state, keys=None):
    m, l, acc = state
    s = jnp.dot(k_tile, q_t, preferred_element_type=F32)
    mask_add = [jnp.where(mask(c), 0.0, NEG_INF) for c in range(TQ // LANES)]
    ms, ls, ps, alphas = [], [], [], []
    for j in range(N_CHUNK):
        r, c = divmod(j, TQ // LANES)
        k0, k1 = keys(c) if keys is not None else (0, TK)
        m_add = mask_add[c] if mask_add[c].ndim == 0 else mask_add[c][k0:k1]
        s_j = s[k0:k1, j * LANES:(j + 1) * LANES] + bias(r, pl.ds(c * LANES, LANES), (k0, k1)) + m_add
        m_j = jnp.maximum(m[j], jnp.max(s_j, axis=0, keepdims=True))
        m_use = jnp.where(m_j > 0.5 * NEG_INF, m_j, 0.0)
        alpha = jnp.exp2(m[j] - m_use)
        p_j = jnp.exp2(s_j - m_use)
        ls.append(alpha * l[j] + jnp.sum(p_j, axis=0, keepdims=True))
        ms.append(m_j)
        alphas.append(alpha)
        parts = [p_j.astype(BF16)]
        if k0 > 0:
            parts.insert(0, jnp.zeros((k0, LANES), BF16))
        if k1 < TK:
            parts.append(jnp.zeros((TK - k1, LANES), BF16))
        ps.append(parts[0] if len(parts) == 1 else jnp.concatenate(parts, axis=0))
    pv = jnp.dot(v_tile, jnp.concatenate(ps, axis=1), preferred_element_type=F32)
    acc = jnp.concatenate(alphas, axis=1) * acc + pv
    return tuple(ms), tuple(ls), acc


N_CHUNK = GROUP * TQ // LANES


def _flash_init():
    return (tuple(jnp.full((1, LANES), NEG_INF, F32) for _ in range(N_CHUNK)),
            tuple(jnp.zeros((1, LANES), F32) for _ in range(N_CHUNK)),
            jnp.zeros((LANES, GROUP * TQ), F32))


def _flash_out(state):
    _, l, acc = state
    return acc * (1.0 / jnp.maximum(jnp.concatenate(l, axis=1), TINY))


def _attn_prompt_kernel(q_ref, gate_ref, z_ref, kcvc_ref, ks_ref, vs_ref, kw_ref, vw_ref, nb_ref, cb_ref, fb_ref,
                        cover_ref, out_ref, *, n_slc):
    qt = pl.program_id(1)
    nc = kcvc_ref.shape[1]
    ki = lax.broadcasted_iota(jnp.int32, (TK, LANES), 0)
    qi = lax.broadcasted_iota(jnp.int32, (TK, LANES), 1)
    causal = lambda c: qi + c * LANES >= ki
    causal_keys = lambda c: (0, (c + 1) * LANES)
    window_edge = lambda c: (ki > qi + c * LANES) & (qt >= 2)
    edge_keys = lambda c: (c * LANES, TK)
    q_t = q_ref[...].T
    g_t = _sigmoid(gate_ref[...]).T
    kc = kcvc_ref[0, :, 0:LANES].astype(BF16)
    vc_t = kcvc_ref[0, :, LANES:2 * LANES].T.astype(BF16)
    c_end = lax.broadcasted_iota(jnp.int32, (nc, TQ), 0) * CMP_STRIDE + (CMP_LEN - 1)
    cmask = qt * TQ + lax.broadcasted_iota(jnp.int32, (nc, TQ), 1) >= c_end
    blk_t = lax.broadcasted_iota(jnp.int32, (n_slc, TQ), 0)
    cur_t = jnp.right_shift(qt * TQ + lax.broadcasted_iota(jnp.int32, (n_slc, TQ), 1), SLC_SHIFT)
    allowed_t = blk_t <= cur_t
    forced_t = (blk_t == 0) | (blk_t == cur_t) | (blk_t == cur_t - 1)
    e_row = jnp.right_shift(lax.broadcasted_iota(jnp.int32, (TK, LANES), 0), SLC_SHIFT)
    e_col = lax.broadcasted_iota(jnp.int32, (TK, LANES), 1)
    kt_prev = jnp.maximum(qt - 1, 0)
    kt_far = jnp.maximum(qt - 2, 0)
    zero_half = jnp.zeros((HEAD_DIM, TQ), F32)

    def kv_tile(k_ref, v_ref, kt):
        start = pl.multiple_of(kt * TK, TK)
        return k_ref[pl.ds(start, TK), :], v_ref[0, :, pl.ds(start, TK)]

    pieces = []
    for g in range(KV_HEADS):
        heads = [GROUP * g + r for r in range(GROUP)]
        parts = []
        for h in heads:
            x = q_t[h * HEAD_DIM:(h + 1) * HEAD_DIM, :] * (SCALE * LOG2E)
            parts.append(jnp.concatenate([x, zero_half] if g == 0 else [zero_half, x], axis=0))
        qg = jnp.concatenate(parts, axis=1).astype(BF16)
        s_c = jnp.dot(kc, qg, preferred_element_type=F32)
        p_parts, p_sum = [], None
        for r, h in enumerate(heads):
            s_r = jnp.where(cmask, s_c[:, r * TQ:(r + 1) * TQ] + cb_ref[0, h], NEG_INF)
            e = jnp.where(cmask, jnp.exp2(s_r - jnp.max(s_r, axis=0, keepdims=True)), 0.0)
            p_r = e * (1.0 / jnp.maximum(jnp.sum(e, axis=0, keepdims=True), TINY))
            p_sum = p_r if p_sum is None else p_sum + p_r
            p_parts.append(p_r.astype(BF16))
        o_cmp = jnp.dot(vc_t, jnp.concatenate(p_parts, axis=1), preferred_element_type=F32)
        hi = p_sum.astype(BF16)
        lo = (p_sum - hi.astype(F32)).astype(BF16)
        imp_t = (jnp.dot(cover_ref[...], hi, preferred_element_type=F32)
                 + jnp.dot(cover_ref[...], lo, preferred_element_type=F32))
        score_t = jnp.where(allowed_t, jnp.where(forced_t, FORCED_SCORE, imp_t[0:n_slc]), MASKED_SCORE)
        sel_t = _select_blocks_t(score_t, allowed_t, n_slc, min(TOP_N, n_slc)).astype(F32)
        sel_pad = jnp.concatenate([sel_t, jnp.zeros((LANES - n_slc, TQ), F32)], axis=0).astype(BF16)

        def sel_mask(kt, extra=None):
            expand = (e_row + kt * (TK // SLC_BLOCK) == e_col).astype(BF16)
            hit = jnp.dot(expand, sel_pad, preferred_element_type=F32)

            def mask(c):
                m = hit[:, c * LANES:(c + 1) * LANES] > 0.5
                return m if extra is None else m & extra(c)

            return mask

        far_bias = lambda r, qs, ks: fb_ref[heads[r], 0:1, qs]
        prev_bias = lambda r, qs, ks: nb_ref[heads[r], ks[0]:ks[1], qs]
        diag_bias = lambda r, qs, ks: nb_ref[heads[r], TK + ks[0]:TK + ks[1], qs]

        def far_body(kt, state):
            k_t, v_t = kv_tile(ks_ref, vs_ref, kt)
            return _flash_step(qg, k_t, v_t, far_bias, sel_mask(kt), state)

        state = lax.fori_loop(0, jnp.maximum(qt - 1, 0), far_body, _flash_init())
        k_t, v_t = kv_tile(ks_ref, vs_ref, kt_prev)
        state = _flash_step(qg, k_t, v_t, prev_bias, sel_mask(kt_prev, lambda c: qt >= 1), state)
        k_t, v_t = kv_tile(ks_ref, vs_ref, qt)
        state = _flash_step(qg, k_t, v_t, diag_bias, sel_mask(qt, causal), state, causal_keys)
        o_slc = _flash_out(state)
        state = _flash_init()
        k_t, v_t = kv_tile(kw_ref, vw_ref, kt_far)
        state = _flash_step(qg, k_t, v_t, far_bias, window_edge, state, edge_keys)
        k_t, v_t = kv_tile(kw_ref, vw_ref, kt_prev)
        state = _flash_step(qg, k_t, v_t, prev_bias, lambda c: qt >= 1, state)
        k_t, v_t = kv_tile(kw_ref, vw_ref, qt)
        state = _flash_step(qg, k_t, v_t, diag_bias, causal, state, causal_keys)
        o_win = _flash_out(state)
        rows = slice(g * HEAD_DIM, (g + 1) * HEAD_DIM)
        for r, h in enumerate(heads):
            cols = slice(r * TQ, (r + 1) * TQ)
            pieces.append(g_t[h:h + 1, :] * o_cmp[rows, cols]
                          + g_t[N_HEADS + h:N_HEADS + h + 1, :] * o_slc[rows, cols]
                          + g_t[2 * N_HEADS + h:2 * N_HEADS + h + 1, :] * o_win[rows, cols])
    out_ref[...] = jnp.concatenate(pieces, axis=0).T * _silu(z_ref[...])


def _attn_prompt(q2d, gate2d, z2d, kcvc, ks, vs_t, kw, vw_t, nb, cb, fb, cover_t, *, bsz, seq):
    assert WINDOW == 2 * TK and seq % TQ == 0
    n_qt = seq // TQ
    nc = seq // CMP_STRIDE
    n_slc = seq // SLC_BLOCK
    kern = functools.partial(_attn_prompt_kernel, n_slc=n_slc)
    tok = lambda w: pl.BlockSpec((TQ, w), lambda b, t: (b * n_qt + t, 0))
    k_spec = pl.BlockSpec((seq, LANES), lambda b, t: (b, 0))
    v_spec = pl.BlockSpec((1, LANES, seq), lambda b, t: (b, 0, 0))
    return pl.pallas_call(
        kern, grid=(bsz, n_qt),
        in_specs=[tok(D_ATTN), tok(LANES), tok(D_ATTN),
                  pl.BlockSpec((1, nc, KV_DIM), lambda b, t: (b, 0, 0)), k_spec, v_spec, k_spec, v_spec,
                  pl.BlockSpec((N_HEADS, 2 * TK, TQ), lambda b, t: (0, 0, 0)),
                  pl.BlockSpec((1, N_HEADS, nc, TQ), lambda b, t: (t, 0, 0, 0)),
                  pl.BlockSpec((N_HEADS, 8, TQ), lambda b, t: (0, 0, 0)),
                  pl.BlockSpec((LANES, nc), lambda b, t: (0, 0))],
        out_specs=tok(D_ATTN),
        out_shape=jax.ShapeDtypeStruct((bsz * seq, D_ATTN), F32),
        name="attn_prompt",
        compiler_params=pltpu.CompilerParams(dimension_semantics=("arbitrary", "arbitrary"),
                                             vmem_limit_bytes=VMEM_LIMIT),
    )(q2d, gate2d, z2d, kcvc, ks, vs_t, kw, vw_t, nb, cb, fb, cover_t)


def _attn_sample_kernel(pt_ref, *refs, n_pages, nq, n_seq, aliased_state):
    del pt_ref
    shared = list(refs[n_seq * n_pages:])
    if aliased_state:
        del shared[-3]
    for i in range(n_seq):
        _attn_sample_one(i, refs[i * n_pages:(i + 1) * n_pages], *shared, n_pages=n_pages, nq=nq)


def _attn_sample_one(i, pages, win_ref, kvsn_ref, kvwn_ref, q_ref, gate_ref, z_ref, kcvc_ref, sba_ref, swa_ref,
                     sbb_ref, sc_ref, cover_ref, expand_ref, out_ref, nwin_ref, *, n_pages, nq):
    past = n_pages * PAGE_SIZE
    wbuf = win_ref.shape[-1]
    nc = kcvc_ref.shape[1]
    rows = KV_HEADS * GROUP * nq
    cur = past // SLC_BLOCK
    n_slc = cur + 1
    low = lax.broadcasted_iota(jnp.int32, (nq, LANES), 1) < HALF
    q = q_ref[i]
    q_left = (jnp.concatenate(_head_rows(q, 0, low) + _head_rows(q, 1, low), axis=0) * SCALE).astype(BF16)
    qi = lax.broadcasted_iota(jnp.int32, (rows, LANES), 0) & (nq - 1)
    ki = lax.broadcasted_iota(jnp.int32, (rows, LANES), 1)
    new_mask = (ki <= qi) & (ki < nq)
    pad_rows = jnp.zeros((LANES - nq, LANES), F32)

    def new_tile(ref, t):
        return jnp.concatenate([ref[i, :, t * LANES:(t + 1) * LANES], pad_rows], axis=0)

    kc = kcvc_ref[i, :, 0:LANES].astype(BF16)
    vc = kcvc_ref[i, :, LANES:2 * LANES].astype(BF16)
    n_idx = lax.broadcasted_iota(jnp.int32, (rows, nc), 1)
    p_c = _masked_softmax(_dot_nt(q_left, kc) + sc_ref[...], n_idx < nc - 1)
    o_cmp = _dot(p_c, vc)
    blk = lax.broadcasted_iota(jnp.int32, (nq, LANES), 1)
    is_blk = blk < n_slc
    forced = (blk == 0) | (blk == cur) | (blk == cur - 1)
    sel_rows = []
    for g in range(KV_HEADS):
        p_sum = p_c[g * GROUP * nq:g * GROUP * nq + nq]
        for r in range(1, GROUP):
            p_sum = p_sum + p_c[(g * GROUP + r) * nq:(g * GROUP + r + 1) * nq]
        imp = _dot_split(p_sum, cover_ref[...])
        score = jnp.where(is_blk, jnp.where(forced, FORCED_SCORE, imp), PAD_SCORE)
        cnt = jnp.zeros((nq, LANES), jnp.int32)
        for j in range(n_slc):
            col = score[:, j:j + 1]
            cnt = cnt + ((col > score) | ((col == score) & (blk > j))).astype(jnp.int32)
        sel_g = ((cnt < min(TOP_N, n_slc)) & is_blk).astype(F32)
        sel_rows += [sel_g] * GROUP
    sel = jnp.concatenate(sel_rows, axis=0)
    k_pages = [pages[p][0, 0, 0].reshape(LANES, PAGE_SIZE).astype(BF16) for p in range(n_pages)]
    v_pages = [pages[p][0, 0, 1].reshape(LANES, PAGE_SIZE).astype(BF16) for p in range(n_pages)]
    k_new = new_tile(kvsn_ref, 0).astype(BF16)
    v_new = new_tile(kvsn_ref, 1).astype(BF16)
    s_a = jnp.concatenate([_dot(q_left, k) for k in k_pages], axis=1) + sba_ref[...]
    mask_a = jnp.dot(sel.astype(BF16), expand_ref[...], preferred_element_type=F32) > 0.5
    s_b = _dot_nt(q_left, k_new) + sbb_ref[...]
    mask_b = new_mask & (sel[:, cur:cur + 1] > 0.5)
    s_a = jnp.where(mask_a, s_a, NEG_INF)
    s_b = jnp.where(mask_b, s_b, NEG_INF)
    m = jnp.maximum(jnp.max(s_a, axis=-1, keepdims=True), jnp.max(s_b, axis=-1, keepdims=True))
    p_a = jnp.exp(s_a - m) * mask_a.astype(F32)
    p_b = jnp.exp(s_b - m) * mask_b.astype(F32)
    l = jnp.sum(p_a, axis=-1, keepdims=True) + jnp.sum(p_b, axis=-1, keepdims=True)
    acc = _dot(p_b, v_new)
    for p in range(n_pages):
        acc = acc + _dot_nt(p_a[:, p * PAGE_SIZE:(p + 1) * PAGE_SIZE], v_pages[p])
    o_slc = acc / jnp.maximum(l, TINY)
    win_t = [win_ref[0, i, t].reshape(LANES, wbuf) for t in range(2)]
    new_w = [new_tile(kvwn_ref, t) for t in range(2)]
    lane_w = lax.broadcasted_iota(jnp.int32, (LANES, wbuf), 1)
    for t in range(2):
        placed = jnp.concatenate([jnp.zeros((LANES, wbuf - LANES), F32), pltpu.roll(new_w[t].T, LANES - nq, 1)],
                                 axis=1)
        shifted = pltpu.roll(win_t[t], wbuf - nq, 1)
        nwin_ref[0, i, t] = jnp.where(lane_w < wbuf - nq, shifted, placed).reshape(KV_HEADS, HEAD_DIM, wbuf)
    kw_t, vw_t = win_t[0].astype(BF16), win_t[1].astype(BF16)
    kw_new, vw_new = new_w[0].astype(BF16), new_w[1].astype(BF16)
    jw = lax.broadcasted_iota(jnp.int32, (rows, wbuf), 1)
    qw = lax.broadcasted_iota(jnp.int32, (rows, wbuf), 0) & (nq - 1)
    mask_wa = jw > qw
    s_wa = jnp.where(mask_wa, _dot(q_left, kw_t) + swa_ref[...], NEG_INF)
    s_wb = jnp.where(new_mask, _dot_nt(q_left, kw_new) + sbb_ref[...], NEG_INF)
    m = jnp.maximum(jnp.max(s_wa, axis=-1, keepdims=True), jnp.max(s_wb, axis=-1, keepdims=True))
    p_wa = jnp.exp(s_wa - m) * mask_wa.astype(F32)
    p_wb = jnp.exp(s_wb - m) * new_mask.astype(F32)
    l = jnp.sum(p_wa, axis=-1, keepdims=True) + jnp.sum(p_wb, axis=-1, keepdims=True)
    acc = _dot_nt(p_wa, vw_t) + _dot(p_wb, vw_new)
    o_win = acc / jnp.maximum(l, TINY)
    gsig = _sigmoid(gate_ref[i])
    o_heads = []
    for h in range(N_HEADS):
        rs = slice(h * nq, (h + 1) * nq)
        o_heads.append(gsig[:, h:h + 1] * o_cmp[rs] + gsig[:, N_HEADS + h:N_HEADS + h + 1] * o_slc[rs]
                       + gsig[:, 2 * N_HEADS + h:2 * N_HEADS + h + 1] * o_win[rs])
    out_ref[i] = _assemble_heads(o_heads, low) * _silu(z_ref[i])


def _attn_sample(cache_t, win_t, layer, table, kvs_new, kvw_new, q, gate, z, kcvc, sba, swa, sbb, sc, cover, expand,
                 new_state=None):
    bsz, n_pages = table.shape
    nq = q.shape[1]
    wbuf = win_t.shape[-1]
    nc = kcvc.shape[1]
    past = n_pages * PAGE_SIZE
    rows = N_HEADS * nq
    assert nq <= SLC_BLOCK and nq & (nq - 1) == 0 and past % SLC_BLOCK == 0 and wbuf == WINDOW
    n_seq = math.gcd(bsz, SEQS_PER_STEP)
    aliased_state = new_state is not None
    kern = functools.partial(_attn_sample_kernel, n_pages=n_pages, nq=nq, n_seq=n_seq, aliased_state=aliased_state)
    page_specs = [pl.BlockSpec((1, 1, 2, KV_HEADS, HEAD_DIM, PAGE_SIZE),
                               lambda b, pt, i=i, p=p: (layer, pt[b * n_seq + i, p], 0, 0, 0, 0))
                  for i in range(n_seq) for p in range(n_pages)]
    per_b = lambda r, w: pl.BlockSpec((n_seq, r, w), lambda b, pt: (b, 0, 0))
    const = lambda r, w: pl.BlockSpec((r, w), lambda b, pt: (0, 0))
    state_spec = pl.BlockSpec((1, n_seq, 2, KV_HEADS, HEAD_DIM, wbuf), lambda b, pt: (layer, b, 0, 0, 0, 0))
    in_specs = page_specs + [
        state_spec, per_b(nq, KV_DIM), per_b(nq, KV_DIM), per_b(nq, D_ATTN), per_b(nq, LANES),
        per_b(nq, D_ATTN), per_b(nc, KV_DIM),
        const(rows, past), const(rows, wbuf), const(rows, LANES), const(rows, nc),
        const(nc, LANES), const(LANES, past)]
    args = [table, *([cache_t] * (n_seq * n_pages)), win_t, kvs_new, kvw_new, q, gate, z, kcvc, sba, swa, sbb, sc,
            cover, expand]
    aliases = {}
    if aliased_state:
        in_specs.append(pl.BlockSpec(memory_space=pl.ANY))
        aliases = {len(args): 1}
        args.append(new_state)
    grid_spec = pltpu.PrefetchScalarGridSpec(
        num_scalar_prefetch=1, grid=(bsz // n_seq,), in_specs=in_specs,
        out_specs=[per_b(nq, D_ATTN), state_spec])
    return pl.pallas_call(
        kern, grid_spec=grid_spec,
        out_shape=[jax.ShapeDtypeStruct((bsz, nq, D_ATTN), F32),
                   jax.ShapeDtypeStruct(win_t.shape, F32)],
        input_output_aliases=aliases, name="attn_sample",
        compiler_params=pltpu.CompilerParams(dimension_semantics=("arbitrary",), vmem_limit_bytes=VMEM_LIMIT),
    )(*args)


def _out_proj_kernel(h_ref, conv_ref, attn_ref, ple_ref, wo_ref, wg_ref, wp_ref, fg_ref, out_ref, *, final):
    h = h_ref[...]
    h = h + jnp.dot(conv_ref[...].astype(BF16), wo_ref[0:D_CONV, :], preferred_element_type=F32)
    h = h + jnp.dot(attn_ref[...].astype(BF16), wo_ref[D_CONV:D_CONV + D_ATTN, :], preferred_element_type=F32)
    gate = _sigmoid(jnp.dot(h.astype(BF16), wg_ref[...], preferred_element_type=F32))
    h = h + gate * jnp.dot(ple_ref[0].astype(BF16), wp_ref[...], preferred_element_type=F32)
    if final:
        h = h * lax.rsqrt(jnp.mean(h * h, axis=-1, keepdims=True) + EPS) * fg_ref[...]
    out_ref[...] = h


def _out_proj(h2d, conv2d, attn2d, ple3d, layer, wo, wg, wp, fg, *, final):
    n = h2d.shape[0]
    tm = min(TM, n)
    ple_dim = ple3d.shape[-1]
    kern = functools.partial(_out_proj_kernel, final=final)
    tok = lambda w: pl.BlockSpec((tm, w), lambda i: (i, 0))
    const = lambda r, w: pl.BlockSpec((r, w), lambda i: (0, 0))
    return pl.pallas_call(
        kern, grid=(n // tm,),
        in_specs=[tok(D_MODEL), tok(D_CONV), tok(D_ATTN), pl.BlockSpec((1, tm, ple_dim), lambda i: (layer, i, 0)),
                  const(D_CONV + D_ATTN, D_MODEL), const(D_MODEL, D_MODEL), const(ple_dim, D_MODEL),
                  const(1, D_MODEL)],
        out_specs=tok(D_MODEL),
        out_shape=jax.ShapeDtypeStruct((n, D_MODEL), F32),
        name="out_proj",
        compiler_params=pltpu.CompilerParams(dimension_semantics=("arbitrary",), vmem_limit_bytes=VMEM_LIMIT),
    )(h2d, conv2d, attn2d, ple3d, wo, wg, wp, fg)


def _cover_matrix(n_cmp_rows, n_cmp, n_slc):
    c_start = np.arange(n_cmp_rows) * CMP_STRIDE
    c_end = c_start + CMP_LEN - 1
    s_start = np.arange(LANES) * SLC_BLOCK
    cover = (c_start[:, None] < s_start[None, :] + SLC_BLOCK) & (c_end[:, None] >= s_start[None, :])
    cover &= (np.arange(n_cmp_rows)[:, None] < n_cmp) & (np.arange(LANES)[None, :] < n_slc)
    return jnp.asarray(cover, dtype=BF16)


def _expand_matrix(past):
    e = np.arange(LANES)[:, None] == (np.arange(past)[None, :] // SLC_BLOCK)
    return jnp.asarray(e, dtype=BF16)


def kernel(x_prompt, x_sample, cache_cmp_kv, cache_slc_kv, page_table, state_win_kv, state_conv, p_prompt, p_sample, norm_g, w_in, conv_w, conv_b, conv_ln_g, conv_ln_b, cmp_pe, cmp_w1, cmp_w2, w_out, w_ple, w_ple_gate, rel_bias, final_norm_g):
    bp, seq, _ = x_prompt.shape
    bs, nq, _ = x_sample.shape
    depth = w_in.shape[0]
    n_pages = page_table.shape[1]
    past = n_pages * PAGE_SIZE
    wbuf = state_win_kv.shape[2]
    n_pool = cache_cmp_kv.shape[1]
    win_p = min(WINDOW, seq)

    nb, cb, fb, sba, swa, sbb, sc = _bias_tables(rel_bias, seq=seq, past=past, wbuf=wbuf, nq_s=nq)
    rows = N_HEADS * nq
    sba, swa, sbb, sc = (a.reshape(rows, a.shape[-1]) for a in (sba, swa, sbb, sc))
    nc_p, nc_s = seq // CMP_STRIDE, past // CMP_STRIDE
    cover_p = _cover_matrix(nc_p, nc_p - 1, seq // SLC_BLOCK)
    cover_s = _cover_matrix(nc_s, nc_s - 1, past // SLC_BLOCK + 1)
    expand_s = _expand_matrix(past)
    conv_zero = jnp.zeros((bp, CONV_HIST, D_CONV), F32)
    fg = final_norm_g.reshape(1, D_MODEL)
    to_t = lambda a: jnp.transpose(a, (0, 1, 3, 4, 5, 2))
    from_t = lambda a: jnp.transpose(a, (0, 1, 5, 2, 3, 4))
    cmp_t, slc_t, win_t = to_t(cache_cmp_kv), to_t(cache_slc_kv), to_t(state_win_kv)
    ple_p = p_prompt.reshape(depth, bp * seq, -1)
    ple_s = p_sample.reshape(depth, bs * nq, -1)

    hp = x_prompt.reshape(bp * seq, D_MODEL)
    hs = x_sample.reshape(bs * nq, D_MODEL)
    outs = [[] for _ in range(8)]
    new_win = None
    for i in range(depth):
        w_pad = jnp.pad(w_in[i], ((0, 0), (0, D_IN_PAD - D_IN))).astype(BF16)
        wkv_t = w_in[i][:, SEC_KVC[0]:SEC_KVW[1]].T.astype(BF16)
        g = norm_g[i].reshape(1, D_MODEL)
        w1 = jnp.transpose(cmp_w1[i].reshape(2, 2, CMP_STRIDE, HEAD_DIM, CMP_HIDDEN), (0, 2, 3, 1, 4))
        w1 = w1.reshape(2, CMP_STRIDE, HEAD_DIM, 2 * CMP_HIDDEN)
        w1cat = jnp.stack([jnp.concatenate([w1, jnp.zeros_like(w1)], axis=-1),
                           jnp.concatenate([jnp.zeros_like(w1), w1], axis=-1)], axis=2)
        w1cat = w1cat.reshape(2, KV_HEADS * CMP_FEAT, KV_HEADS * 2 * CMP_HIDDEN).astype(BF16)
        pe = cmp_pe[i].reshape(2, 2, CMP_STRIDE, 1, HEAD_DIM)
        pe = jnp.broadcast_to(pe, (2, 2, CMP_STRIDE, KV_HEADS, HEAD_DIM)).reshape(2, 2, KV_HEADS * CMP_FEAT)
        pe_pad = jnp.pad(pe, ((0, 0), (0, 6), (0, 0)))
        w2big = jnp.zeros((N_TG * CMP_HIDDEN, KV_DIM), F32)
        for tg in range(N_TG):
            w2big = w2big.at[tg * CMP_HIDDEN:(tg + 1) * CMP_HIDDEN, tg * HEAD_DIM:(tg + 1) * HEAD_DIM].set(
                cmp_w2[i, tg // KV_HEADS])
        w2big = w2big.astype(BF16)
        wo, wg, wp = w_out[i].astype(BF16), w_ple_gate[i].astype(BF16), w_ple[i].astype(BF16)
        cw, cbias = conv_w[i], conv_b[i].reshape(1, D_CONV)
        lg, lb = conv_ln_g[i].reshape(1, D_CONV), conv_ln_b[i].reshape(1, D_CONV)
        final = i == depth - 1

        conv_out, q, kvc_t, kvs_t, kvw_t, z, gate, ks, vs_t, kw, vw_t, new_conv = _in_proj(
            hp, g, w_pad, wkv_t, (conv_zero, cw, cbias, lg, lb), seq=seq)
        kcvc = _compress_seq(kvc_t, pe_pad, w1cat, w2big)
        attn = _attn_prompt(q, gate, z, kcvc, ks, vs_t, kw, vw_t, nb, cb, fb, cover_p.T, bsz=bp, seq=seq)
        hp = _out_proj(hp, conv_out.reshape(bp * seq, D_CONV), attn, ple_p, i, wo, wg, wp, fg, final=final)
        six_d = lambda a: a.reshape(bp, 2, KV_HEADS, HEAD_DIM, a.shape[-1])
        outs[0].append(six_d(kvc_t))
        outs[2].append(six_d(kvs_t))
        outs[4].append(six_d(kvw_t[:, :, seq - win_p:]))
        outs[6].append(new_conv)

        c3, q, kvc, kvs, kvw, z, gate = _in_proj(hs, g, w_pad)
        conv_out, new_conv = _conv_module(c3.reshape(bs, nq, 3 * D_CONV), state_conv[i], cw, cbias, lg, lb)
        kcvc = _compress_paged(cmp_t, i, page_table, pe_pad, w1cat, w2big)
        attn, new_win = _attn_sample(
            slc_t, win_t, i, page_table, kvs.reshape(bs, nq, KV_DIM), kvw.reshape(bs, nq, KV_DIM),
            q.reshape(bs, nq, D_ATTN), gate.reshape(bs, nq, LANES), z.reshape(bs, nq, D_ATTN), kcvc,
            sba, swa, sbb, sc, cover_s, expand_s, new_state=new_win)
        hs = _out_proj(hs, conv_out.reshape(bs * nq, D_CONV), attn.reshape(bs * nq, D_ATTN), ple_s, i,
                       wo, wg, wp, fg, final=final)
        outs[1].append(kvc.reshape(bs, nq, 2, KV_HEADS, HEAD_DIM))
        outs[3].append(kvs.reshape(bs, nq, 2, KV_HEADS, HEAD_DIM))
        outs[7].append(new_conv)

    outs[5] = None
    stacked = [new_win if o is None else jnp.stack(o) for o in outs]
    for k in (0, 2, 4, 5):
        stacked[k] = from_t(stacked[k])
    return (hp.reshape(bp, seq, D_MODEL), hs.reshape(bs, nq, D_MODEL)) + tuple(stacked)
```

```python
import functools
import math

import numpy as np
import jax
import jax.numpy as jnp
from jax import lax
from jax.experimental import pallas as pl
from jax.experimental.pallas import tpu as pltpu

F32 = jnp.float32
BF16 = jnp.bfloat16

D_MODEL = 1024
D_CONV = 512
CONV_WIDTH = 31
CONV_HIST = CONV_WIDTH - 1
HEAD_DIM = 64
N_HEADS = 8
KV_HEADS = 2
GROUP = N_HEADS // KV_HEADS
D_ATTN = N_HEADS * HEAD_DIM
KV_DIM = 2 * KV_HEADS * HEAD_DIM
N_BRANCH = 3
CMP_STRIDE = 16
CMP_LEN = 2 * CMP_STRIDE
CMP_HIDDEN = 128
SLC_BLOCK = 64
SLC_SHIFT = 6
TOP_N = 8
WINDOW = 512
NUM_BUCKETS = 32
MAX_DISTANCE = 128
PAGE_SIZE = 128
EPS = 1e-6
NEG_INF = -1e30
FORCED_SCORE = 1e4
MASKED_SCORE = -1e4
PAD_SCORE = -3e4
TINY = 1e-30
SCALE = HEAD_DIM ** -0.5
LOG2E = math.log2(math.e)

LANES = 128
SUBLANES = 8
HALF = LANES // 2
TQ = 256
TK = 256
TM = 512
VMEM_LIMIT = 56 * 1024 * 1024

SEC_CONV = (0, 3 * D_CONV)
SEC_Q = (SEC_CONV[1], SEC_CONV[1] + D_ATTN)
SEC_KVC = (SEC_Q[1], SEC_Q[1] + KV_DIM)
SEC_KVS = (SEC_KVC[1], SEC_KVC[1] + KV_DIM)
SEC_KVW = (SEC_KVS[1], SEC_KVS[1] + KV_DIM)
SEC_Z = (SEC_KVW[1], SEC_KVW[1] + D_ATTN)
SEC_GATE = (SEC_Z[1], SEC_Z[1] + LANES)
D_IN = SEC_Z[1] + N_BRANCH * N_HEADS
D_IN_PAD = SEC_GATE[1]
SECTIONS = (SEC_CONV, SEC_Q, SEC_KVC, SEC_KVS, SEC_KVW, SEC_Z, SEC_GATE)


def _bucket_lower_bounds():
    n = np.arange(0, 4 * MAX_DISTANCE, dtype=np.int64)
    max_exact = NUM_BUCKETS // 2
    nf = np.maximum(n, 1).astype(np.float32)
    large = max_exact + (np.log(nf / np.float32(max_exact)) / np.float32(math.log(MAX_DISTANCE / max_exact))
                         * np.float32(NUM_BUCKETS - max_exact)).astype(np.int32)
    large = np.minimum(large, NUM_BUCKETS - 1)
    bucket = np.where(n < max_exact, n, large)
    return [int(np.argmax(bucket >= b)) for b in range(NUM_BUCKETS)]


BUCKET_LOWER = _bucket_lower_bounds()


def _dot(a, b):
    return jnp.dot(a.astype(BF16), b.astype(BF16), preferred_element_type=F32)


def _dot_nt(a, b):
    return lax.dot_general(a.astype(BF16), b.astype(BF16), (((1,), (1,)), ((), ())),
                           preferred_element_type=F32)


def _dot_split(a, b):
    hi = a.astype(BF16)
    lo = (a - hi.astype(F32)).astype(BF16)
    return (jnp.dot(hi, b, preferred_element_type=F32) + jnp.dot(lo, b, preferred_element_type=F32))


def _sigmoid(x):
    return 1.0 / (1.0 + jnp.exp(-x))


def _silu(x):
    return x * _sigmoid(x)


def _masked_softmax(s, mask):
    s = jnp.where(mask, s, NEG_INF)
    m = jnp.max(s, axis=-1, keepdims=True)
    e = jnp.exp(s - m) * mask.astype(F32)
    l = jnp.sum(e, axis=-1, keepdims=True)
    return e / jnp.maximum(l, TINY)


def _bias_of(dist, rb_ref, h):
    out = jnp.full(dist.shape, rb_ref[0, h], F32)
    for b in range(1, NUM_BUCKETS):
        out = jnp.where(dist >= BUCKET_LOWER[b], rb_ref[b, h], out)
    return out


def _bias_kernel(rb_ref, nb_ref, cb_ref, fb_ref, sba_ref, swa_ref, sbb_ref, sc_ref, *, n_qt, nc_p, past, wbuf, nq_s):
    h = pl.program_id(0)
    c = lax.broadcasted_iota(jnp.int32, (2 * TK, TQ), 0)
    qi = lax.broadcasted_iota(jnp.int32, (2 * TK, TQ), 1)
    nb_ref[0] = _bias_of(qi + TK - c, rb_ref, h) * LOG2E
    nn = lax.broadcasted_iota(jnp.int32, (nc_p, TQ), 0)
    qn = lax.broadcasted_iota(jnp.int32, (nc_p, TQ), 1)
    for t in range(n_qt):
        cb_ref[t, 0] = _bias_of(t * TQ + qn - (nn * CMP_STRIDE + CMP_LEN - 1), rb_ref, h) * LOG2E
    fb_ref[0] = jnp.full(fb_ref.shape[1:], rb_ref[NUM_BUCKETS - 1, h] * LOG2E, F32)
    qs = lax.broadcasted_iota(jnp.int32, (nq_s, past), 0)
    ks = lax.broadcasted_iota(jnp.int32, (nq_s, past), 1)
    sba_ref[0] = _bias_of(past + qs - ks, rb_ref, h)
    qs = lax.broadcasted_iota(jnp.int32, (nq_s, wbuf), 0)
    ks = lax.broadcasted_iota(jnp.int32, (nq_s, wbuf), 1)
    swa_ref[0] = _bias_of(wbuf + qs - ks, rb_ref, h)
    qs = lax.broadcasted_iota(jnp.int32, (nq_s, LANES), 0)
    ks = lax.broadcasted_iota(jnp.int32, (nq_s, LANES), 1)
    sbb_ref[0] = _bias_of(qs - ks, rb_ref, h)
    nc_s = sc_ref.shape[2]
    qs = lax.broadcasted_iota(jnp.int32, (nq_s, nc_s), 0)
    ks = lax.broadcasted_iota(jnp.int32, (nq_s, nc_s), 1)
    sc_ref[0] = _bias_of(past + qs - (ks * CMP_STRIDE + CMP_LEN - 1), rb_ref, h)


def _bias_tables(rel_bias, *, seq, past, wbuf, nq_s):
    n_qt = seq // TQ
    nc_p = seq // CMP_STRIDE
    nc_s = past // CMP_STRIDE
    kern = functools.partial(_bias_kernel, n_qt=n_qt, nc_p=nc_p, past=past, wbuf=wbuf, nq_s=nq_s)
    shapes = (
        jax.ShapeDtypeStruct((N_HEADS, 2 * TK, TQ), F32),
        jax.ShapeDtypeStruct((n_qt, N_HEADS, nc_p, TQ), F32),
        jax.ShapeDtypeStruct((N_HEADS, SUBLANES, TQ), F32),
        jax.ShapeDtypeStruct((N_HEADS, nq_s, past), F32),
        jax.ShapeDtypeStruct((N_HEADS, nq_s, wbuf), F32),
        jax.ShapeDtypeStruct((N_HEADS, nq_s, LANES), F32),
        jax.ShapeDtypeStruct((N_HEADS, nq_s, nc_s), F32),
    )
    out_specs = (
        pl.BlockSpec((1, 2 * TK, TQ), lambda h: (h, 0, 0)),
        pl.BlockSpec((n_qt, 1, nc_p, TQ), lambda h: (0, h, 0, 0)),
        pl.BlockSpec((1, SUBLANES, TQ), lambda h: (h, 0, 0)),
        pl.BlockSpec((1, nq_s, past), lambda h: (h, 0, 0)),
        pl.BlockSpec((1, nq_s, wbuf), lambda h: (h, 0, 0)),
        pl.BlockSpec((1, nq_s, LANES), lambda h: (h, 0, 0)),
        pl.BlockSpec((1, nq_s, nc_s), lambda h: (h, 0, 0)),
    )
    return pl.pallas_call(
        kern, grid=(N_HEADS,),
        in_specs=[pl.BlockSpec(memory_space=pltpu.SMEM)],
        out_specs=out_specs, out_shape=shapes, name="bias_tables",
        compiler_params=pltpu.CompilerParams(dimension_semantics=("arbitrary",)),
    )(rel_bias)


KV_SECTIONS = (SEC_KVC, SEC_KVS, SEC_KVW)
ATTN_KV_SECTIONS = (SEC_KVS, SEC_KVW)


def _in_proj_kernel(x_ref, g_ref, w_ref, *refs, kv_transposed, conv_tiles, n_aliased):
    x = x_ref[...]
    u = x * lax.rsqrt(jnp.mean(x * x, axis=-1, keepdims=True) + EPS) * g_ref[...]
    ub = u.astype(BF16)
    refs = list(refs)
    wkv_t_ref = refs.pop(0) if kv_transposed else None
    conv_in = [refs.pop(0) for _ in range(5)] if conv_tiles else None
    del refs[:n_aliased]
    out_refs = [refs.pop(0) for _ in SECTIONS]
    attn_refs = [refs.pop(0) for _ in range(2 * len(ATTN_KV_SECTIONS))] if kv_transposed else None

    def project(ref, sec):
        if kv_transposed and sec in KV_SECTIONS:
            k = KV_SECTIONS.index(sec)
            kv_t = lax.dot_general(wkv_t_ref[k * KV_DIM:(k + 1) * KV_DIM, :], ub, (((1,), (1,)), ((), ())),
                                   preferred_element_type=F32)
            ref[0, 0] = kv_t
            if sec in ATTN_KV_SECTIONS:
                j = ATTN_KV_SECTIONS.index(sec)
                attn_refs[2 * j][...] = kv_t[0:LANES, :].T.astype(BF16)
                attn_refs[2 * j + 1][0] = kv_t[LANES:2 * LANES, :].astype(BF16)
        else:
            ref[...] = jnp.dot(ub, w_ref[:, sec[0]:sec[1]], preferred_element_type=F32)

    jobs = [functools.partial(project, ref, sec) for ref, sec in zip(out_refs, SECTIONS)
            if not (conv_tiles and sec == SEC_CONV)]
    if not conv_tiles:
        for job in jobs:
            job()
        return
    hist_ref, cw_ref, cb_ref, lg_ref, lb_ref = conv_in
    new_ref, xbuf, sbuf = refs
    tm = x_ref.shape[0]
    t = pl.program_id(0) % conv_tiles
    c3 = jnp.dot(ub, w_ref[:, SEC_CONV[0]:SEC_CONV[1]], preferred_element_type=F32)
    pl.when(t == 0)(lambda: _conv_load_history(xbuf, hist_ref[0]))
    xbuf[HIST_PAD:HIST_PAD + tm, :] = c3[:, 0:D_CONV] * _sigmoid(c3[:, D_CONV:2 * D_CONV])
    acc = jnp.zeros((tm, D_CONV), F32) + cb_ref[...]
    for phase in range(SUBLANES):
        acc = _conv_phase(xbuf, sbuf, cw_ref, acc, phase, tm)
        if phase < len(jobs):
            jobs[phase]()
    for job in jobs[SUBLANES:]:
        job()
    out_refs[0][...] = _conv_finish(acc, c3[:, 2 * D_CONV:3 * D_CONV], lg_ref, lb_ref)
    _conv_carry(xbuf, new_ref, 0, t, conv_tiles, tm)


def _in_proj(x2d, g, w_pad, wkv_t=None, conv=None, *, seq=None, layer=0, depth=1, kv_prev=None):
    n = x2d.shape[0]
    tm = min(TM, n)
    kv_transposed = wkv_t is not None
    in_specs = [pl.BlockSpec((tm, D_MODEL), lambda i: (i, 0)),
                pl.BlockSpec((1, D_MODEL), lambda i: (0, 0)),
                pl.BlockSpec((D_MODEL, D_IN_PAD), lambda i: (0, 0))]
    args = [x2d, g, w_pad]
    out_specs, out_shape, scratch = [], [], []
    if kv_transposed:
        in_specs.append(pl.BlockSpec((len(KV_SECTIONS) * KV_DIM, D_MODEL), lambda i: (0, 0)))
        args.append(wkv_t)
        tiles_per_seq = seq // tm
    if conv is not None:
        assert kv_transposed and tm >= CONV_HIST
        vec = pl.BlockSpec((1, D_CONV), lambda i: (0, 0))
        in_specs += [pl.BlockSpec((1, CONV_HIST, D_CONV), lambda i: (i // tiles_per_seq, 0, 0)),
                     pl.BlockSpec((CONV_WIDTH, D_CONV), lambda i: (0, 0)), vec, vec, vec]
        args += list(conv)
        scratch = [pltpu.VMEM((HIST_PAD + tm, D_CONV), F32), pltpu.VMEM((HIST_PAD + tm, D_CONV), F32)]
    for sec in SECTIONS:
        w = sec[1] - sec[0]
        if conv is not None and sec == SEC_CONV:
            out_specs.append(pl.BlockSpec((tm, D_CONV), lambda i: (i, 0)))
            out_shape.append(jax.ShapeDtypeStruct((n, D_CONV), F32))
        elif kv_transposed and sec in KV_SECTIONS:
            out_specs.append(pl.BlockSpec((1, 1, w, tm),
                                          lambda i: (layer, i // tiles_per_seq, 0, i % tiles_per_seq)))
            out_shape.append(jax.ShapeDtypeStruct((depth, n // seq, w, seq), F32))
        else:
            out_specs.append(pl.BlockSpec((tm, w), lambda i: (i, 0)))
            out_shape.append(jax.ShapeDtypeStruct((n, w), F32))
    if kv_transposed:
        for _ in ATTN_KV_SECTIONS:
            out_specs.append(pl.BlockSpec((tm, LANES), lambda i: (i, 0)))
            out_shape.append(jax.ShapeDtypeStruct((n, LANES), BF16))
            out_specs.append(pl.BlockSpec((1, LANES, tm), lambda i: (i // tiles_per_seq, 0, i % tiles_per_seq)))
            out_shape.append(jax.ShapeDtypeStruct((n // seq, LANES, seq), BF16))
    if conv is not None:
        out_specs.append(pl.BlockSpec((1, CONV_HIST, D_CONV), lambda i: (i // tiles_per_seq, 0, 0)))
        out_shape.append(jax.ShapeDtypeStruct((n // seq, CONV_HIST, D_CONV), F32))
    aliases = {}
    if kv_prev is not None:
        first_out = SECTIONS.index(KV_SECTIONS[0])
        for k, prev in enumerate(kv_prev):
            aliases[len(args)] = first_out + k
            in_specs.append(pl.BlockSpec(memory_space=pl.ANY))
            args.append(prev)
    kern = functools.partial(_in_proj_kernel, kv_transposed=kv_transposed,
                             conv_tiles=tiles_per_seq if conv is not None else 0, n_aliased=len(aliases))
    return pl.pallas_call(
        kern, grid=(n // tm,), scratch_shapes=scratch, input_output_aliases=aliases,
        in_specs=in_specs, out_specs=out_specs, out_shape=out_shape, name="in_proj",
        compiler_params=pltpu.CompilerParams(dimension_semantics=("arbitrary",), vmem_limit_bytes=VMEM_LIMIT),
    )(*args)


HIST_PAD = 32


CONV_OFF = HIST_PAD - CONV_HIST


def _conv_load_history(xbuf, hist):
    xbuf[0:HIST_PAD, :] = jnp.zeros((HIST_PAD, D_CONV), F32)
    xbuf[CONV_OFF:HIST_PAD, :] = hist


def _conv_phase(xbuf, sbuf, w_ref, acc, phase, tt):
    a_max = (CONV_WIDTH - 1 - phase) // SUBLANES
    span = tt + SUBLANES * a_max
    sbuf[0:span, :] = xbuf[CONV_OFF + phase:CONV_OFF + phase + span, :]
    for a in range(a_max + 1):
        k = SUBLANES * a + phase
        acc = acc + sbuf[SUBLANES * a:SUBLANES * a + tt, :] * w_ref[k:k + 1, :]
    return acc


def _conv_finish(acc, z, lg_ref, lb_ref):
    mu = jnp.mean(acc, axis=-1, keepdims=True)
    xc = acc - mu
    var = jnp.mean(xc * xc, axis=-1, keepdims=True)
    y = xc * lax.rsqrt(var + EPS) * lg_ref[...] + lb_ref[...]
    return _silu(y) * _silu(z)


def _conv_carry(xbuf, new_ref, i, t, n_t, tt):
    tail = xbuf[CONV_OFF + tt:HIST_PAD + tt, :]
    if n_t == 1:
        new_ref[i] = tail
        return

    @pl.when(t == n_t - 1)
    def _():
        new_ref[i] = tail

    @pl.when(t < n_t - 1)
    def _():
        xbuf[CONV_OFF:HIST_PAD, :] = tail


def _conv_kernel(c3_ref, hist_ref, w_ref, b_ref, lg_ref, lb_ref, out_ref, new_ref, xbuf, sbuf, *, tt, n_t, n_seq):
    t = pl.program_id(1)

    def one_sequence(i):
        if n_t == 1:
            _conv_load_history(xbuf, hist_ref[i])
        else:
            pl.when(t == 0)(lambda: _conv_load_history(xbuf, hist_ref[i]))
        xbuf[HIST_PAD:HIST_PAD + tt, :] = c3_ref[i, :, 0:D_CONV] * _sigmoid(c3_ref[i, :, D_CONV:2 * D_CONV])
        acc = jnp.zeros((tt, D_CONV), F32) + b_ref[...]
        for phase in range(SUBLANES):
            acc = _conv_phase(xbuf, sbuf, w_ref, acc, phase, tt)
        out_ref[i] = _conv_finish(acc, c3_ref[i, :, 2 * D_CONV:3 * D_CONV], lg_ref, lb_ref)
        _conv_carry(xbuf, new_ref, i, t, n_t, tt)

    if n_seq == 1:
        one_sequence(0)
    else:
        def body(i, carry):
            one_sequence(i)
            return carry

        lax.fori_loop(0, n_seq, body, 0)


def _conv_module(c3, hist, conv_w, conv_b, ln_g, ln_b):
    bsz, t_len, _ = c3.shape
    tt = min(256, t_len)
    n_t = t_len // tt
    n_seq = 1 if n_t > 1 else math.gcd(bsz, 16)
    kern = functools.partial(_conv_kernel, tt=tt, n_t=n_t, n_seq=n_seq)
    vec = pl.BlockSpec((1, D_CONV), lambda b, t: (0, 0))
    return pl.pallas_call(
        kern, grid=(bsz // n_seq, n_t),
        in_specs=[pl.BlockSpec((n_seq, tt, 3 * D_CONV), lambda b, t: (b, t, 0)),
                  pl.BlockSpec((n_seq, CONV_HIST, D_CONV), lambda b, t: (b, 0, 0)),
                  pl.BlockSpec((CONV_WIDTH, D_CONV), lambda b, t: (0, 0)),
                  vec, vec, vec],
        out_specs=[pl.BlockSpec((n_seq, tt, D_CONV), lambda b, t: (b, t, 0)),
                   pl.BlockSpec((n_seq, CONV_HIST, D_CONV), lambda b, t: (b, 0, 0))],
        out_shape=[jax.ShapeDtypeStruct((bsz, t_len, D_CONV), F32),
                   jax.ShapeDtypeStruct((bsz, CONV_HIST, D_CONV), F32)],
        scratch_shapes=[pltpu.VMEM((HIST_PAD + tt, D_CONV), F32), pltpu.VMEM((HIST_PAD + tt, D_CONV), F32)],
        name="conv_module",
        compiler_params=pltpu.CompilerParams(dimension_semantics=("arbitrary", "arbitrary")),
    )(c3, hist, conv_w, conv_b, ln_g, ln_b)


HALVES_PER_PAGE = PAGE_SIZE // CMP_STRIDE
N_TG = 2 * KV_HEADS
N_LT = KV_DIM // LANES
CMP_FEAT = CMP_STRIDE * HEAD_DIM
X_PITCH = 24
SEQS_PER_STEP = 4


def _compress_body(page_tile, n_pages, pe_ref, w1_ref, w2_ref, out_ref, x_scr, y_scr, h_scr):
    n_half = n_pages * HALVES_PER_PAGE
    n_seq = out_ref.shape[0]
    n_rows = n_seq * n_half
    for i in range(n_seq):
        for p in range(n_pages):
            for t in range(N_LT):
                tile = page_tile(i, p, t).T
                for n in range(HALVES_PER_PAGE):
                    dst = (p * HALVES_PER_PAGE + n) * X_PITCH
                    x_scr[N_LT * i + t, dst:dst + CMP_STRIDE, :] = tile[n * CMP_STRIDE:(n + 1) * CMP_STRIDE, :]
        rows = slice(i * n_half, (i + 1) * n_half)
        for s in range(CMP_STRIDE):
            for t in range(N_LT):
                y_scr[t, rows, s * LANES:(s + 1) * LANES] = (
                    x_scr[N_LT * i + t, pl.ds(s, n_half, stride=X_PITCH), :].astype(BF16))
    two_h = 2 * CMP_HIDDEN
    h_scr[n_rows:n_rows + SUBLANES, :] = jnp.zeros((SUBLANES, KV_HEADS * two_h), F32)
    acts = []
    for t in range(N_LT):
        w1 = w1_ref[t]
        c = jnp.dot(pe_ref[t].astype(BF16), w1, preferred_element_type=F32)
        h_scr[0:n_rows, :] = jnp.dot(y_scr[t], w1, preferred_element_type=F32)
        for g in range(KV_HEADS):
            a0, b0 = g * two_h, g * two_h + CMP_HIDDEN
            cvec = c[0:1, a0:a0 + CMP_HIDDEN] + c[1:2, b0:b0 + CMP_HIDDEN]
            hid = h_scr[0:n_rows, a0:a0 + CMP_HIDDEN] + h_scr[1:n_rows + 1, b0:b0 + CMP_HIDDEN] + cvec
            acts.append(_silu(hid))
    out = jnp.dot(jnp.concatenate(acts, axis=1).astype(BF16), w2_ref[...], preferred_element_type=F32)
    row = lax.broadcasted_iota(jnp.int32, out.shape, 0) & (n_half - 1)
    out_ref[...] = jnp.where(row < n_half - 1, out, 0.0).reshape(n_seq, n_half, KV_DIM)


def _compress_paged_kernel(pt_ref, *refs, n_pages, n_seq):
    del pt_ref
    pages = refs[:n_seq * n_pages]

    def page_tile(i, p, t):
        return pages[i * n_pages + p][0, 0, t].reshape(LANES, PAGE_SIZE)

    _compress_body(page_tile, n_pages, *refs[n_seq * n_pages:])


def _compress_seq_kernel(kv_ref, *refs, n_pages):
    def page_tile(i, p, t):
        return kv_ref[0, i, t * LANES:(t + 1) * LANES, p * PAGE_SIZE:(p + 1) * PAGE_SIZE]

    _compress_body(page_tile, n_pages, *refs)


def _compress_specs(n_half, n_seq):
    assert n_half & (n_half - 1) == 0
    const = lambda shape: pl.BlockSpec(shape, lambda *a: (0,) * len(shape))
    feat = KV_HEADS * CMP_FEAT
    hid = KV_HEADS * 2 * CMP_HIDDEN
    weight_specs = [const((2, SUBLANES, feat)), const((2, feat, hid)), const((N_TG * CMP_HIDDEN, KV_DIM))]
    scratch = [pltpu.VMEM((n_seq * N_LT, n_half * X_PITCH, LANES), F32),
               pltpu.VMEM((N_LT, n_seq * n_half, feat), BF16),
               pltpu.VMEM((n_seq * n_half + SUBLANES, hid), F32)]
    return weight_specs, scratch


def _compress_paged(cache_t, layer, table, pe_pad, w1cat, w2big):
    bsz, n_pages = table.shape
    n_half = n_pages * HALVES_PER_PAGE
    n_seq = math.gcd(bsz, SEQS_PER_STEP)
    page_specs = [pl.BlockSpec((1, 1, 2, KV_HEADS, HEAD_DIM, PAGE_SIZE),
                               lambda b, pt, i=i, p=p: (layer, pt[b * n_seq + i, p], 0, 0, 0, 0))
                  for i in range(n_seq) for p in range(n_pages)]
    weight_specs, scratch = _compress_specs(n_half, n_seq)
    out_spec = pl.BlockSpec((n_seq, n_half, KV_DIM), lambda b, pt: (b, 0, 0))
    grid_spec = pltpu.PrefetchScalarGridSpec(
        num_scalar_prefetch=1, grid=(bsz // n_seq,), in_specs=page_specs + weight_specs, out_specs=out_spec,
        scratch_shapes=scratch)
    return pl.pallas_call(
        functools.partial(_compress_paged_kernel, n_pages=n_pages, n_seq=n_seq), grid_spec=grid_spec,
        out_shape=jax.ShapeDtypeStruct((bsz, n_half, KV_DIM), F32), name="compress_paged",
        compiler_params=pltpu.CompilerParams(dimension_semantics=("arbitrary",), vmem_limit_bytes=VMEM_LIMIT),
    )(table, *([cache_t] * (n_seq * n_pages)), pe_pad, w1cat, w2big)


def _compress_seq(kv_t, layer, pe_pad, w1cat, w2big):
    _, bsz, _, seq = kv_t.shape
    n_pages = seq // PAGE_SIZE
    n_half = n_pages * HALVES_PER_PAGE
    weight_specs, scratch = _compress_specs(n_half, 1)
    out_spec = pl.BlockSpec((1, n_half, KV_DIM), lambda b: (b, 0, 0))
    return pl.pallas_call(
        functools.partial(_compress_seq_kernel, n_pages=n_pages), grid=(bsz,),
        in_specs=[pl.BlockSpec((1, 1, KV_DIM, seq), lambda b: (layer, b, 0, 0))] + weight_specs, out_specs=out_spec,
        scratch_shapes=scratch,
        out_shape=jax.ShapeDtypeStruct((bsz, n_half, KV_DIM), F32), name="compress_seq",
        compiler_params=pltpu.CompilerParams(dimension_semantics=("arbitrary",), vmem_limit_bytes=VMEM_LIMIT),
    )(kv_t, pe_pad, w1cat, w2big)


def _head_rows(q, g, low):
    parts = []
    for r in range(GROUP):
        h = GROUP * g + r
        tile = q[:, LANES * (h // 2):LANES * (h // 2 + 1)]
        if (h % 2) != g:
            tile = pltpu.roll(tile, HALF, 1)
        parts.append(jnp.where(low, tile, 0.0) if g == 0 else jnp.where(low, 0.0, tile))
    return parts


def _assemble_heads(o_heads, low):
    tiles = []
    for j in range(N_HEADS // 2):
        a, b = o_heads[2 * j], o_heads[2 * j + 1]
        if (2 * j) // GROUP == 0:
            tiles.append(jnp.where(low, a, pltpu.roll(b, HALF, 1)))
        else:
            tiles.append(jnp.where(low, pltpu.roll(a, HALF, 1), b))
    return jnp.concatenate(tiles, axis=1)


def _select_blocks_t(score_t, allowed_t, n_blk, top_n):
    idx = lax.broadcasted_iota(jnp.int32, score_t.shape, 0)
    cnt = jnp.zeros(score_t.shape, jnp.int32)
    for i in range(n_blk):
        row = score_t[i:i + 1, :]
        ahead = (row > score_t) | ((row == score_t) & (idx > i))
        cnt = cnt + ahead.astype(jnp.int32)
    return (cnt < top_n) & allowed_t


def _flash_step(q_t, k_tile, v_tile, bias, mask, state, keys=None):
    m, l, acc = state
    s = jnp.dot(k_tile, q_t, preferred_element_type=F32)
    mask_add = [jnp.where(mask(c), 0.0, NEG_INF) for c in range(TQ // LANES)]
    ms, ls, ps, alphas = [], [], [], []
    for j in range(N_CHUNK):
        r, c = divmod(j, TQ // LANES)
        k0, k1 = keys(c) if keys is not None else (0, TK)
        m_add = mask_add[c] if mask_add[c].ndim == 0 else mask_add[c][k0:k1]
        s_j = s[k0:k1, j * LANES:(j + 1) * LANES] + bias(r, pl.ds(c * LANES, LANES), (k0, k1)) + m_add
        m_j = jnp.maximum(m[j], jnp.max(s_j, axis=0, keepdims=True))
        m_use = jnp.where(m_j > 0.5 * NEG_INF, m_j, 0.0)
        alpha = jnp.exp2(m[j] - m_use)
        p_j = jnp.exp2(s_j - m_use)
        ls.append(alpha * l[j] + jnp.sum(p_j, axis=0, keepdims=True))
        ms.append(m_j)
        alphas.append(alpha)
        parts = [p_j.astype(BF16)]
        if k0 > 0:
            parts.insert(0, jnp.zeros((k0, LANES), BF16))
        if k1 < TK:
            parts.append(jnp.zeros((TK - k1, LANES), BF16))
        ps.append(parts[0] if len(parts) == 1 else jnp.concatenate(parts, axis=0))
    pv = jnp.dot(v_tile, jnp.concatenate(ps, axis=1), preferred_element_type=F32)
    acc = jnp.concatenate(alphas, axis=1) * acc + pv
    return tuple(ms), tuple(ls), acc


N_CHUNK = GROUP * TQ // LANES


def _flash_init():
    return (tuple(jnp.full((1, LANES), NEG_INF, F32) for _ in range(N_CHUNK)),
            tuple(jnp.zeros((1, LANES), F32) for _ in range(N_CHUNK)),
            jnp.zeros((LANES, GROUP * TQ), F32))


def _flash_out(state):
    _, l, acc = state
    return acc * (1.0 / jnp.maximum(jnp.concatenate(l, axis=1), TINY))


def _attn_prompt_kernel(q_ref, gate_ref, z_ref, kcvc_ref, ks_ref, vs_ref, kw_ref, vw_ref, nb_ref, cb_ref, fb_ref,
                        cover_ref, out_ref, *, n_slc):
    qt = pl.program_id(1)
    nc = kcvc_ref.shape[1]
    ki = lax.broadcasted_iota(jnp.int32, (TK, LANES), 0)
    qi = lax.broadcasted_iota(jnp.int32, (TK, LANES), 1)
    causal = lambda c: qi + c * LANES >= ki
    causal_keys = lambda c: (0, (c + 1) * LANES)
    window_edge = lambda c: (ki > qi + c * LANES) & (qt >= 2)
    edge_keys = lambda c: (c * LANES, TK)
    q_t = q_ref[...].T
    g_t = _sigmoid(gate_ref[...]).T
    kc = kcvc_ref[0, :, 0:LANES].astype(BF16)
    vc_t = kcvc_ref[0, :, LANES:2 * LANES].T.astype(BF16)
    c_end = lax.broadcasted_iota(jnp.int32, (nc, TQ), 0) * CMP_STRIDE + (CMP_LEN - 1)
    cmask = qt * TQ + lax.broadcasted_iota(jnp.int32, (nc, TQ), 1) >= c_end
    blk_t = lax.broadcasted_iota(jnp.int32, (n_slc, TQ), 0)
    cur_t = jnp.right_shift(qt * TQ + lax.broadcasted_iota(jnp.int32, (n_slc, TQ), 1), SLC_SHIFT)
    allowed_t = blk_t <= cur_t
    forced_t = (blk_t == 0) | (blk_t == cur_t) | (blk_t == cur_t - 1)
    e_row = jnp.right_shift(lax.broadcasted_iota(jnp.int32, (TK, LANES), 0), SLC_SHIFT)
    e_col = lax.broadcasted_iota(jnp.int32, (TK, LANES), 1)
    kt_prev = jnp.maximum(qt - 1, 0)
    kt_far = jnp.maximum(qt - 2, 0)
    zero_half = jnp.zeros((HEAD_DIM, TQ), F32)

    def kv_tile(k_ref, v_ref, kt):
        start = pl.multiple_of(kt * TK, TK)
        return k_ref[pl.ds(start, TK), :], v_ref[0, :, pl.ds(start, TK)]

    pieces = []
    for g in range(KV_HEADS):
        heads = [GROUP * g + r for r in range(GROUP)]
        parts = []
        for h in heads:
            x = q_t[h * HEAD_DIM:(h + 1) * HEAD_DIM, :] * (SCALE * LOG2E)
            parts.append(jnp.concatenate([x, zero_half] if g == 0 else [zero_half, x], axis=0))
        qg = jnp.concatenate(parts, axis=1).astype(BF16)
        s_c = jnp.dot(kc, qg, preferred_element_type=F32)
        p_parts, p_sum = [], None
        for r, h in enumerate(heads):
            s_r = jnp.where(cmask, s_c[:, r * TQ:(r + 1) * TQ] + cb_ref[0, h], NEG_INF)
            e = jnp.where(cmask, jnp.exp2(s_r - jnp.max(s_r, axis=0, keepdims=True)), 0.0)
            p_r = e * (1.0 / jnp.maximum(jnp.sum(e, axis=0, keepdims=True), TINY))
            p_sum = p_r if p_sum is None else p_sum + p_r
            p_parts.append(p_r.astype(BF16))
        o_cmp = jnp.dot(vc_t, jnp.concatenate(p_parts, axis=1), preferred_element_type=F32)
        hi = p_sum.astype(BF16)
        lo = (p_sum - hi.astype(F32)).astype(BF16)
        imp_t = (jnp.dot(cover_ref[...], hi, preferred_element_type=F32)
                 + jnp.dot(cover_ref[...], lo, preferred_element_type=F32))
        score_t = jnp.where(allowed_t, jnp.where(forced_t, FORCED_SCORE, imp_t[0:n_slc]), MASKED_SCORE)
        sel_t = _select_blocks_t(score_t, allowed_t, n_slc, min(TOP_N, n_slc)).astype(F32)
        sel_pad = jnp.concatenate([sel_t, jnp.zeros((LANES - n_slc, TQ), F32)], axis=0).astype(BF16)

        def sel_mask(kt, extra=None):
            expand = (e_row + kt * (TK // SLC_BLOCK) == e_col).astype(BF16)
            hit = jnp.dot(expand, sel_pad, preferred_element_type=F32)

            def mask(c):
                m = hit[:, c * LANES:(c + 1) * LANES] > 0.5
                return m if extra is None else m & extra(c)

            return mask

        far_bias = lambda r, qs, ks: fb_ref[heads[r], 0:1, qs]
        prev_bias = lambda r, qs, ks: nb_ref[heads[r], ks[0]:ks[1], qs]
        diag_bias = lambda r, qs, ks: nb_ref[heads[r], TK + ks[0]:TK + ks[1], qs]

        def far_body(kt, state):
            k_t, v_t = kv_tile(ks_ref, vs_ref, kt)
            return _flash_step(qg, k_t, v_t, far_bias, sel_mask(kt), state)

        state = lax.fori_loop(0, jnp.maximum(qt - 1, 0), far_body, _flash_init())
        k_t, v_t = kv_tile(ks_ref, vs_ref, kt_prev)
        state = _flash_step(qg, k_t, v_t, prev_bias, sel_mask(kt_prev, lambda c: qt >= 1), state)
        k_t, v_t = kv_tile(ks_ref, vs_ref, qt)
        state = _flash_step(qg, k_t, v_t, diag_bias, sel_mask(qt, causal), state, causal_keys)
        o_slc = _flash_out(state)
        state = _flash_init()
        k_t, v_t = kv_tile(kw_ref, vw_ref, kt_far)
        state = _flash_step(qg, k_t, v_t, far_bias, window_edge, state, edge_keys)
        k_t, v_t = kv_tile(kw_ref, vw_ref, kt_prev)
        state = _flash_step(qg, k_t, v_t, prev_bias, lambda c: qt >= 1, state)
        k_t, v_t = kv_tile(kw_ref, vw_ref, qt)
        state = _flash_step(qg, k_t, v_t, diag_bias, causal, state, causal_keys)
        o_win = _flash_out(state)
        rows = slice(g * HEAD_DIM, (g + 1) * HEAD_DIM)
        for r, h in enumerate(heads):
            cols = slice(r * TQ, (r + 1) * TQ)
            pieces.append(g_t[h:h + 1, :] * o_cmp[rows, cols]
                          + g_t[N_HEADS + h:N_HEADS + h + 1, :] * o_slc[rows, cols]
                          + g_t[2 * N_HEADS + h:2 * N_HEADS + h + 1, :] * o_win[rows, cols])
    out_ref[...] = jnp.concatenate(pieces, axis=0).T * _silu(z_ref[...])


def _attn_prompt(q2d, gate2d, z2d, kcvc, ks, vs_t, kw, vw_t, nb, cb, fb, cover_t, *, bsz, seq):
    assert WINDOW == 2 * TK and seq % TQ == 0
    n_qt = seq // TQ
    nc = seq // CMP_STRIDE
    n_slc = seq // SLC_BLOCK
    kern = functools.partial(_attn_prompt_kernel, n_slc=n_slc)
    tok = lambda w: pl.BlockSpec((TQ, w), lambda b, t: (b * n_qt + t, 0))
    k_spec = pl.BlockSpec((seq, LANES), lambda b, t: (b, 0))
    v_spec = pl.BlockSpec((1, LANES, seq), lambda b, t: (b, 0, 0))
    return pl.pallas_call(
        kern, grid=(bsz, n_qt),
        in_specs=[tok(D_ATTN), tok(LANES), tok(D_ATTN),
                  pl.BlockSpec((1, nc, KV_DIM), lambda b, t: (b, 0, 0)), k_spec, v_spec, k_spec, v_spec,
                  pl.BlockSpec((N_HEADS, 2 * TK, TQ), lambda b, t: (0, 0, 0)),
                  pl.BlockSpec((1, N_HEADS, nc, TQ), lambda b, t: (t, 0, 0, 0)),
                  pl.BlockSpec((N_HEADS, SUBLANES, TQ), lambda b, t: (0, 0, 0)),
                  pl.BlockSpec((LANES, nc), lambda b, t: (0, 0))],
        out_specs=tok(D_ATTN),
        out_shape=jax.ShapeDtypeStruct((bsz * seq, D_ATTN), F32),
        name="attn_prompt",
        compiler_params=pltpu.CompilerParams(dimension_semantics=("arbitrary", "arbitrary"),
                                             vmem_limit_bytes=VMEM_LIMIT),
    )(q2d, gate2d, z2d, kcvc, ks, vs_t, kw, vw_t, nb, cb, fb, cover_t)


def _attn_sample_kernel(pt_ref, *refs, n_pages, nq, n_seq, aliased_state):
    del pt_ref
    shared = list(refs[n_seq * n_pages:])
    if aliased_state:
        del shared[-3]
    for i in range(n_seq):
        _attn_sample_one(i, refs[i * n_pages:(i + 1) * n_pages], *shared, n_pages=n_pages, nq=nq)


def _attn_sample_one(i, pages, win_ref, kvsn_ref, kvwn_ref, q_ref, gate_ref, z_ref, kcvc_ref, sba_ref, swa_ref,
                     sbb_ref, sc_ref, cover_ref, expand_ref, out_ref, nwin_ref, *, n_pages, nq):
    past = n_pages * PAGE_SIZE
    wbuf = win_ref.shape[-1]
    nc = kcvc_ref.shape[1]
    rows = KV_HEADS * GROUP * nq
    cur = past // SLC_BLOCK
    n_slc = cur + 1
    low = lax.broadcasted_iota(jnp.int32, (nq, LANES), 1) < HALF
    q = q_ref[i]
    q_left = (jnp.concatenate(_head_rows(q, 0, low) + _head_rows(q, 1, low), axis=0) * SCALE).astype(BF16)
    qi = lax.broadcasted_iota(jnp.int32, (rows, LANES), 0) & (nq - 1)
    ki = lax.broadcasted_iota(jnp.int32, (rows, LANES), 1)
    new_mask = (ki <= qi) & (ki < nq)
    pad_rows = jnp.zeros((LANES - nq, LANES), F32)

    def new_tile(ref, t):
        return jnp.concatenate([ref[i, :, t * LANES:(t + 1) * LANES], pad_rows], axis=0)

    kc = kcvc_ref[i, :, 0:LANES].astype(BF16)
    vc = kcvc_ref[i, :, LANES:2 * LANES].astype(BF16)
    n_idx = lax.broadcasted_iota(jnp.int32, (rows, nc), 1)
    p_c = _masked_softmax(_dot_nt(q_left, kc) + sc_ref[...], n_idx < nc - 1)
    o_cmp = _dot(p_c, vc)
    blk = lax.broadcasted_iota(jnp.int32, (nq, LANES), 1)
    is_blk = blk < n_slc
    forced = (blk == 0) | (blk == cur) | (blk == cur - 1)
    sel_rows = []
    for g in range(KV_HEADS):
        p_sum = p_c[g * GROUP * nq:g * GROUP * nq + nq]
        for r in range(1, GROUP):
            p_sum = p_sum + p_c[(g * GROUP + r) * nq:(g * GROUP + r + 1) * nq]
        imp = _dot_split(p_sum, cover_ref[...])
        score = jnp.where(is_blk, jnp.where(forced, FORCED_SCORE, imp), PAD_SCORE)
        cnt = jnp.zeros((nq, LANES), jnp.int32)
        for j in range(n_slc):
            col = score[:, j:j + 1]
            cnt = cnt + ((col > score) | ((col == score) & (blk > j))).astype(jnp.int32)
        sel_g = ((cnt < min(TOP_N, n_slc)) & is_blk).astype(F32)
        sel_rows += [sel_g] * GROUP
    sel = jnp.concatenate(sel_rows, axis=0)
    k_pages = [pages[p][0, 0, 0].reshape(LANES, PAGE_SIZE).astype(BF16) for p in range(n_pages)]
    v_pages = [pages[p][0, 0, 1].reshape(LANES, PAGE_SIZE).astype(BF16) for p in range(n_pages)]
    k_new = new_tile(kvsn_ref, 0).astype(BF16)
    v_new = new_tile(kvsn_ref, 1).astype(BF16)
    s_a = jnp.concatenate([_dot(q_left, k) for k in k_pages], axis=1) + sba_ref[...]
    mask_a = jnp.dot(sel.astype(BF16), expand_ref[...], preferred_element_type=F32) > 0.5
    s_b = _dot_nt(q_left, k_new) + sbb_ref[...]
    mask_b = new_mask & (sel[:, cur:cur + 1] > 0.5)
    s_a = jnp.where(mask_a, s_a, NEG_INF)
    s_b = jnp.where(mask_b, s_b, NEG_INF)
    m = jnp.maximum(jnp.max(s_a, axis=-1, keepdims=True), jnp.max(s_b, axis=-1, keepdims=True))
    p_a = jnp.exp(s_a - m) * mask_a.astype(F32)
    p_b = jnp.exp(s_b - m) * mask_b.astype(F32)
    l = jnp.sum(p_a, axis=-1, keepdims=True) + jnp.sum(p_b, axis=-1, keepdims=True)
    acc = _dot(p_b, v_new)
    for p in range(n_pages):
        acc = acc + _dot_nt(p_a[:, p * PAGE_SIZE:(p + 1) * PAGE_SIZE], v_pages[p])
    o_slc = acc / jnp.maximum(l, TINY)
    win_t = [win_ref[0, i, t].reshape(LANES, wbuf) for t in range(2)]
    new_w = [new_tile(kvwn_ref, t) for t in range(2)]
    lane_w = lax.broadcasted_iota(jnp.int32, (LANES, wbuf), 1)
    for t in range(2):
        placed = jnp.concatenate([jnp.zeros((LANES, wbuf - LANES), F32), pltpu.roll(new_w[t].T, LANES - nq, 1)],
                                 axis=1)
        shifted = pltpu.roll(win_t[t], wbuf - nq, 1)
        nwin_ref[0, i, t] = jnp.where(lane_w < wbuf - nq, shifted, placed).reshape(KV_HEADS, HEAD_DIM, wbuf)
    kw_t, vw_t = win_t[0].astype(BF16), win_t[1].astype(BF16)
    kw_new, vw_new = new_w[0].astype(BF16), new_w[1].astype(BF16)
    jw = lax.broadcasted_iota(jnp.int32, (rows, wbuf), 1)
    qw = lax.broadcasted_iota(jnp.int32, (rows, wbuf), 0) & (nq - 1)
    mask_wa = jw > qw
    s_wa = jnp.where(mask_wa, _dot(q_left, kw_t) + swa_ref[...], NEG_INF)
    s_wb = jnp.where(new_mask, _dot_nt(q_left, kw_new) + sbb_ref[...], NEG_INF)
    m = jnp.maximum(jnp.max(s_wa, axis=-1, keepdims=True), jnp.max(s_wb, axis=-1, keepdims=True))
    p_wa = jnp.exp(s_wa - m) * mask_wa.astype(F32)
    p_wb = jnp.exp(s_wb - m) * new_mask.astype(F32)
    l = jnp.sum(p_wa, axis=-1, keepdims=True) + jnp.sum(p_wb, axis=-1, keepdims=True)
    acc = _dot_nt(p_wa, vw_t) + _dot(p_wb, vw_new)
    o_win = acc / jnp.maximum(l, TINY)
    gsig = _sigmoid(gate_ref[i])
    o_heads = []
    for h in range(N_HEADS):
        rs = slice(h * nq, (h + 1) * nq)
        o_heads.append(gsig[:, h:h + 1] * o_cmp[rs] + gsig[:, N_HEADS + h:N_HEADS + h + 1] * o_slc[rs]
                       + gsig[:, 2 * N_HEADS + h:2 * N_HEADS + h + 1] * o_win[rs])
    out_ref[i] = _assemble_heads(o_heads, low) * _silu(z_ref[i])


def _attn_sample(cache_t, win_t, layer, table, kvs_new, kvw_new, q, gate, z, kcvc, sba, swa, sbb, sc, cover, expand,
                 new_state=None):
    bsz, n_pages = table.shape
    nq = q.shape[1]
    wbuf = win_t.shape[-1]
    nc = kcvc.shape[1]
    past = n_pages * PAGE_SIZE
    rows = N_HEADS * nq
    assert nq <= SLC_BLOCK and nq & (nq - 1) == 0 and past % SLC_BLOCK == 0 and wbuf == WINDOW
    n_seq = math.gcd(bsz, SEQS_PER_STEP)
    aliased_state = new_state is not None
    kern = functools.partial(_attn_sample_kernel, n_pages=n_pages, nq=nq, n_seq=n_seq, aliased_state=aliased_state)
    page_specs = [pl.BlockSpec((1, 1, 2, KV_HEADS, HEAD_DIM, PAGE_SIZE),
                               lambda b, pt, i=i, p=p: (layer, pt[b * n_seq + i, p], 0, 0, 0, 0))
                  for i in range(n_seq) for p in range(n_pages)]
    per_b = lambda r, w: pl.BlockSpec((n_seq, r, w), lambda b, pt: (b, 0, 0))
    const = lambda r, w: pl.BlockSpec((r, w), lambda b, pt: (0, 0))
    state_spec = pl.BlockSpec((1, n_seq, 2, KV_HEADS, HEAD_DIM, wbuf), lambda b, pt: (layer, b, 0, 0, 0, 0))
    in_specs = page_specs + [
        state_spec, per_b(nq, KV_DIM), per_b(nq, KV_DIM), per_b(nq, D_ATTN), per_b(nq, LANES),
        per_b(nq, D_ATTN), per_b(nc, KV_DIM),
        const(rows, past), const(rows, wbuf), const(rows, LANES), const(rows, nc),
        const(nc, LANES), const(LANES, past)]
    args = [table, *([cache_t] * (n_seq * n_pages)), win_t, kvs_new, kvw_new, q, gate, z, kcvc, sba, swa, sbb, sc,
            cover, expand]
    aliases = {}
    if aliased_state:
        in_specs.append(pl.BlockSpec(memory_space=pl.ANY))
        aliases = {len(args): 1}
        args.append(new_state)
    grid_spec = pltpu.PrefetchScalarGridSpec(
        num_scalar_prefetch=1, grid=(bsz // n_seq,), in_specs=in_specs,
        out_specs=[per_b(nq, D_ATTN), state_spec])
    return pl.pallas_call(
        kern, grid_spec=grid_spec,
        out_shape=[jax.ShapeDtypeStruct((bsz, nq, D_ATTN), F32),
                   jax.ShapeDtypeStruct(win_t.shape, F32)],
        input_output_aliases=aliases, name="attn_sample",
        compiler_params=pltpu.CompilerParams(dimension_semantics=("arbitrary",), vmem_limit_bytes=VMEM_LIMIT),
    )(*args)


def _out_proj_kernel(h_ref, conv_ref, attn_ref, ple_ref, wo_ref, wg_ref, wp_ref, fg_ref, out_ref, *, final):
    h = h_ref[...]
    h = h + jnp.dot(conv_ref[...].astype(BF16), wo_ref[0:D_CONV, :], preferred_element_type=F32)
    h = h + jnp.dot(attn_ref[...].astype(BF16), wo_ref[D_CONV:D_CONV + D_ATTN, :], preferred_element_type=F32)
    gate = _sigmoid(jnp.dot(h.astype(BF16), wg_ref[...], preferred_element_type=F32))
    h = h + gate * jnp.dot(ple_ref[0].astype(BF16), wp_ref[...], preferred_element_type=F32)
    if final:
        h = h * lax.rsqrt(jnp.mean(h * h, axis=-1, keepdims=True) + EPS) * fg_ref[...]
    out_ref[...] = h


def _out_proj(h2d, conv2d, attn2d, ple3d, layer, wo, wg, wp, fg, *, final):
    n = h2d.shape[0]
    tm = min(TM, n)
    ple_dim = ple3d.shape[-1]
    kern = functools.partial(_out_proj_kernel, final=final)
    tok = lambda w: pl.BlockSpec((tm, w), lambda i: (i, 0))
    const = lambda r, w: pl.BlockSpec((r, w), lambda i: (0, 0))
    return pl.pallas_call(
        kern, grid=(n // tm,),
        in_specs=[tok(D_MODEL), tok(D_CONV), tok(D_ATTN), pl.BlockSpec((1, tm, ple_dim), lambda i: (layer, i, 0)),
                  const(D_CONV + D_ATTN, D_MODEL), const(D_MODEL, D_MODEL), const(ple_dim, D_MODEL),
                  const(1, D_MODEL)],
        out_specs=tok(D_MODEL),
        out_shape=jax.ShapeDtypeStruct((n, D_MODEL), F32),
        name="out_proj",
        compiler_params=pltpu.CompilerParams(dimension_semantics=("arbitrary",), vmem_limit_bytes=VMEM_LIMIT),
    )(h2d, conv2d, attn2d, ple3d, wo, wg, wp, fg)


def _cover_matrix(n_cmp_rows, n_cmp, n_slc):
    c_start = np.arange(n_cmp_rows) * CMP_STRIDE
    c_end = c_start + CMP_LEN - 1
    s_start = np.arange(LANES) * SLC_BLOCK
    cover = (c_start[:, None] < s_start[None, :] + SLC_BLOCK) & (c_end[:, None] >= s_start[None, :])
    cover &= (np.arange(n_cmp_rows)[:, None] < n_cmp) & (np.arange(LANES)[None, :] < n_slc)
    return jnp.asarray(cover, dtype=BF16)


def _expand_matrix(past):
    e = np.arange(LANES)[:, None] == (np.arange(past)[None, :] // SLC_BLOCK)
    return jnp.asarray(e, dtype=BF16)


def kernel(x_prompt, x_sample, cache_cmp_kv, cache_slc_kv, page_table, state_win_kv, state_conv, p_prompt, p_sample, norm_g, w_in, conv_w, conv_b, conv_ln_g, conv_ln_b, cmp_pe, cmp_w1, cmp_w2, w_out, w_ple, w_ple_gate, rel_bias, final_norm_g):
    bp, seq, _ = x_prompt.shape
    bs, nq, _ = x_sample.shape
    depth = w_in.shape[0]
    n_pages = page_table.shape[1]
    past = n_pages * PAGE_SIZE
    wbuf = state_win_kv.shape[2]
    win_p = min(WINDOW, seq)

    nb, cb, fb, sba, swa, sbb, sc = _bias_tables(rel_bias, seq=seq, past=past, wbuf=wbuf, nq_s=nq)
    rows = N_HEADS * nq
    sba, swa, sbb, sc = (a.reshape(rows, a.shape[-1]) for a in (sba, swa, sbb, sc))
    nc_p, nc_s = seq // CMP_STRIDE, past // CMP_STRIDE
    cover_p = _cover_matrix(nc_p, nc_p - 1, seq // SLC_BLOCK)
    cover_s = _cover_matrix(nc_s, nc_s - 1, past // SLC_BLOCK + 1)
    expand_s = _expand_matrix(past)
    conv_zero = jnp.zeros((bp, CONV_HIST, D_CONV), F32)
    fg = final_norm_g.reshape(1, D_MODEL)
    to_t = lambda a: jnp.transpose(a, (0, 1, 3, 4, 5, 2))
    from_t = lambda a: jnp.transpose(a, (0, 1, 5, 2, 3, 4))
    cmp_t, slc_t, win_t = to_t(cache_cmp_kv), to_t(cache_slc_kv), to_t(state_win_kv)
    ple_p = p_prompt.reshape(depth, bp * seq, -1)
    ple_s = p_sample.reshape(depth, bs * nq, -1)

    hp = x_prompt.reshape(bp * seq, D_MODEL)
    hs = x_sample.reshape(bs * nq, D_MODEL)
    outs = [[] for _ in range(8)]
    new_win = kv_prev = None
    for i in range(depth):
        w_pad = jnp.pad(w_in[i], ((0, 0), (0, D_IN_PAD - D_IN))).astype(BF16)
        wkv_t = w_in[i][:, SEC_KVC[0]:SEC_KVW[1]].T.astype(BF16)
        g = norm_g[i].reshape(1, D_MODEL)
        w1 = jnp.transpose(cmp_w1[i].reshape(2, 2, CMP_STRIDE, HEAD_DIM, CMP_HIDDEN), (0, 2, 3, 1, 4))
        w1 = w1.reshape(2, CMP_STRIDE, HEAD_DIM, 2 * CMP_HIDDEN)
        w1cat = jnp.stack([jnp.concatenate([w1, jnp.zeros_like(w1)], axis=-1),
                           jnp.concatenate([jnp.zeros_like(w1), w1], axis=-1)], axis=2)
        w1cat = w1cat.reshape(2, KV_HEADS * CMP_FEAT, KV_HEADS * 2 * CMP_HIDDEN).astype(BF16)
        pe = cmp_pe[i].reshape(2, 2, CMP_STRIDE, 1, HEAD_DIM)
        pe = jnp.broadcast_to(pe, (2, 2, CMP_STRIDE, KV_HEADS, HEAD_DIM)).reshape(2, 2, KV_HEADS * CMP_FEAT)
        pe_pad = jnp.pad(pe, ((0, 0), (0, SUBLANES - 2), (0, 0)))
        w2big = jnp.zeros((N_TG * CMP_HIDDEN, KV_DIM), F32)
        for tg in range(N_TG):
            w2big = w2big.at[tg * CMP_HIDDEN:(tg + 1) * CMP_HIDDEN, tg * HEAD_DIM:(tg + 1) * HEAD_DIM].set(
                cmp_w2[i, tg // KV_HEADS])
        w2big = w2big.astype(BF16)
        wo, wg, wp = w_out[i].astype(BF16), w_ple_gate[i].astype(BF16), w_ple[i].astype(BF16)
        cw, cbias = conv_w[i], conv_b[i].reshape(1, D_CONV)
        lg, lb = conv_ln_g[i].reshape(1, D_CONV), conv_ln_b[i].reshape(1, D_CONV)
        final = i == depth - 1

        conv_out, q, kvc_t, kvs_t, kvw_t, z, gate, ks, vs_t, kw, vw_t, new_conv = _in_proj(
            hp, g, w_pad, wkv_t, (conv_zero, cw, cbias, lg, lb), seq=seq, layer=i, depth=depth, kv_prev=kv_prev)
        kv_prev = (kvc_t, kvs_t, kvw_t)
        kcvc = _compress_seq(kvc_t, i, pe_pad, w1cat, w2big)
        attn = _attn_prompt(q, gate, z, kcvc, ks, vs_t, kw, vw_t, nb, cb, fb, cover_p.T, bsz=bp, seq=seq)
        hp = _out_proj(hp, conv_out.reshape(bp * seq, D_CONV), attn, ple_p, i, wo, wg, wp, fg, final=final)
        outs[6].append(new_conv)

        c3, q, kvc, kvs, kvw, z, gate = _in_proj(hs, g, w_pad)
        conv_out, new_conv = _conv_module(c3.reshape(bs, nq, 3 * D_CONV), state_conv[i], cw, cbias, lg, lb)
        kcvc = _compress_paged(cmp_t, i, page_table, pe_pad, w1cat, w2big)
        attn, new_win = _attn_sample(
            slc_t, win_t, i, page_table, kvs.reshape(bs, nq, KV_DIM), kvw.reshape(bs, nq, KV_DIM),
            q.reshape(bs, nq, D_ATTN), gate.reshape(bs, nq, LANES), z.reshape(bs, nq, D_ATTN), kcvc,
            sba, swa, sbb, sc, cover_s, expand_s, new_state=new_win)
        hs = _out_proj(hs, conv_out.reshape(bs * nq, D_CONV), attn.reshape(bs * nq, D_ATTN), ple_s, i,
                       wo, wg, wp, fg, final=final)
        outs[1].append(kvc.reshape(bs, nq, 2, KV_HEADS, HEAD_DIM))
        outs[3].append(kvs.reshape(bs, nq, 2, KV_HEADS, HEAD_DIM))
        outs[7].append(new_conv)

    six_d = lambda a: a.reshape(depth, bp, 2, KV_HEADS, HEAD_DIM, a.shape[-1])
    kvc_t, kvs_t, kvw_t = kv_prev
    whole = {0: six_d(kvc_t), 2: six_d(kvs_t), 4: six_d(kvw_t[..., seq - win_p:]), 5: new_win}
    stacked = [whole[k] if k in whole else jnp.stack(o) for k, o in enumerate(outs)]
    for k in (0, 2, 4, 5):
        stacked[k] = from_t(stacked[k])
    return (hp.reshape(bp, seq, D_MODEL), hs.reshape(bs, nq, D_MODEL)) + tuple(stacked)
```

```python
import functools
import math

import numpy as np
import jax
import jax.numpy as jnp
from jax import lax
from jax.experimental import pallas as pl
from jax.experimental.pallas import tpu as pltpu

F32 = jnp.float32
BF16 = jnp.bfloat16

D_MODEL = 1024
D_CONV = 512
CONV_WIDTH = 31
CONV_HIST = CONV_WIDTH - 1
HEAD_DIM = 64
N_HEADS = 8
KV_HEADS = 2
GROUP = N_HEADS // KV_HEADS
D_ATTN = N_HEADS * HEAD_DIM
KV_DIM = 2 * KV_HEADS * HEAD_DIM
N_BRANCH = 3
CMP_STRIDE = 16
CMP_LEN = 2 * CMP_STRIDE
CMP_HIDDEN = 128
SLC_BLOCK = 64
SLC_SHIFT = 6
TOP_N = 8
WINDOW = 512
NUM_BUCKETS = 32
MAX_DISTANCE = 128
PAGE_SIZE = 128
EPS = 1e-6
NEG_INF = -1e30
FORCED_SCORE = 1e4
MASKED_SCORE = -1e4
PAD_SCORE = -3e4
TINY = 1e-30
SCALE = HEAD_DIM ** -0.5
LOG2E = math.log2(math.e)

LANES = 128
SUBLANES = 8
HALF = LANES // 2
TQ = 256
TK = 256
TM = 512
VMEM_LIMIT = 56 * 1024 * 1024

SEC_CONV = (0, 3 * D_CONV)
SEC_Q = (SEC_CONV[1], SEC_CONV[1] + D_ATTN)
SEC_KVC = (SEC_Q[1], SEC_Q[1] + KV_DIM)
SEC_KVS = (SEC_KVC[1], SEC_KVC[1] + KV_DIM)
SEC_KVW = (SEC_KVS[1], SEC_KVS[1] + KV_DIM)
SEC_Z = (SEC_KVW[1], SEC_KVW[1] + D_ATTN)
SEC_GATE = (SEC_Z[1], SEC_Z[1] + LANES)
D_IN = SEC_Z[1] + N_BRANCH * N_HEADS
D_IN_PAD = SEC_GATE[1]
SECTIONS = (SEC_CONV, SEC_Q, SEC_KVC, SEC_KVS, SEC_KVW, SEC_Z, SEC_GATE)


def _bucket_lower_bounds():
    n = np.arange(0, 4 * MAX_DISTANCE, dtype=np.int64)
    max_exact = NUM_BUCKETS // 2
    nf = np.maximum(n, 1).astype(np.float32)
    large = max_exact + (np.log(nf / np.float32(max_exact)) / np.float32(math.log(MAX_DISTANCE / max_exact))
                         * np.float32(NUM_BUCKETS - max_exact)).astype(np.int32)
    large = np.minimum(large, NUM_BUCKETS - 1)
    bucket = np.where(n < max_exact, n, large)
    return [int(np.argmax(bucket >= b)) for b in range(NUM_BUCKETS)]


BUCKET_LOWER = _bucket_lower_bounds()


def _dot(a, b):
    return jnp.dot(a.astype(BF16), b.astype(BF16), preferred_element_type=F32)


def _dot_nt(a, b):
    return lax.dot_general(a.astype(BF16), b.astype(BF16), (((1,), (1,)), ((), ())),
                           preferred_element_type=F32)


def _dot_split(a, b):
    hi = a.astype(BF16)
    lo = (a - hi.astype(F32)).astype(BF16)
    return (jnp.dot(hi, b, preferred_element_type=F32) + jnp.dot(lo, b, preferred_element_type=F32))


def _sigmoid(x):
    return 1.0 / (1.0 + jnp.exp(-x))


def _silu(x):
    return x * _sigmoid(x)


def _masked_softmax(s, mask):
    s = jnp.where(mask, s, NEG_INF)
    m = jnp.max(s, axis=-1, keepdims=True)
    e = jnp.exp(s - m) * mask.astype(F32)
    l = jnp.sum(e, axis=-1, keepdims=True)
    return e / jnp.maximum(l, TINY)


def _bias_of(dist, rb_ref, h):
    out = jnp.full(dist.shape, rb_ref[0, h], F32)
    for b in range(1, NUM_BUCKETS):
        out = jnp.where(dist >= BUCKET_LOWER[b], rb_ref[b, h], out)
    return out


def _bias_kernel(rb_ref, nb_ref, cb_ref, fb_ref, sba_ref, swa_ref, sbb_ref, sc_ref, *, n_qt, nc_p, past, wbuf, nq_s):
    h = pl.program_id(0)
    c = lax.broadcasted_iota(jnp.int32, (2 * TK, TQ), 0)
    qi = lax.broadcasted_iota(jnp.int32, (2 * TK, TQ), 1)
    nb_ref[0] = _bias_of(qi + TK - c, rb_ref, h) * LOG2E
    nn = lax.broadcasted_iota(jnp.int32, (nc_p, TQ), 0)
    qn = lax.broadcasted_iota(jnp.int32, (nc_p, TQ), 1)
    for t in range(n_qt):
        cb_ref[t, 0] = _bias_of(t * TQ + qn - (nn * CMP_STRIDE + CMP_LEN - 1), rb_ref, h) * LOG2E
    fb_ref[0] = jnp.full(fb_ref.shape[1:], rb_ref[NUM_BUCKETS - 1, h] * LOG2E, F32)
    qs = lax.broadcasted_iota(jnp.int32, (nq_s, past), 0)
    ks = lax.broadcasted_iota(jnp.int32, (nq_s, past), 1)
    sba_ref[0] = _bias_of(past + qs - ks, rb_ref, h)
    qs = lax.broadcasted_iota(jnp.int32, (nq_s, wbuf), 0)
    ks = lax.broadcasted_iota(jnp.int32, (nq_s, wbuf), 1)
    swa_ref[0] = _bias_of(wbuf + qs - ks, rb_ref, h)
    qs = lax.broadcasted_iota(jnp.int32, (nq_s, LANES), 0)
    ks = lax.broadcasted_iota(jnp.int32, (nq_s, LANES), 1)
    sbb_ref[0] = _bias_of(qs - ks, rb_ref, h)
    nc_s = sc_ref.shape[2]
    qs = lax.broadcasted_iota(jnp.int32, (nq_s, nc_s), 0)
    ks = lax.broadcasted_iota(jnp.int32, (nq_s, nc_s), 1)
    sc_ref[0] = _bias_of(past + qs - (ks * CMP_STRIDE + CMP_LEN - 1), rb_ref, h)


def _bias_tables(rel_bias, *, seq, past, wbuf, nq_s):
    n_qt = seq // TQ
    nc_p = seq // CMP_STRIDE
    nc_s = past // CMP_STRIDE
    kern = functools.partial(_bias_kernel, n_qt=n_qt, nc_p=nc_p, past=past, wbuf=wbuf, nq_s=nq_s)
    shapes = (
        jax.ShapeDtypeStruct((N_HEADS, 2 * TK, TQ), F32),
        jax.ShapeDtypeStruct((n_qt, N_HEADS, nc_p, TQ), F32),
        jax.ShapeDtypeStruct((N_HEADS, SUBLANES, TQ), F32),
        jax.ShapeDtypeStruct((N_HEADS, nq_s, past), F32),
        jax.ShapeDtypeStruct((N_HEADS, nq_s, wbuf), F32),
        jax.ShapeDtypeStruct((N_HEADS, nq_s, LANES), F32),
        jax.ShapeDtypeStruct((N_HEADS, nq_s, nc_s), F32),
    )
    out_specs = (
        pl.BlockSpec((1, 2 * TK, TQ), lambda h: (h, 0, 0)),
        pl.BlockSpec((n_qt, 1, nc_p, TQ), lambda h: (0, h, 0, 0)),
        pl.BlockSpec((1, SUBLANES, TQ), lambda h: (h, 0, 0)),
        pl.BlockSpec((1, nq_s, past), lambda h: (h, 0, 0)),
        pl.BlockSpec((1, nq_s, wbuf), lambda h: (h, 0, 0)),
        pl.BlockSpec((1, nq_s, LANES), lambda h: (h, 0, 0)),
        pl.BlockSpec((1, nq_s, nc_s), lambda h: (h, 0, 0)),
    )
    return pl.pallas_call(
        kern, grid=(N_HEADS,),
        in_specs=[pl.BlockSpec(memory_space=pltpu.SMEM)],
        out_specs=out_specs, out_shape=shapes, name="bias_tables",
        compiler_params=pltpu.CompilerParams(dimension_semantics=("arbitrary",)),
    )(rel_bias)


KV_SECTIONS = (SEC_KVC, SEC_KVS, SEC_KVW)
ATTN_KV_SECTIONS = (SEC_KVS, SEC_KVW)


def _in_proj_kernel(x_ref, g_ref, w_ref, *refs, kv_transposed, conv_tiles, n_aliased):
    x = x_ref[...]
    u = x * lax.rsqrt(jnp.mean(x * x, axis=-1, keepdims=True) + EPS) * g_ref[...]
    ub = u.astype(BF16)
    refs = list(refs)
    wkv_t_ref = refs.pop(0) if kv_transposed else None
    conv_in = [refs.pop(0) for _ in range(5)] if conv_tiles else None
    del refs[:n_aliased]
    out_refs = [refs.pop(0) for _ in SECTIONS]
    attn_refs = [refs.pop(0) for _ in range(2 * len(ATTN_KV_SECTIONS))] if kv_transposed else None

    def project(ref, sec):
        if kv_transposed and sec in KV_SECTIONS:
            k = KV_SECTIONS.index(sec)
            kv_t = lax.dot_general(wkv_t_ref[k * KV_DIM:(k + 1) * KV_DIM, :], ub, (((1,), (1,)), ((), ())),
                                   preferred_element_type=F32)
            ref[0, 0] = kv_t
            if sec in ATTN_KV_SECTIONS:
                j = ATTN_KV_SECTIONS.index(sec)
                attn_refs[2 * j][...] = kv_t[0:LANES, :].T.astype(BF16)
                attn_refs[2 * j + 1][0] = kv_t[LANES:2 * LANES, :].astype(BF16)
        else:
            ref[...] = jnp.dot(ub, w_ref[:, sec[0]:sec[1]], preferred_element_type=F32)

    jobs = [functools.partial(project, ref, sec) for ref, sec in zip(out_refs, SECTIONS)
            if not (conv_tiles and sec == SEC_CONV)]
    if not conv_tiles:
        for job in jobs:
            job()
        return
    hist_ref, cw_ref, cb_ref, lg_ref, lb_ref = conv_in
    new_ref, xbuf, sbuf = refs
    tm = x_ref.shape[0]
    t = pl.program_id(0) % conv_tiles
    c3 = jnp.dot(ub, w_ref[:, SEC_CONV[0]:SEC_CONV[1]], preferred_element_type=F32)
    pl.when(t == 0)(lambda: _conv_load_history(xbuf, hist_ref[0]))
    xbuf[HIST_PAD:HIST_PAD + tm, :] = c3[:, 0:D_CONV] * _sigmoid(c3[:, D_CONV:2 * D_CONV])
    acc = jnp.zeros((tm, D_CONV), F32) + cb_ref[...]
    for phase in range(SUBLANES):
        acc = _conv_phase(xbuf, sbuf, cw_ref, acc, phase, tm)
        if phase < len(jobs):
            jobs[phase]()
    for job in jobs[SUBLANES:]:
        job()
    out_refs[0][...] = _conv_finish(acc, c3[:, 2 * D_CONV:3 * D_CONV], lg_ref, lb_ref)
    _conv_carry(xbuf, new_ref, 0, t, conv_tiles, tm)


def _in_proj(x2d, g, w_pad, wkv_t=None, conv=None, *, seq=None, layer=0, depth=1, kv_prev=None):
    n = x2d.shape[0]
    tm = min(TM, n)
    kv_transposed = wkv_t is not None
    in_specs = [pl.BlockSpec((tm, D_MODEL), lambda i: (i, 0)),
                pl.BlockSpec((1, D_MODEL), lambda i: (0, 0)),
                pl.BlockSpec((D_MODEL, D_IN_PAD), lambda i: (0, 0))]
    args = [x2d, g, w_pad]
    out_specs, out_shape, scratch = [], [], []
    if kv_transposed:
        in_specs.append(pl.BlockSpec((len(KV_SECTIONS) * KV_DIM, D_MODEL), lambda i: (0, 0)))
        args.append(wkv_t)
        tiles_per_seq = seq // tm
    if conv is not None:
        assert kv_transposed and tm >= CONV_HIST
        vec = pl.BlockSpec((1, D_CONV), lambda i: (0, 0))
        in_specs += [pl.BlockSpec((1, CONV_HIST, D_CONV), lambda i: (i // tiles_per_seq, 0, 0)),
                     pl.BlockSpec((CONV_WIDTH, D_CONV), lambda i: (0, 0)), vec, vec, vec]
        args += list(conv)
        scratch = [pltpu.VMEM((HIST_PAD + tm, D_CONV), F32), pltpu.VMEM((HIST_PAD + tm, D_CONV), F32)]
    for sec in SECTIONS:
        w = sec[1] - sec[0]
        if conv is not None and sec == SEC_CONV:
            out_specs.append(pl.BlockSpec((tm, D_CONV), lambda i: (i, 0)))
            out_shape.append(jax.ShapeDtypeStruct((n, D_CONV), F32))
        elif kv_transposed and sec in KV_SECTIONS:
            out_specs.append(pl.BlockSpec((1, 1, w, tm),
                                          lambda i: (layer, i // tiles_per_seq, 0, i % tiles_per_seq)))
            out_shape.append(jax.ShapeDtypeStruct((depth, n // seq, w, seq), F32))
        else:
            out_specs.append(pl.BlockSpec((tm, w), lambda i: (i, 0)))
            out_shape.append(jax.ShapeDtypeStruct((n, w), F32))
    if kv_transposed:
        for _ in ATTN_KV_SECTIONS:
            out_specs.append(pl.BlockSpec((tm, LANES), lambda i: (i, 0)))
            out_shape.append(jax.ShapeDtypeStruct((n, LANES), BF16))
            out_specs.append(pl.BlockSpec((1, LANES, tm), lambda i: (i // tiles_per_seq, 0, i % tiles_per_seq)))
            out_shape.append(jax.ShapeDtypeStruct((n // seq, LANES, seq), BF16))
    if conv is not None:
        out_specs.append(pl.BlockSpec((1, CONV_HIST, D_CONV), lambda i: (i // tiles_per_seq, 0, 0)))
        out_shape.append(jax.ShapeDtypeStruct((n // seq, CONV_HIST, D_CONV), F32))
    aliases = {}
    if kv_prev is not None:
        first_out = SECTIONS.index(KV_SECTIONS[0])
        for k, prev in enumerate(kv_prev):
            aliases[len(args)] = first_out + k
            in_specs.append(pl.BlockSpec(memory_space=pl.ANY))
            args.append(prev)
    kern = functools.partial(_in_proj_kernel, kv_transposed=kv_transposed,
                             conv_tiles=tiles_per_seq if conv is not None else 0, n_aliased=len(aliases))
    return pl.pallas_call(
        kern, grid=(n // tm,), scratch_shapes=scratch, input_output_aliases=aliases,
        in_specs=in_specs, out_specs=out_specs, out_shape=out_shape, name="in_proj",
        compiler_params=pltpu.CompilerParams(dimension_semantics=("arbitrary",), vmem_limit_bytes=VMEM_LIMIT),
    )(*args)


HIST_PAD = 32


CONV_OFF = HIST_PAD - CONV_HIST


def _conv_load_history(xbuf, hist):
    xbuf[0:HIST_PAD, :] = jnp.zeros((HIST_PAD, D_CONV), F32)
    xbuf[CONV_OFF:HIST_PAD, :] = hist


def _conv_phase(xbuf, sbuf, w_ref, acc, phase, tt):
    a_max = (CONV_WIDTH - 1 - phase) // SUBLANES
    span = tt + SUBLANES * a_max
    sbuf[0:span, :] = xbuf[CONV_OFF + phase:CONV_OFF + phase + span, :]
    for a in range(a_max + 1):
        k = SUBLANES * a + phase
        acc = acc + sbuf[SUBLANES * a:SUBLANES * a + tt, :] * w_ref[k:k + 1, :]
    return acc


def _conv_finish(acc, z, lg_ref, lb_ref):
    mu = jnp.mean(acc, axis=-1, keepdims=True)
    xc = acc - mu
    var = jnp.mean(xc * xc, axis=-1, keepdims=True)
    y = xc * lax.rsqrt(var + EPS) * lg_ref[...] + lb_ref[...]
    return _silu(y) * _silu(z)


def _conv_carry(xbuf, new_ref, i, t, n_t, tt):
    tail = xbuf[CONV_OFF + tt:HIST_PAD + tt, :]
    if n_t == 1:
        new_ref[i] = tail
        return

    @pl.when(t == n_t - 1)
    def _():
        new_ref[i] = tail

    @pl.when(t < n_t - 1)
    def _():
        xbuf[CONV_OFF:HIST_PAD, :] = tail


def _conv_steps_kernel(c3_ref, hist_ref, w_ref, b_ref, lg_ref, lb_ref, *refs):
    out_ref, new_ref = refs[-2:]
    nq = c3_ref.shape[0]
    glu = [c3_ref[q, :, 0:D_CONV] * _sigmoid(c3_ref[q, :, D_CONV:2 * D_CONV]) for q in range(nq)]
    row = lambda j: hist_ref[0, j] if j < CONV_HIST else glu[j - CONV_HIST]
    for q in range(nq):
        acc = jnp.zeros(glu[0].shape, F32) + b_ref[...]
        for k in range(CONV_WIDTH):
            acc = acc + row(q + k) * w_ref[k:k + 1, :]
        out_ref[q] = _conv_finish(acc, c3_ref[q, :, 2 * D_CONV:3 * D_CONV], lg_ref, lb_ref)
    for k in range(CONV_HIST):
        new_ref[0, k] = row(k + nq)


def _conv_steps(c3_qs, hist_t, layer, conv_w, conv_b, ln_g, ln_b, new_state=None):
    nq, bsz, _ = c3_qs.shape
    assert nq <= CONV_HIST
    n_seq = math.gcd(bsz, 64)
    vec = pl.BlockSpec((1, D_CONV), lambda b: (0, 0))
    state_spec = pl.BlockSpec((1, CONV_HIST, n_seq, D_CONV), lambda b: (layer, 0, b, 0))
    in_specs = [pl.BlockSpec((nq, n_seq, 3 * D_CONV), lambda b: (0, b, 0)), state_spec,
                pl.BlockSpec((CONV_WIDTH, D_CONV), lambda b: (0, 0)), vec, vec, vec]
    args = [c3_qs, hist_t, conv_w, conv_b, ln_g, ln_b]
    aliases = {}
    if new_state is not None:
        in_specs.append(pl.BlockSpec(memory_space=pl.ANY))
        aliases = {len(args): 1}
        args.append(new_state)
    return pl.pallas_call(
        _conv_steps_kernel, grid=(bsz // n_seq,), in_specs=in_specs,
        out_specs=[pl.BlockSpec((nq, n_seq, D_CONV), lambda b: (0, b, 0)), state_spec],
        out_shape=[jax.ShapeDtypeStruct((nq, bsz, D_CONV), F32), jax.ShapeDtypeStruct(hist_t.shape, F32)],
        input_output_aliases=aliases, name="conv_steps",
        compiler_params=pltpu.CompilerParams(dimension_semantics=("arbitrary",), vmem_limit_bytes=VMEM_LIMIT),
    )(*args)


HALVES_PER_PAGE = PAGE_SIZE // CMP_STRIDE
N_TG = 2 * KV_HEADS
N_LT = KV_DIM // LANES
CMP_FEAT = CMP_STRIDE * HEAD_DIM
X_PITCH = 24
SEQS_PER_STEP = 4


def _compress_body(page_tile, n_pages, pe_ref, w1_ref, w2_ref, out_ref, x_scr, y_scr, h_scr):
    n_half = n_pages * HALVES_PER_PAGE
    n_seq = out_ref.shape[0]
    n_rows = n_seq * n_half
    for i in range(n_seq):
        for p in range(n_pages):
            for t in range(N_LT):
                tile = page_tile(i, p, t).T
                for n in range(HALVES_PER_PAGE):
                    dst = (p * HALVES_PER_PAGE + n) * X_PITCH
                    x_scr[N_LT * i + t, dst:dst + CMP_STRIDE, :] = tile[n * CMP_STRIDE:(n + 1) * CMP_STRIDE, :]
        rows = slice(i * n_half, (i + 1) * n_half)
        for s in range(CMP_STRIDE):
            for t in range(N_LT):
                y_scr[t, rows, s * LANES:(s + 1) * LANES] = (
                    x_scr[N_LT * i + t, pl.ds(s, n_half, stride=X_PITCH), :].astype(BF16))
    two_h = 2 * CMP_HIDDEN
    h_scr[n_rows:n_rows + SUBLANES, :] = jnp.zeros((SUBLANES, KV_HEADS * two_h), F32)
    acts = []
    for t in range(N_LT):
        w1 = w1_ref[t]
        c = jnp.dot(pe_ref[t].astype(BF16), w1, preferred_element_type=F32)
        h_scr[0:n_rows, :] = jnp.dot(y_scr[t], w1, preferred_element_type=F32)
        for g in range(KV_HEADS):
            a0, b0 = g * two_h, g * two_h + CMP_HIDDEN
            cvec = c[0:1, a0:a0 + CMP_HIDDEN] + c[1:2, b0:b0 + CMP_HIDDEN]
            hid = h_scr[0:n_rows, a0:a0 + CMP_HIDDEN] + h_scr[1:n_rows + 1, b0:b0 + CMP_HIDDEN] + cvec
            acts.append(_silu(hid))
    out = jnp.dot(jnp.concatenate(acts, axis=1).astype(BF16), w2_ref[...], preferred_element_type=F32)
    row = lax.broadcasted_iota(jnp.int32, out.shape, 0) & (n_half - 1)
    out_ref[...] = jnp.where(row < n_half - 1, out, 0.0).reshape(n_seq, n_half, KV_DIM)


def _compress_paged_kernel(pt_ref, *refs, n_pages, n_seq):
    del pt_ref
    pages = refs[:n_seq * n_pages]

    def page_tile(i, p, t):
        return pages[i * n_pages + p][0, 0, t].reshape(LANES, PAGE_SIZE)

    _compress_body(page_tile, n_pages, *refs[n_seq * n_pages:])


def _compress_seq_kernel(kv_ref, *refs, n_pages):
    def page_tile(i, p, t):
        return kv_ref[0, i, t * LANES:(t + 1) * LANES, p * PAGE_SIZE:(p + 1) * PAGE_SIZE]

    _compress_body(page_tile, n_pages, *refs)


def _compress_specs(n_half, n_seq):
    assert n_half & (n_half - 1) == 0
    const = lambda shape: pl.BlockSpec(shape, lambda *a: (0,) * len(shape))
    feat = KV_HEADS * CMP_FEAT
    hid = KV_HEADS * 2 * CMP_HIDDEN
    weight_specs = [const((2, SUBLANES, feat)), const((2, feat, hid)), const((N_TG * CMP_HIDDEN, KV_DIM))]
    scratch = [pltpu.VMEM((n_seq * N_LT, n_half * X_PITCH, LANES), F32),
               pltpu.VMEM((N_LT, n_seq * n_half, feat), BF16),
               pltpu.VMEM((n_seq * n_half + SUBLANES, hid), F32)]
    return weight_specs, scratch


def _compress_paged(cache_t, layer, table, pe_pad, w1cat, w2big):
    bsz, n_pages = table.shape
    n_half = n_pages * HALVES_PER_PAGE
    n_seq = math.gcd(bsz, SEQS_PER_STEP)
    page_specs = [pl.BlockSpec((1, 1, 2, KV_HEADS, HEAD_DIM, PAGE_SIZE),
                               lambda b, pt, i=i, p=p: (layer, pt[b * n_seq + i, p], 0, 0, 0, 0))
                  for i in range(n_seq) for p in range(n_pages)]
    weight_specs, scratch = _compress_specs(n_half, n_seq)
    out_spec = pl.BlockSpec((n_seq, n_half, KV_DIM), lambda b, pt: (b, 0, 0))
    grid_spec = pltpu.PrefetchScalarGridSpec(
        num_scalar_prefetch=1, grid=(bsz // n_seq,), in_specs=page_specs + weight_specs, out_specs=out_spec,
        scratch_shapes=scratch)
    return pl.pallas_call(
        functools.partial(_compress_paged_kernel, n_pages=n_pages, n_seq=n_seq), grid_spec=grid_spec,
        out_shape=jax.ShapeDtypeStruct((bsz, n_half, KV_DIM), F32), name="compress_paged",
        compiler_params=pltpu.CompilerParams(dimension_semantics=("arbitrary",), vmem_limit_bytes=VMEM_LIMIT),
    )(table, *([cache_t] * (n_seq * n_pages)), pe_pad, w1cat, w2big)


def _compress_seq(kv_t, layer, pe_pad, w1cat, w2big):
    _, bsz, _, seq = kv_t.shape
    n_pages = seq // PAGE_SIZE
    n_half = n_pages * HALVES_PER_PAGE
    weight_specs, scratch = _compress_specs(n_half, 1)
    out_spec = pl.BlockSpec((1, n_half, KV_DIM), lambda b: (b, 0, 0))
    return pl.pallas_call(
        functools.partial(_compress_seq_kernel, n_pages=n_pages), grid=(bsz,),
        in_specs=[pl.BlockSpec((1, 1, KV_DIM, seq), lambda b: (layer, b, 0, 0))] + weight_specs, out_specs=out_spec,
        scratch_shapes=scratch,
        out_shape=jax.ShapeDtypeStruct((bsz, n_half, KV_DIM), F32), name="compress_seq",
        compiler_params=pltpu.CompilerParams(dimension_semantics=("arbitrary",), vmem_limit_bytes=VMEM_LIMIT),
    )(kv_t, pe_pad, w1cat, w2big)


def _head_rows(q, g, low):
    parts = []
    for r in range(GROUP):
        h = GROUP * g + r
        tile = q[:, LANES * (h // 2):LANES * (h // 2 + 1)]
        if (h % 2) != g:
            tile = pltpu.roll(tile, HALF, 1)
        parts.append(jnp.where(low, tile, 0.0) if g == 0 else jnp.where(low, 0.0, tile))
    return parts


def _assemble_heads(o_heads, low):
    tiles = []
    for j in range(N_HEADS // 2):
        a, b = o_heads[2 * j], o_heads[2 * j + 1]
        if (2 * j) // GROUP == 0:
            tiles.append(jnp.where(low, a, pltpu.roll(b, HALF, 1)))
        else:
            tiles.append(jnp.where(low, pltpu.roll(a, HALF, 1), b))
    return jnp.concatenate(tiles, axis=1)


def _select_blocks_t(score_t, allowed_t, n_blk, top_n):
    idx = lax.broadcasted_iota(jnp.int32, score_t.shape, 0)
    cnt = jnp.zeros(score_t.shape, jnp.int32)
    for i in range(n_blk):
        row = score_t[i:i + 1, :]
        ahead = (row > score_t) | ((row == score_t) & (idx > i))
        cnt = cnt + ahead.astype(jnp.int32)
    return (cnt < top_n) & allowed_t


def _flash_step(q_t, k_tile, v_tile, bias, mask, state, keys=None):
    m, l, acc = state
    s = jnp.dot(k_tile, q_t, preferred_element_type=F32)
    mask_add = [jnp.where(mask(c), 0.0, NEG_INF) for c in range(TQ // LANES)]
    ms, ls, ps, alphas = [], [], [], []
    for j in range(N_CHUNK):
        r, c = divmod(j, TQ // LANES)
        k0, k1 = keys(c) if keys is not None else (0, TK)
        m_add = mask_add[c] if mask_add[c].ndim == 0 else mask_add[c][k0:k1]
        s_j = s[k0:k1, j * LANES:(j + 1) * LANES] + bias(r, pl.ds(c * LANES, LANES), (k0, k1)) + m_add
        m_j = jnp.maximum(m[j], jnp.max(s_j, axis=0, keepdims=True))
        m_use = jnp.where(m_j > 0.5 * NEG_INF, m_j, 0.0)
        alpha = jnp.exp2(m[j] - m_use)
        p_j = jnp.exp2(s_j - m_use)
        ls.append(alpha * l[j] + jnp.sum(p_j, axis=0, keepdims=True))
        ms.append(m_j)
        alphas.append(alpha)
        parts = [p_j.astype(BF16)]
        if k0 > 0:
            parts.insert(0, jnp.zeros((k0, LANES), BF16))
        if k1 < TK:
            parts.append(jnp.zeros((TK - k1, LANES), BF16))
        ps.append(parts[0] if len(parts) == 1 else jnp.concatenate(parts, axis=0))
    pv = jnp.dot(v_tile, jnp.concatenate(ps, axis=1), preferred_element_type=F32)
    acc = jnp.concatenate(alphas, axis=1) * acc + pv
    return tuple(ms), tuple(ls), acc


N_CHUNK = GROUP * TQ // LANES


def _flash_init():
    return (tuple(jnp.full((1, LANES), NEG_INF, F32) for _ in range(N_CHUNK)),
            tuple(jnp.zeros((1, LANES), F32) for _ in range(N_CHUNK)),
            jnp.zeros((LANES, GROUP * TQ), F32))


def _flash_out(state):
    _, l, acc = state
    return acc * (1.0 / jnp.maximum(jnp.concatenate(l, axis=1), TINY))


def _attn_prompt_kernel(q_ref, gate_ref, z_ref, kcvc_ref, ks_ref, vs_ref, kw_ref, vw_ref, nb_ref, cb_ref, fb_ref,
                        cover_ref, out_ref, *, n_slc):
    qt = pl.program_id(1)
    nc = kcvc_ref.shape[1]
    ki = lax.broadcasted_iota(jnp.int32, (TK, LANES), 0)
    qi = lax.broadcasted_iota(jnp.int32, (TK, LANES), 1)
    causal = lambda c: qi + c * LANES >= ki
    causal_keys = lambda c: (0, (c + 1) * LANES)
    window_edge = lambda c: (ki > qi + c * LANES) & (qt >= 2)
    edge_keys = lambda c: (c * LANES, TK)
    q_t = q_ref[...].T
    g_t = _sigmoid(gate_ref[...]).T
    kc = kcvc_ref[0, :, 0:LANES].astype(BF16)
    vc_t = kcvc_ref[0, :, LANES:2 * LANES].T.astype(BF16)
    c_end = lax.broadcasted_iota(jnp.int32, (nc, TQ), 0) * CMP_STRIDE + (CMP_LEN - 1)
    cmask = qt * TQ + lax.broadcasted_iota(jnp.int32, (nc, TQ), 1) >= c_end
    blk_t = lax.broadcasted_iota(jnp.int32, (n_slc, TQ), 0)
    cur_t = jnp.right_shift(qt * TQ + lax.broadcasted_iota(jnp.int32, (n_slc, TQ), 1), SLC_SHIFT)
    allowed_t = blk_t <= cur_t
    forced_t = (blk_t == 0) | (blk_t == cur_t) | (blk_t == cur_t - 1)
    e_row = jnp.right_shift(lax.broadcasted_iota(jnp.int32, (TK, LANES), 0), SLC_SHIFT)
    e_col = lax.broadcasted_iota(jnp.int32, (TK, LANES), 1)
    kt_prev = jnp.maximum(qt - 1, 0)
    kt_far = jnp.maximum(qt - 2, 0)
    zero_half = jnp.zeros((HEAD_DIM, TQ), F32)

    def kv_tile(k_ref, v_ref, kt):
        start = pl.multiple_of(kt * TK, TK)
        return k_ref[pl.ds(start, TK), :], v_ref[0, :, pl.ds(start, TK)]

    pieces = []
    for g in range(KV_HEADS):
        heads = [GROUP * g + r for r in range(GROUP)]
        parts = []
        for h in heads:
            x = q_t[h * HEAD_DIM:(h + 1) * HEAD_DIM, :] * (SCALE * LOG2E)
            parts.append(jnp.concatenate([x, zero_half] if g == 0 else [zero_half, x], axis=0))
        qg = jnp.concatenate(parts, axis=1).astype(BF16)
        s_c = jnp.dot(kc, qg, preferred_element_type=F32)
        p_parts, p_sum = [], None
        for r, h in enumerate(heads):
            s_r = jnp.where(cmask, s_c[:, r * TQ:(r + 1) * TQ] + cb_ref[0, h], NEG_INF)
            e = jnp.where(cmask, jnp.exp2(s_r - jnp.max(s_r, axis=0, keepdims=True)), 0.0)
            p_r = e * (1.0 / jnp.maximum(jnp.sum(e, axis=0, keepdims=True), TINY))
            p_sum = p_r if p_sum is None else p_sum + p_r
            p_parts.append(p_r.astype(BF16))
        o_cmp = jnp.dot(vc_t, jnp.concatenate(p_parts, axis=1), preferred_element_type=F32)
        hi = p_sum.astype(BF16)
        lo = (p_sum - hi.astype(F32)).astype(BF16)
        imp_t = (jnp.dot(cover_ref[...], hi, preferred_element_type=F32)
                 + jnp.dot(cover_ref[...], lo, preferred_element_type=F32))
        score_t = jnp.where(allowed_t, jnp.where(forced_t, FORCED_SCORE, imp_t[0:n_slc]), MASKED_SCORE)
        sel_t = _select_blocks_t(score_t, allowed_t, n_slc, min(TOP_N, n_slc)).astype(F32)
        sel_pad = jnp.concatenate([sel_t, jnp.zeros((LANES - n_slc, TQ), F32)], axis=0).astype(BF16)

        def sel_mask(kt, extra=None):
            expand = (e_row + kt * (TK // SLC_BLOCK) == e_col).astype(BF16)
            hit = jnp.dot(expand, sel_pad, preferred_element_type=F32)

            def mask(c):
                m = hit[:, c * LANES:(c + 1) * LANES] > 0.5
                return m if extra is None else m & extra(c)

            return mask

        far_bias = lambda r, qs, ks: fb_ref[heads[r], 0:1, qs]
        prev_bias = lambda r, qs, ks: nb_ref[heads[r], ks[0]:ks[1], qs]
        diag_bias = lambda r, qs, ks: nb_ref[heads[r], TK + ks[0]:TK + ks[1], qs]

        def far_body(kt, state):
            k_t, v_t = kv_tile(ks_ref, vs_ref, kt)
            return _flash_step(qg, k_t, v_t, far_bias, sel_mask(kt), state)

        state = lax.fori_loop(0, jnp.maximum(qt - 1, 0), far_body, _flash_init())
        k_t, v_t = kv_tile(ks_ref, vs_ref, kt_prev)
        state = _flash_step(qg, k_t, v_t, prev_bias, sel_mask(kt_prev, lambda c: qt >= 1), state)
        k_t, v_t = kv_tile(ks_ref, vs_ref, qt)
        state = _flash_step(qg, k_t, v_t, diag_bias, sel_mask(qt, causal), state, causal_keys)
        o_slc = _flash_out(state)
        state = _flash_init()
        k_t, v_t = kv_tile(kw_ref, vw_ref, kt_far)
        state = _flash_step(qg, k_t, v_t, far_bias, window_edge, state, edge_keys)
        k_t, v_t = kv_tile(kw_ref, vw_ref, kt_prev)
        state = _flash_step(qg, k_t, v_t, prev_bias, lambda c: qt >= 1, state)
        k_t, v_t = kv_tile(kw_ref, vw_ref, qt)
        state = _flash_step(qg, k_t, v_t, diag_bias, causal, state, causal_keys)
        o_win = _flash_out(state)
        rows = slice(g * HEAD_DIM, (g + 1) * HEAD_DIM)
        for r, h in enumerate(heads):
            cols = slice(r * TQ, (r + 1) * TQ)
            pieces.append(g_t[h:h + 1, :] * o_cmp[rows, cols]
                          + g_t[N_HEADS + h:N_HEADS + h + 1, :] * o_slc[rows, cols]
                          + g_t[2 * N_HEADS + h:2 * N_HEADS + h + 1, :] * o_win[rows, cols])
    out_ref[...] = jnp.concatenate(pieces, axis=0).T * _silu(z_ref[...])


def _attn_prompt(q2d, gate2d, z2d, kcvc, ks, vs_t, kw, vw_t, nb, cb, fb, cover_t, *, bsz, seq):
    assert WINDOW == 2 * TK and seq % TQ == 0
    n_qt = seq // TQ
    nc = seq // CMP_STRIDE
    n_slc = seq // SLC_BLOCK
    kern = functools.partial(_attn_prompt_kernel, n_slc=n_slc)
    tok = lambda w: pl.BlockSpec((TQ, w), lambda b, t: (b * n_qt + t, 0))
    k_spec = pl.BlockSpec((seq, LANES), lambda b, t: (b, 0))
    v_spec = pl.BlockSpec((1, LANES, seq), lambda b, t: (b, 0, 0))
    return pl.pallas_call(
        kern, grid=(bsz, n_qt),
        in_specs=[tok(D_ATTN), tok(LANES), tok(D_ATTN),
                  pl.BlockSpec((1, nc, KV_DIM), lambda b, t: (b, 0, 0)), k_spec, v_spec, k_spec, v_spec,
                  pl.BlockSpec((N_HEADS, 2 * TK, TQ), lambda b, t: (0, 0, 0)),
                  pl.BlockSpec((1, N_HEADS, nc, TQ), lambda b, t: (t, 0, 0, 0)),
                  pl.BlockSpec((N_HEADS, SUBLANES, TQ), lambda b, t: (0, 0, 0)),
                  pl.BlockSpec((LANES, nc), lambda b, t: (0, 0))],
        out_specs=tok(D_ATTN),
        out_shape=jax.ShapeDtypeStruct((bsz * seq, D_ATTN), F32),
        name="attn_prompt",
        compiler_params=pltpu.CompilerParams(dimension_semantics=("arbitrary", "arbitrary"),
                                             vmem_limit_bytes=VMEM_LIMIT),
    )(q2d, gate2d, z2d, kcvc, ks, vs_t, kw, vw_t, nb, cb, fb, cover_t)


def _attn_sample_kernel(pt_ref, *refs, n_pages, nq, n_seq, aliased_state):
    del pt_ref
    shared = list(refs[n_seq * n_pages:])
    if aliased_state:
        del shared[-3]
    for i in range(n_seq):
        _attn_sample_one(i, refs[i * n_pages:(i + 1) * n_pages], *shared, n_pages=n_pages, nq=nq)


def _attn_sample_one(i, pages, win_ref, kvsn_ref, kvwn_ref, q_ref, gate_ref, z_ref, kcvc_ref, sba_ref, swa_ref,
                     sbb_ref, sc_ref, cover_ref, expand_ref, out_ref, nwin_ref, *, n_pages, nq):
    past = n_pages * PAGE_SIZE
    wbuf = win_ref.shape[-1]
    nc = kcvc_ref.shape[1]
    rows = KV_HEADS * GROUP * nq
    cur = past // SLC_BLOCK
    n_slc = cur + 1
    low = lax.broadcasted_iota(jnp.int32, (nq, LANES), 1) < HALF
    q = q_ref[i]
    q_left = (jnp.concatenate(_head_rows(q, 0, low) + _head_rows(q, 1, low), axis=0) * SCALE).astype(BF16)
    qi = lax.broadcasted_iota(jnp.int32, (rows, LANES), 0) & (nq - 1)
    ki = lax.broadcasted_iota(jnp.int32, (rows, LANES), 1)
    new_mask = (ki <= qi) & (ki < nq)
    pad_rows = jnp.zeros((LANES - nq, LANES), F32)

    def new_tile(ref, t):
        return jnp.concatenate([ref[i, :, t * LANES:(t + 1) * LANES], pad_rows], axis=0)

    kc = kcvc_ref[i, :, 0:LANES].astype(BF16)
    vc = kcvc_ref[i, :, LANES:2 * LANES].astype(BF16)
    n_idx = lax.broadcasted_iota(jnp.int32, (rows, nc), 1)
    p_c = _masked_softmax(_dot_nt(q_left, kc) + sc_ref[...], n_idx < nc - 1)
    o_cmp = _dot(p_c, vc)
    blk = lax.broadcasted_iota(jnp.int32, (nq, LANES), 1)
    is_blk = blk < n_slc
    forced = (blk == 0) | (blk == cur) | (blk == cur - 1)
    sel_rows = []
    for g in range(KV_HEADS):
        p_sum = p_c[g * GROUP * nq:g * GROUP * nq + nq]
        for r in range(1, GROUP):
            p_sum = p_sum + p_c[(g * GROUP + r) * nq:(g * GROUP + r + 1) * nq]
        imp = _dot_split(p_sum, cover_ref[...])
        score = jnp.where(is_blk, jnp.where(forced, FORCED_SCORE, imp), PAD_SCORE)
        cnt = jnp.zeros((nq, LANES), jnp.int32)
        for j in range(n_slc):
            col = score[:, j:j + 1]
            cnt = cnt + ((col > score) | ((col == score) & (blk > j))).astype(jnp.int32)
        sel_g = ((cnt < min(TOP_N, n_slc)) & is_blk).astype(F32)
        sel_rows += [sel_g] * GROUP
    sel = jnp.concatenate(sel_rows, axis=0)
    k_pages = [pages[p][0, 0, 0].reshape(LANES, PAGE_SIZE).astype(BF16) for p in range(n_pages)]
    v_pages = [pages[p][0, 0, 1].reshape(LANES, PAGE_SIZE).astype(BF16) for p in range(n_pages)]
    k_new = new_tile(kvsn_ref, 0).astype(BF16)
    v_new = new_tile(kvsn_ref, 1).astype(BF16)
    s_a = jnp.concatenate([_dot(q_left, k) for k in k_pages], axis=1) + sba_ref[...]
    mask_a = jnp.dot(sel.astype(BF16), expand_ref[...], preferred_element_type=F32) > 0.5
    s_b = _dot_nt(q_left, k_new) + sbb_ref[...]
    mask_b = new_mask & (sel[:, cur:cur + 1] > 0.5)
    s_a = jnp.where(mask_a, s_a, NEG_INF)
    s_b = jnp.where(mask_b, s_b, NEG_INF)
    m = jnp.maximum(jnp.max(s_a, axis=-1, keepdims=True), jnp.max(s_b, axis=-1, keepdims=True))
    p_a = jnp.exp(s_a - m) * mask_a.astype(F32)
    p_b = jnp.exp(s_b - m) * mask_b.astype(F32)
    l = jnp.sum(p_a, axis=-1, keepdims=True) + jnp.sum(p_b, axis=-1, keepdims=True)
    acc = _dot(p_b, v_new)
    for p in range(n_pages):
        acc = acc + _dot_nt(p_a[:, p * PAGE_SIZE:(p + 1) * PAGE_SIZE], v_pages[p])
    o_slc = acc / jnp.maximum(l, TINY)
    win_t = [win_ref[0, i, t].reshape(LANES, wbuf) for t in range(2)]
    new_w = [new_tile(kvwn_ref, t) for t in range(2)]
    lane_w = lax.broadcasted_iota(jnp.int32, (LANES, wbuf), 1)
    for t in range(2):
        placed = jnp.concatenate([jnp.zeros((LANES, wbuf - LANES), F32), pltpu.roll(new_w[t].T, LANES - nq, 1)],
                                 axis=1)
        shifted = pltpu.roll(win_t[t], wbuf - nq, 1)
        nwin_ref[0, i, t] = jnp.where(lane_w < wbuf - nq, shifted, placed).reshape(KV_HEADS, HEAD_DIM, wbuf)
    kw_t, vw_t = win_t[0].astype(BF16), win_t[1].astype(BF16)
    kw_new, vw_new = new_w[0].astype(BF16), new_w[1].astype(BF16)
    jw = lax.broadcasted_iota(jnp.int32, (rows, wbuf), 1)
    qw = lax.broadcasted_iota(jnp.int32, (rows, wbuf), 0) & (nq - 1)
    mask_wa = jw > qw
    s_wa = jnp.where(mask_wa, _dot(q_left, kw_t) + swa_ref[...], NEG_INF)
    s_wb = jnp.where(new_mask, _dot_nt(q_left, kw_new) + sbb_ref[...], NEG_INF)
    m = jnp.maximum(jnp.max(s_wa, axis=-1, keepdims=True), jnp.max(s_wb, axis=-1, keepdims=True))
    p_wa = jnp.exp(s_wa - m) * mask_wa.astype(F32)
    p_wb = jnp.exp(s_wb - m) * new_mask.astype(F32)
    l = jnp.sum(p_wa, axis=-1, keepdims=True) + jnp.sum(p_wb, axis=-1, keepdims=True)
    acc = _dot_nt(p_wa, vw_t) + _dot(p_wb, vw_new)
    o_win = acc / jnp.maximum(l, TINY)
    gsig = _sigmoid(gate_ref[i])
    o_heads = []
    for h in range(N_HEADS):
        rs = slice(h * nq, (h + 1) * nq)
        o_heads.append(gsig[:, h:h + 1] * o_cmp[rs] + gsig[:, N_HEADS + h:N_HEADS + h + 1] * o_slc[rs]
                       + gsig[:, 2 * N_HEADS + h:2 * N_HEADS + h + 1] * o_win[rs])
    out_ref[i] = _assemble_heads(o_heads, low) * _silu(z_ref[i])


def _attn_sample(cache_t, win_t, layer, table, kvs_new, kvw_new, q, gate, z, kcvc, sba, swa, sbb, sc, cover, expand,
                 new_state=None):
    bsz, n_pages = table.shape
    nq = q.shape[1]
    wbuf = win_t.shape[-1]
    nc = kcvc.shape[1]
    past = n_pages * PAGE_SIZE
    rows = N_HEADS * nq
    assert nq <= SLC_BLOCK and nq & (nq - 1) == 0 and past % SLC_BLOCK == 0 and wbuf == WINDOW
    n_seq = math.gcd(bsz, SEQS_PER_STEP)
    aliased_state = new_state is not None
    kern = functools.partial(_attn_sample_kernel, n_pages=n_pages, nq=nq, n_seq=n_seq, aliased_state=aliased_state)
    page_specs = [pl.BlockSpec((1, 1, 2, KV_HEADS, HEAD_DIM, PAGE_SIZE),
                               lambda b, pt, i=i, p=p: (layer, pt[b * n_seq + i, p], 0, 0, 0, 0))
                  for i in range(n_seq) for p in range(n_pages)]
    per_b = lambda r, w: pl.BlockSpec((n_seq, r, w), lambda b, pt: (b, 0, 0))
    const = lambda r, w: pl.BlockSpec((r, w), lambda b, pt: (0, 0))
    state_spec = pl.BlockSpec((1, n_seq, 2, KV_HEADS, HEAD_DIM, wbuf), lambda b, pt: (layer, b, 0, 0, 0, 0))
    in_specs = page_specs + [
        state_spec, per_b(nq, KV_DIM), per_b(nq, KV_DIM), per_b(nq, D_ATTN), per_b(nq, LANES),
        per_b(nq, D_ATTN), per_b(nc, KV_DIM),
        const(rows, past), const(rows, wbuf), const(rows, LANES), const(rows, nc),
        const(nc, LANES), const(LANES, past)]
    args = [table, *([cache_t] * (n_seq * n_pages)), win_t, kvs_new, kvw_new, q, gate, z, kcvc, sba, swa, sbb, sc,
            cover, expand]
    aliases = {}
    if aliased_state:
        in_specs.append(pl.BlockSpec(memory_space=pl.ANY))
        aliases = {len(args): 1}
        args.append(new_state)
    grid_spec = pltpu.PrefetchScalarGridSpec(
        num_scalar_prefetch=1, grid=(bsz // n_seq,), in_specs=in_specs,
        out_specs=[per_b(nq, D_ATTN), state_spec])
    return pl.pallas_call(
        kern, grid_spec=grid_spec,
        out_shape=[jax.ShapeDtypeStruct((bsz, nq, D_ATTN), F32),
                   jax.ShapeDtypeStruct(win_t.shape, F32)],
        input_output_aliases=aliases, name="attn_sample",
        compiler_params=pltpu.CompilerParams(dimension_semantics=("arbitrary",), vmem_limit_bytes=VMEM_LIMIT),
    )(*args)


def _out_proj_kernel(h_ref, conv_ref, attn_ref, ple_ref, wo_ref, wg_ref, wp_ref, fg_ref, out_ref, *, final):
    h = h_ref[...]
    h = h + jnp.dot(conv_ref[...].astype(BF16), wo_ref[0:D_CONV, :], preferred_element_type=F32)
    h = h + jnp.dot(attn_ref[...].astype(BF16), wo_ref[D_CONV:D_CONV + D_ATTN, :], preferred_element_type=F32)
    gate = _sigmoid(jnp.dot(h.astype(BF16), wg_ref[...], preferred_element_type=F32))
    h = h + gate * jnp.dot(ple_ref[0].astype(BF16), wp_ref[...], preferred_element_type=F32)
    if final:
        h = h * lax.rsqrt(jnp.mean(h * h, axis=-1, keepdims=True) + EPS) * fg_ref[...]
    out_ref[...] = h


def _out_proj(h2d, conv2d, attn2d, ple3d, layer, wo, wg, wp, fg, *, final):
    n = h2d.shape[0]
    tm = min(TM, n)
    ple_dim = ple3d.shape[-1]
    kern = functools.partial(_out_proj_kernel, final=final)
    tok = lambda w: pl.BlockSpec((tm, w), lambda i: (i, 0))
    const = lambda r, w: pl.BlockSpec((r, w), lambda i: (0, 0))
    return pl.pallas_call(
        kern, grid=(n // tm,),
        in_specs=[tok(D_MODEL), tok(D_CONV), tok(D_ATTN), pl.BlockSpec((1, tm, ple_dim), lambda i: (layer, i, 0)),
                  const(D_CONV + D_ATTN, D_MODEL), const(D_MODEL, D_MODEL), const(ple_dim, D_MODEL),
                  const(1, D_MODEL)],
        out_specs=tok(D_MODEL),
        out_shape=jax.ShapeDtypeStruct((n, D_MODEL), F32),
        name="out_proj",
        compiler_params=pltpu.CompilerParams(dimension_semantics=("arbitrary",), vmem_limit_bytes=VMEM_LIMIT),
    )(h2d, conv2d, attn2d, ple3d, wo, wg, wp, fg)


def _cover_matrix(n_cmp_rows, n_cmp, n_slc):
    c_start = np.arange(n_cmp_rows) * CMP_STRIDE
    c_end = c_start + CMP_LEN - 1
    s_start = np.arange(LANES) * SLC_BLOCK
    cover = (c_start[:, None] < s_start[None, :] + SLC_BLOCK) & (c_end[:, None] >= s_start[None, :])
    cover &= (np.arange(n_cmp_rows)[:, None] < n_cmp) & (np.arange(LANES)[None, :] < n_slc)
    return jnp.asarray(cover, dtype=BF16)


def _expand_matrix(past):
    e = np.arange(LANES)[:, None] == (np.arange(past)[None, :] // SLC_BLOCK)
    return jnp.asarray(e, dtype=BF16)


def kernel(x_prompt, x_sample, cache_cmp_kv, cache_slc_kv, page_table, state_win_kv, state_conv, p_prompt, p_sample, norm_g, w_in, conv_w, conv_b, conv_ln_g, conv_ln_b, cmp_pe, cmp_w1, cmp_w2, w_out, w_ple, w_ple_gate, rel_bias, final_norm_g):
    bp, seq, _ = x_prompt.shape
    bs, nq, _ = x_sample.shape
    depth = w_in.shape[0]
    n_pages = page_table.shape[1]
    past = n_pages * PAGE_SIZE
    wbuf = state_win_kv.shape[2]
    win_p = min(WINDOW, seq)

    nb, cb, fb, sba, swa, sbb, sc = _bias_tables(rel_bias, seq=seq, past=past, wbuf=wbuf, nq_s=nq)
    rows = N_HEADS * nq
    sba, swa, sbb, sc = (a.reshape(rows, a.shape[-1]) for a in (sba, swa, sbb, sc))
    nc_p, nc_s = seq // CMP_STRIDE, past // CMP_STRIDE
    cover_p = _cover_matrix(nc_p, nc_p - 1, seq // SLC_BLOCK)
    cover_s = _cover_matrix(nc_s, nc_s - 1, past // SLC_BLOCK + 1)
    expand_s = _expand_matrix(past)
    conv_zero = jnp.zeros((bp, CONV_HIST, D_CONV), F32)
    fg = final_norm_g.reshape(1, D_MODEL)
    to_t = lambda a: jnp.transpose(a, (0, 1, 3, 4, 5, 2))
    from_t = lambda a: jnp.transpose(a, (0, 1, 5, 2, 3, 4))
    cmp_t, slc_t, win_t = to_t(cache_cmp_kv), to_t(cache_slc_kv), to_t(state_win_kv)
    ple_p = p_prompt.reshape(depth, bp * seq, -1)
    ple_s = p_sample.reshape(depth, bs * nq, -1)

    hp = x_prompt.reshape(bp * seq, D_MODEL)
    hs = x_sample.reshape(bs * nq, D_MODEL)
    outs = [[] for _ in range(8)]
    new_win = kv_prev = conv_state = None
    conv_hist_t = jnp.transpose(state_conv, (0, 2, 1, 3))
    for i in range(depth):
        w_pad = jnp.pad(w_in[i], ((0, 0), (0, D_IN_PAD - D_IN))).astype(BF16)
        wkv_t = w_in[i][:, SEC_KVC[0]:SEC_KVW[1]].T.astype(BF16)
        g = norm_g[i].reshape(1, D_MODEL)
        w1 = jnp.transpose(cmp_w1[i].reshape(2, 2, CMP_STRIDE, HEAD_DIM, CMP_HIDDEN), (0, 2, 3, 1, 4))
        w1 = w1.reshape(2, CMP_STRIDE, HEAD_DIM, 2 * CMP_HIDDEN)
        w1cat = jnp.stack([jnp.concatenate([w1, jnp.zeros_like(w1)], axis=-1),
                           jnp.concatenate([jnp.zeros_like(w1), w1], axis=-1)], axis=2)
        w1cat = w1cat.reshape(2, KV_HEADS * CMP_FEAT, KV_HEADS * 2 * CMP_HIDDEN).astype(BF16)
        pe = cmp_pe[i].reshape(2, 2, CMP_STRIDE, 1, HEAD_DIM)
        pe = jnp.broadcast_to(pe, (2, 2, CMP_STRIDE, KV_HEADS, HEAD_DIM)).reshape(2, 2, KV_HEADS * CMP_FEAT)
        pe_pad = jnp.pad(pe, ((0, 0), (0, SUBLANES - 2), (0, 0)))
        w2big = jnp.zeros((N_TG * CMP_HIDDEN, KV_DIM), F32)
        for tg in range(N_TG):
            w2big = w2big.at[tg * CMP_HIDDEN:(tg + 1) * CMP_HIDDEN, tg * HEAD_DIM:(tg + 1) * HEAD_DIM].set(
                cmp_w2[i, tg // KV_HEADS])
        w2big = w2big.astype(BF16)
        wo, wg, wp = w_out[i].astype(BF16), w_ple_gate[i].astype(BF16), w_ple[i].astype(BF16)
        cw, cbias = conv_w[i], conv_b[i].reshape(1, D_CONV)
        lg, lb = conv_ln_g[i].reshape(1, D_CONV), conv_ln_b[i].reshape(1, D_CONV)
        final = i == depth - 1

        conv_out, q, kvc_t, kvs_t, kvw_t, z, gate, ks, vs_t, kw, vw_t, new_conv = _in_proj(
            hp, g, w_pad, wkv_t, (conv_zero, cw, cbias, lg, lb), seq=seq, layer=i, depth=depth, kv_prev=kv_prev)
        kv_prev = (kvc_t, kvs_t, kvw_t)
        kcvc = _compress_seq(kvc_t, i, pe_pad, w1cat, w2big)
        attn = _attn_prompt(q, gate, z, kcvc, ks, vs_t, kw, vw_t, nb, cb, fb, cover_p.T, bsz=bp, seq=seq)
        hp = _out_proj(hp, conv_out.reshape(bp * seq, D_CONV), attn, ple_p, i, wo, wg, wp, fg, final=final)
        outs[6].append(new_conv)

        c3, q, kvc, kvs, kvw, z, gate = _in_proj(hs, g, w_pad)
        conv_qs, conv_state = _conv_steps(jnp.transpose(c3.reshape(bs, nq, 3 * D_CONV), (1, 0, 2)), conv_hist_t, i,
                                          cw, cbias, lg, lb, new_state=conv_state)
        conv_out = jnp.transpose(conv_qs, (1, 0, 2))
        kcvc = _compress_paged(cmp_t, i, page_table, pe_pad, w1cat, w2big)
        attn, new_win = _attn_sample(
            slc_t, win_t, i, page_table, kvs.reshape(bs, nq, KV_DIM), kvw.reshape(bs, nq, KV_DIM),
            q.reshape(bs, nq, D_ATTN), gate.reshape(bs, nq, LANES), z.reshape(bs, nq, D_ATTN), kcvc,
            sba, swa, sbb, sc, cover_s, expand_s, new_state=new_win)
        hs = _out_proj(hs, conv_out.reshape(bs * nq, D_CONV), attn.reshape(bs * nq, D_ATTN), ple_s, i,
                       wo, wg, wp, fg, final=final)
        outs[1].append(kvc.reshape(bs, nq, 2, KV_HEADS, HEAD_DIM))
        outs[3].append(kvs.reshape(bs, nq, 2, KV_HEADS, HEAD_DIM))

    six_d = lambda a: a.reshape(depth, bp, 2, KV_HEADS, HEAD_DIM, a.shape[-1])
    kvc_t, kvs_t, kvw_t = kv_prev
    whole = {0: six_d(kvc_t), 2: six_d(kvs_t), 4: six_d(kvw_t[..., seq - win_p:]), 5: new_win,
             7: jnp.transpose(conv_state, (0, 2, 1, 3))}
    stacked = [whole[k] if k in whole else jnp.stack(o) for k, o in enumerate(outs)]
    for k in (0, 2, 4, 5):
        stacked[k] = from_t(stacked[k])
    return (hp.reshape(bp, seq, D_MODEL), hs.reshape(bs, nq, D_MODEL)) + tuple(stacked)
```
